```python
import jax, jax.numpy as jnp
from jax import lax
import numpy as np

D_MODEL = 2048
BATCH = 8
SEQ = 2048
DEPTH = 2

D_MIX = D_MODEL
POOL_WIDTH = D_MIX // 4
POOL_WINDOWS = (2, 4, 8, 16)
POOL_GROUPS = len(POOL_WINDOWS)
POOL_GROUP_DIM = POOL_WIDTH // POOL_GROUPS
CONV_WIDTH = D_MIX // 4
CONV_HEADS = 4
CONV_K = 3
NSA_WIDTH = D_MIX - POOL_WIDTH - CONV_WIDTH
HEAD_DIM = 128
NSA_HEADS = NSA_WIDTH // HEAD_DIM
NSA_KV_HEADS = 2
NSA_GROUP = NSA_HEADS // NSA_KV_HEADS
KV_DIM = NSA_KV_HEADS * HEAD_DIM
N_BRANCH = 3
CMP_LEN = 32
CMP_STRIDE = 16
CMP_HIDDEN = 256
SEL_LEN = 64
N_SELECT = 16
SEL_FORCE = 1.0e4
WINDOW = 512
WIN_Q_BLOCK = 128
SEL_Q_BLOCK = 32
D_IN = POOL_WIDTH + 3 * CONV_WIDTH + NSA_WIDTH + 6 * KV_DIM + N_BRANCH * NSA_HEADS
N_GROUPS_MOE = 4
EXPERTS_PER_GROUP = 8
N_EXPERTS = N_GROUPS_MOE * EXPERTS_PER_GROUP
D_EXPERT = 256
TOP_K_EXPERTS = 2
EPS = 1e-6
NEG = -1e30

kernel_name = 'hybrid_pool_conv_nsa_hmoe'


def rmsnorm(x, w):
    xf = x.astype(jnp.float32)
    y = xf * lax.rsqrt(jnp.mean(xf * xf, axis=-1, keepdims=True) + EPS)
    return (y * w.astype(jnp.float32)).astype(x.dtype)


def pool_mixer(u, pool_w, pool_scale):
    B_, S, _ = u.shape
    ug = u.reshape(B_, S, POOL_GROUPS, POOL_GROUP_DIM).astype(jnp.float32)
    cs = jnp.cumsum(ug, axis=1)
    t = jnp.arange(S)
    outs = []
    for g, w in enumerate(POOL_WINDOWS):
        prev = jnp.pad(cs[:, :, g], ((0, 0), (w, 0), (0, 0)))[:, :S]
        cnt = jnp.minimum(t + 1, w).astype(jnp.float32)[None, :, None]
        outs.append((cs[:, :, g] - prev) / cnt)
    pooled = jnp.stack(outs, axis=2)
    mixed = (pooled - ug).astype(u.dtype)
    y = jnp.einsum('bsgc,gcd->bsgd', mixed, pool_w).reshape(B_, S, POOL_WIDTH)
    return y * pool_scale


def conv_mixer(b, c, v, conv_w):
    S = v.shape[1]
    u = c * v
    up = jnp.pad(u, ((0, 0), (CONV_K - 1, 0), (0, 0)))
    y = conv_w[0] * up[:, 0:S]
    for k in range(1, CONV_K):
        y = y + conv_w[k] * up[:, k:k + S]
    return b * y


def compress(kv, pe, w1, w2):
    S = kv.shape[1]
    n_c = (S - CMP_LEN) // CMP_STRIDE + 1
    idx = np.arange(n_c)[:, None] * CMP_STRIDE + np.arange(CMP_LEN)[None, :]
    blk = kv[:, idx] + pe[None, None, :, None, :]
    h = jax.nn.silu(jnp.einsum('bnlhd,lde->bnhe', blk, w1))
    return jnp.einsum('bnhe,ed->bnhd', h, w2)


def nsa_mixer(zc, q_norm_w, k_norm_w, pe_k, w1_k, w2_k, pe_v, w1_v, w2_v):
    B_, S, _ = zc.shape
    Hkv, G, dk = NSA_KV_HEADS, NSA_GROUP, HEAD_DIM
    offs = np.cumsum([0, NSA_WIDTH] + [KV_DIM] * 6)
    q = zc[..., :NSA_WIDTH].reshape(B_, S, Hkv, G, dk)
    q = rmsnorm(q, q_norm_w) * (dk ** -0.5)
    kvs = [zc[..., offs[i + 1]:offs[i + 2]].reshape(B_, S, Hkv, dk) for i in range(6)]
    k_c_raw, v_c_raw, k_s, v_s, k_w, v_w = kvs
    gates = jax.nn.sigmoid(zc[..., offs[-1]:].astype(jnp.float32))
    gates = gates.reshape(B_, S, N_BRANCH, Hkv, G).astype(zc.dtype)
    t = jnp.arange(S)

    k_c = rmsnorm(compress(k_c_raw, pe_k, w1_k, w2_k), k_norm_w[0])
    v_c = compress(v_c_raw, pe_v, w1_v, w2_v)
    n_c = k_c.shape[1]
    cmp_end = jnp.arange(n_c) * CMP_STRIDE + CMP_LEN - 1
    cmask = cmp_end[None, :] <= t[:, None]
    s = jnp.einsum('bshgd,bnhd->bhgsn', q, k_c).astype(jnp.float32)
    p_cmp = jax.nn.softmax(jnp.where(cmask, s, NEG), axis=-1) * cmask
    o_cmp = jnp.einsum('bhgsn,bnhd->bshgd', p_cmp.astype(v_c.dtype), v_c)

    n_sel = S // SEL_LEN
    n_top = min(N_SELECT, n_sel)
    ci = np.arange(n_c)[:, None] * CMP_STRIDE
    sj = np.arange(n_sel)[None, :] * SEL_LEN
    overlap = ((ci < sj + SEL_LEN) & (ci + CMP_LEN > sj)).astype(np.float32)
    imp = jnp.einsum('bhgsn,nj->bhsj', p_cmp, jnp.asarray(overlap))
    blk_ids = jnp.arange(n_sel)
    valid = (blk_ids * SEL_LEN)[None, :] <= t[:, None]
    forced = (blk_ids[None, :] == (t // SEL_LEN)[:, None]) | (blk_ids[None, :] == 0)
    score = jnp.where(forced, SEL_FORCE, jnp.where(valid, imp, -1.0))
    _, sel_idx = lax.top_k(score, n_top)
    k_s = rmsnorm(k_s, k_norm_w[1])
    k_blk = k_s.reshape(B_, n_sel, SEL_LEN, Hkv, dk).transpose(0, 3, 1, 2, 4)
    v_blk = v_s.reshape(B_, n_sel, SEL_LEN, Hkv, dk).transpose(0, 3, 1, 2, 4)
    n_qc = S // SEL_Q_BLOCK
    q_ch = q.reshape(B_, n_qc, SEL_Q_BLOCK, Hkv, G, dk).transpose(1, 0, 3, 4, 2, 5)
    idx_ch = sel_idx.reshape(B_, Hkv, n_qc, SEL_Q_BLOCK, n_top).transpose(2, 0, 1, 3, 4)
    pos_ch = t.reshape(n_qc, SEL_Q_BLOCK)
    bi = jnp.arange(B_)[:, None, None, None]
    hi = jnp.arange(Hkv)[None, :, None, None]

    def sel_block(args):
        qc, ic, pc = args
        kg = k_blk[bi, hi, ic]
        vg = v_blk[bi, hi, ic]
        sc = jnp.einsum('bhgqd,bhqnld->bhgqnl', qc, kg).astype(jnp.float32)
        kpos = ic[..., None] * SEL_LEN + jnp.arange(SEL_LEN)
        m = (kpos <= pc[None, None, :, None, None])[:, :, None]
        sc = jnp.where(m, sc, NEG).reshape(B_, Hkv, G, SEL_Q_BLOCK, n_top * SEL_LEN)
        pr = jax.nn.softmax(sc, axis=-1).reshape(B_, Hkv, G, SEL_Q_BLOCK, n_top, SEL_LEN)
        return jnp.einsum('bhgqnl,bhqnld->bhgqd', pr.astype(vg.dtype), vg)

    o_sel = lax.map(sel_block, (q_ch, idx_ch, pos_ch))
    o_sel = o_sel.transpose(1, 0, 4, 2, 3, 5).reshape(B_, S, Hkv, G, dk)

    k_w = rmsnorm(k_w, k_norm_w[2])
    n_qb = S // WIN_Q_BLOCK
    n_off = WINDOW // WIN_Q_BLOCK + 1

    def band(a):
        ap = jnp.pad(a, ((0, 0), (WINDOW, 0), (0, 0), (0, 0)))
        ap = ap.reshape(B_, n_qb + n_off - 1, WIN_Q_BLOCK, Hkv, dk)
        return jnp.concatenate([ap[:, o:o + n_qb] for o in range(n_off)], axis=2)

    kb, vb = band(k_w), band(v_w)
    qb = q.reshape(B_, n_qb, WIN_Q_BLOCK, Hkv, G, dk)
    sw = jnp.einsum('bnqhgd,bnkhd->bnhgqk', qb, kb).astype(jnp.float32)
    qpos = jnp.arange(n_qb)[:, None] * WIN_Q_BLOCK + jnp.arange(WIN_Q_BLOCK)[None, :]
    kpos = jnp.arange(n_qb)[:, None] * WIN_Q_BLOCK - WINDOW + jnp.arange(n_off * WIN_Q_BLOCK)[None, :]
    d = qpos[:, :, None] - kpos[:, None, :]
    wm = (d >= 0) & (d < WINDOW) & (kpos[:, None, :] >= 0)
    pw = jax.nn.softmax(jnp.where(wm[None, :, None, None], sw, NEG), axis=-1)
    o_win = jnp.einsum('bnhgqk,bnkhd->bnqhgd', pw.astype(vb.dtype), vb).reshape(B_, S, Hkv, G, dk)

    o = (gates[:, :, 0, :, :, None] * o_cmp + gates[:, :, 1, :, :, None] * o_sel
         + gates[:, :, 2, :, :, None] * o_win)
    return o.reshape(B_, S, NSA_WIDTH)


def hier_moe(x, wg, bg, we, be, w_gate, w_up, w_down):
    B_, S, D = x.shape
    xt = x.reshape(B_ * S, D)
    lg = (xt @ wg).astype(jnp.float32) + bg
    grp = jnp.argmax(lg, axis=-1)
    p_grp = jnp.take_along_axis(jax.nn.softmax(lg, axis=-1), grp[:, None], axis=1)
    le = ((xt @ we).astype(jnp.float32) + be).reshape(-1, N_GROUPS_MOE, EXPERTS_PER_GROUP)
    le_g = jnp.take_along_axis(le, grp[:, None, None], axis=1)[:, 0]
    top_p, top_i = lax.top_k(jax.nn.softmax(le_g, axis=-1), TOP_K_EXPERTS)
    w = p_grp * top_p / jnp.sum(top_p, axis=-1, keepdims=True)
    eid = grp[:, None] * EXPERTS_PER_GROUP + top_i
    combine = jnp.sum(jax.nn.one_hot(eid, N_EXPERTS, dtype=jnp.float32) * w[..., None], axis=1)
    h = jax.nn.silu(jnp.einsum('td,edf->tef', xt, w_gate)) * jnp.einsum('td,edf->tef', xt, w_up)
    h = h * combine.astype(h.dtype)[..., None]
    y = jnp.einsum('tef,efd->td', h, w_down)
    return y.reshape(B_, S, D)


def setup_inputs(seed: int = 0) -> dict:
    key = jax.random.key(seed)
    ks = jax.random.split(key, 24)
    L = DEPTH

    def nrm(k, shape, fan_in):
        return jax.random.normal(k, shape, jnp.float32) * (fan_in ** -0.5)

    def gain(k, shape):
        return 1.0 + 0.05 * jax.random.normal(k, shape, jnp.float32)

    return {
        'x': jax.random.normal(ks[0], (BATCH, SEQ, D_MODEL), jnp.float32),
        'norm1_w': gain(ks[1], (L, D_MODEL)),
        'w_in': nrm(ks[2], (L, D_MODEL, D_IN), D_MODEL),
        'pool_w': nrm(ks[3], (L, POOL_GROUPS, POOL_GROUP_DIM, POOL_GROUP_DIM), POOL_GROUP_DIM),
        'pool_scale': gain(ks[4], (L, POOL_WIDTH)),
        'conv_w': nrm(ks[5], (L, CONV_K, CONV_WIDTH), CONV_K),
        'cmp_pe_k': 0.02 * jax.random.normal(ks[6], (L, CMP_LEN, HEAD_DIM), jnp.float32),
        'cmp_w1_k': nrm(ks[7], (L, CMP_LEN, HEAD_DIM, CMP_HIDDEN), CMP_LEN * HEAD_DIM),
        'cmp_w2_k': nrm(ks[8], (L, CMP_HIDDEN, HEAD_DIM), CMP_HIDDEN),
        'cmp_pe_v': 0.02 * jax.random.normal(ks[9], (L, CMP_LEN, HEAD_DIM), jnp.float32),
        'cmp_w1_v': nrm(ks[10], (L, CMP_LEN, HEAD_DIM, CMP_HIDDEN), CMP_LEN * HEAD_DIM),
        'cmp_w2_v': nrm(ks[11], (L, CMP_HIDDEN, HEAD_DIM), CMP_HIDDEN),
        'q_norm_w': gain(ks[12], (L, HEAD_DIM)),
        'k_norm_w': gain(ks[13], (L, N_BRANCH, HEAD_DIM)),
        'w_out': nrm(ks[14], (L, D_MIX, D_MODEL), D_MIX),
        'norm2_w': gain(ks[15], (L, D_MODEL)),
        'router_grp_w': nrm(ks[16], (L, D_MODEL, N_GROUPS_MOE), D_MODEL),
        'router_grp_b': 0.01 * jax.random.normal(ks[17], (L, N_GROUPS_MOE), jnp.float32),
        'router_exp_w': nrm(ks[18], (L, D_MODEL, N_EXPERTS), D_MODEL),
        'router_exp_b': 0.01 * jax.random.normal(ks[19], (L, N_EXPERTS), jnp.float32),
        'exp_w_gate': nrm(ks[20], (L, N_EXPERTS, D_MODEL, D_EXPERT), D_MODEL),
        'exp_w_up': nrm(ks[21], (L, N_EXPERTS, D_MODEL, D_EXPERT), D_MODEL),
        'exp_w_down': nrm(ks[22], (L, N_EXPERTS, D_EXPERT, D_MODEL), D_EXPERT),
    }


def reference(x, norm1_w, w_in, pool_w, pool_scale, conv_w, cmp_pe_k, cmp_w1_k, cmp_w2_k,
              cmp_pe_v, cmp_w1_v, cmp_w2_v, q_norm_w, k_norm_w, w_out, norm2_w,
              router_grp_w, router_grp_b, router_exp_w, router_exp_b,
              exp_w_gate, exp_w_up, exp_w_down):
    a_end = POOL_WIDTH
    b_end = a_end + 3 * CONV_WIDTH
    for l in range(DEPTH):
        xn = rmsnorm(x, norm1_w[l])
        z = xn @ w_in[l]
        y_a = pool_mixer(z[..., :a_end], pool_w[l], pool_scale[l])
        zb = z[..., a_end:b_end]
        y_b = conv_mixer(zb[..., :CONV_WIDTH], zb[..., CONV_WIDTH:2 * CONV_WIDTH],
                         zb[..., 2 * CONV_WIDTH:], conv_w[l])
        y_c = nsa_mixer(z[..., b_end:], q_norm_w[l], k_norm_w[l], cmp_pe_k[l], cmp_w1_k[l], cmp_w2_k[l],
                        cmp_pe_v[l], cmp_w1_v[l], cmp_w2_v[l])
        mixed = jnp.concatenate([y_a, y_b, y_c], axis=-1)
        x = x + mixed @ w_out[l]
        hn = rmsnorm(x, norm2_w[l])
        x = x + hier_moe(hn, router_grp_w[l], router_grp_b[l], router_exp_w[l], router_exp_b[l],
                         exp_w_gate[l], exp_w_up[l], exp_w_down[l])
    return x
```

```python
import functools

import numpy as np
import jax
import jax.numpy as jnp
from jax import lax
from jax.experimental import pallas as pl
from jax.experimental.pallas import tpu as pltpu

F32 = jnp.float32
BF16 = jnp.bfloat16

POOL_WINDOWS = (2, 4, 8, 16)
LANE = 128
POOL_WIDTH = 512
CONV_WIDTH = 512
CONV_K = 3
NSA_WIDTH = 1024
HEAD_DIM = 128
NSA_KV_HEADS = 2
NSA_GROUP = 4
N_BRANCH = 3
CMP_LEN = 32
CMP_STRIDE = 16
CMP_HIDDEN = 256
SEL_LEN = 64
N_SELECT = 16
SEL_FORCE = 1.0e4
WINDOW = 512
N_GROUPS_MOE = 4
EXPERTS_PER_GROUP = 8
N_EXPERTS = 32
D_EXPERT = 256
EPS = 1e-6
NEG = -1e30

COL_U = 0
COL_B = 512
COL_C = 1024
COL_V = 1536
COL_Q = 2048
COL_KC = 3072
COL_KS = 3584
COL_VS = 3840
COL_KW = 4096
COL_VW = 4352
COL_GATE = 4608
D_IN = 4632
D_IN_PAD = 5120

VMEM_LIMIT = 56 * 1024 * 1024


def _params(sem):
    return pltpu.CompilerParams(dimension_semantics=sem, vmem_limit_bytes=VMEM_LIMIT)


def _rms(x, w):
    return x * lax.rsqrt(jnp.mean(x * x, axis=-1, keepdims=True) + EPS) * w


def _silu(x):
    return x / (1.0 + jnp.exp(-x))


def _dot(a, b):
    return jnp.dot(a, b, preferred_element_type=F32)


def _dot_t(a, b):
    return lax.dot_general(a, b, (((1,), (1,)), ((), ())), preferred_element_type=F32)


def _split3_dot(a, b):
    hi = a.astype(BF16)
    r1 = a - hi.astype(F32)
    mid = r1.astype(BF16)
    lo = (r1 - mid.astype(F32)).astype(BF16)
    return _dot(hi, b) + _dot(mid, b) + _dot(lo, b)


def _inproj_kernel(x_ref, nw_ref, w_ref, o_ref, xn_ref):
    @pl.when(pl.program_id(1) == 0)
    def _():
        xn_ref[...] = _rms(x_ref[...], nw_ref[...]).astype(BF16)

    o_ref[...] = _dot(xn_ref[...], w_ref[...]).astype(o_ref.dtype)


def _inproj(x2d, nw, w_bf16, tm=512, tn=512):
    T, D = x2d.shape
    N = w_bf16.shape[1]
    return pl.pallas_call(
        _inproj_kernel,
        grid=(T // tm, N // tn),
        in_specs=[
            pl.BlockSpec((tm, D), lambda i, j: (i, 0)),
            pl.BlockSpec((1, D), lambda i, j: (0, 0)),
            pl.BlockSpec((D, tn), lambda i, j: (0, j)),
        ],
        out_specs=pl.BlockSpec((tm, tn), lambda i, j: (i, j)),
        out_shape=jax.ShapeDtypeStruct((T, N), BF16),
        scratch_shapes=[pltpu.VMEM((tm, D), BF16)],
        compiler_params=_params(("parallel", "arbitrary")),
        name="inproj",
    )(x2d, nw, w_bf16)


def _mix_ab_kernel(u_ref, b_ref, c_ref, v_ref, pw_ref, ps_ref, cw_ref, o_ref):
    S = u_ref.shape[0]
    row = lax.broadcasted_iota(jnp.int32, (S, LANE), 0)

    def shift(a, k):
        return jnp.where(row >= k, pltpu.roll(a, k, axis=0), 0.0)

    for g, w in enumerate(POOL_WINDOWS):
        sl = slice(g * LANE, (g + 1) * LANE)
        u = u_ref[:, sl].astype(F32)
        s = u
        k = 1
        while k < w:
            s = s + shift(s, k)
            k *= 2
        cnt = jnp.minimum(row + 1, w).astype(F32)
        mixed = (s / cnt - u).astype(BF16)
        y = _dot(mixed, pw_ref[g]) * ps_ref[:, sl]
        o_ref[:, sl] = y.astype(o_ref.dtype)

    for h in range(CONV_WIDTH // LANE):
        sl = slice(h * LANE, (h + 1) * LANE)
        u2 = c_ref[:, sl].astype(F32) * v_ref[:, sl].astype(F32)
        y = cw_ref[0:1, sl] * shift(u2, 2)
        y = y + cw_ref[1:2, sl] * shift(u2, 1)
        y = y + cw_ref[2:3, sl] * u2
        y = b_ref[:, sl].astype(F32) * y
        o_ref[:, POOL_WIDTH + h * LANE:POOL_WIDTH + (h + 1) * LANE] = y.astype(o_ref.dtype)


def _mix_ab(z, pool_w_bf16, pool_scale, conv_w, B, S):
    T = z.shape[0]
    blk = lambda c: pl.BlockSpec((S, 512), lambda b, c=c: (b, c))
    return pl.pallas_call(
        _mix_ab_kernel,
        grid=(B,),
        in_specs=[
            blk(COL_U // 512), blk(COL_B // 512), blk(COL_C // 512), blk(COL_V // 512),
            pl.BlockSpec((4, LANE, LANE), lambda b: (0, 0, 0)),
            pl.BlockSpec((1, POOL_WIDTH), lambda b: (0, 0)),
            pl.BlockSpec((CONV_K, CONV_WIDTH), lambda b: (0, 0)),
        ],
        out_specs=pl.BlockSpec((S, POOL_WIDTH + CONV_WIDTH), lambda b: (b, 0)),
        out_shape=jax.ShapeDtypeStruct((T, POOL_WIDTH + CONV_WIDTH), BF16),
        compiler_params=_params(("parallel",)),
        name="mix_ab",
    )(z, z, z, z, pool_w_bf16, pool_scale, conv_w)


def _compress_kernel(x_ref, w1k_ref, w1v_ref, pek_ref, pev_ref, w1kf_ref, w1vf_ref,
                     w2k_ref, w2v_ref, knw_ref, kc_ref, vc_ref):
    n_half = CMP_LEN // CMP_STRIDE
    streams = ((w1k_ref, pek_ref, w1kf_ref, w2k_ref, kc_ref),
               (w1v_ref, pev_ref, w1vf_ref, w2v_ref, vc_ref))
    for which, (w1_ref, pe_ref, w1f_ref, w2_ref, out_ref) in enumerate(streams):
        pe_term = _dot(pe_ref[...], w1f_ref[...])[0:1, :]
        for h in range(NSA_KV_HEADS):
            acc = None
            for l in range(CMP_STRIDE):
                c0 = ((l * 2 + which) * NSA_KV_HEADS + h) * HEAD_DIM
                part = _dot(x_ref[0, :, c0:c0 + HEAD_DIM], w1_ref[l])
                acc = part if acc is None else acc + part
            n16 = acc.shape[0]
            first = acc[:, :CMP_HIDDEN]
            second = acc[:, CMP_HIDDEN:]
            assert n_half == 2
            hid = first + pltpu.roll(second, n16 - 1, axis=0) + pe_term
            out = _dot(_silu(hid).astype(BF16), w2_ref[...])
            if which == 0:
                out = _rms(out, knw_ref[0:1, :])
            out_ref[0, h] = out.astype(out_ref.dtype)


def _compress(xkv, w1k, w1v, pek8, pev8, w1kf, w1vf, w2k, w2v, knw):
    B, n16, W = xkv.shape
    full = lambda a: pl.BlockSpec(a.shape, lambda b, nd=a.ndim: (0,) * nd)
    out_sds = jax.ShapeDtypeStruct((B, NSA_KV_HEADS, n16, HEAD_DIM), BF16)
    out_spec = pl.BlockSpec((1, NSA_KV_HEADS, n16, HEAD_DIM), lambda b: (b, 0, 0, 0))
    return pl.pallas_call(
        _compress_kernel,
        grid=(B,),
        in_specs=[pl.BlockSpec((1, n16, W), lambda b: (b, 0, 0)),
                  full(w1k), full(w1v), full(pek8), full(pev8), full(w1kf), full(w1vf),
                  full(w2k), full(w2v), full(knw)],
        out_specs=[out_spec, out_spec],
        out_shape=[out_sds, out_sds],
        compiler_params=_params(("parallel",)),
        name="compress",
    )(xkv, w1k, w1v, pek8, pev8, w1kf, w1vf, w2k, w2v, knw)


def _nsa_kernel(zq_ref, zg_ref, ks_ref, vs_ref, kw_ref, vw_ref, kc_ref, vc_ref,
                qnw_ref, knw_ref, ovl_ref, exp_ref, o_ref, ksn_ref, kwn_ref, *, tq):
    G = NSA_GROUP
    hkv = pl.program_id(1)
    i = pl.program_id(2)
    tk = tq

    @pl.when(i == 0)
    def _():
        ksn_ref[...] = _rms(ks_ref[...].astype(F32), knw_ref[1:2, :]).astype(BF16)
        kwn_ref[...] = _rms(kw_ref[...].astype(F32), knw_ref[2:3, :]).astype(BF16)

    qs = []
    for g in range(G):
        qg = zq_ref[:, g * HEAD_DIM:(g + 1) * HEAD_DIM].astype(F32)
        qg = _rms(qg, qnw_ref[...]) * (HEAD_DIM ** -0.5)
        qs.append(qg.astype(BF16))
    q4 = jnp.concatenate(qs, axis=0)

    t0 = i * tq
    tpos_c = t0 + (lax.broadcasted_iota(jnp.int32, (G * tq, LANE), 0) & (tq - 1))

    s = _dot_t(q4, kc_ref[0, 0])
    n_idx = lax.broadcasted_iota(jnp.int32, (G * tq, LANE), 1)
    cm = (n_idx * CMP_STRIDE + (CMP_LEN - 1)) <= tpos_c
    sm = jnp.where(cm, s, NEG)
    e = jnp.exp(sm - jnp.max(sm, axis=1, keepdims=True))
    p = e / jnp.sum(e, axis=1, keepdims=True)
    p = jnp.where(cm, p, 0.0)
    o_cmp = _dot(p.astype(BF16), vc_ref[0, 0])

    psum = p[0:tq]
    for g in range(1, G):
        psum = psum + p[g * tq:(g + 1) * tq]
    imp = _split3_dot(psum, ovl_ref[...])
    n_sel = exp_ref.shape[0] // SEL_LEN
    n_top = min(N_SELECT, n_sel)
    j_idx = lax.broadcasted_iota(jnp.int32, (tq, LANE), 1)
    t_row = t0 + lax.broadcasted_iota(jnp.int32, (tq, LANE), 0)
    forced = (j_idx == (t_row // SEL_LEN)) | (j_idx == 0)
    valid = (j_idx * SEL_LEN) <= t_row
    score = jnp.where(forced, SEL_FORCE, jnp.where(valid, imp, -1.0))
    score = jnp.where(j_idx < n_sel, score, -2.0)
    rank = jnp.zeros((tq, LANE), F32)
    for c in range(n_sel):
        col = jnp.broadcast_to(score[:, c:c + 1], (tq, LANE))
        beats = jnp.where(col > score, 1.0, jnp.where((col == score) & (j_idx > c), 1.0, 0.0))
        rank = rank + beats
    selm = jnp.where((rank < n_top) & (j_idx < n_sel), 1.0, 0.0).astype(BF16)

    def sel_body(kt, carry):
        m, l, acc = carry
        k0 = pl.multiple_of(kt * tk, tk)
        kblk = ksn_ref[pl.ds(k0, tk), :]
        vblk = vs_ref[pl.ds(k0, tk), :]
        sc = _dot_t(q4, kblk)
        em = _dot_t(selm, exp_ref[pl.ds(k0, tk), :])
        em4 = jnp.concatenate([em] * G, axis=0)
        kpos = k0 + lax.broadcasted_iota(jnp.int32, (G * tq, tk), 1)
        tpos = t0 + (lax.broadcasted_iota(jnp.int32, (G * tq, tk), 0) & (tq - 1))
        sc = jnp.where((em4 > 0.5) & (kpos <= tpos), sc, NEG)
        m_new = jnp.maximum(m, jnp.max(sc, axis=1, keepdims=True))
        alpha = jnp.exp(m - m_new)
        pr = jnp.exp(sc - m_new)
        l = alpha * l + jnp.sum(pr, axis=1, keepdims=True)
        acc = alpha * acc + _dot(pr.astype(BF16), vblk)
        return m_new, l, acc

    init = (jnp.full((G * tq, 1), NEG, F32), jnp.zeros((G * tq, 1), F32),
            jnp.zeros((G * tq, HEAD_DIM), F32))
    _, l_s, acc_s = lax.fori_loop(0, i + 1, sel_body, init)
    o_sel = acc_s / l_s

    m, l, acc = init
    for d in range(WINDOW // tk + 1):
        kt = i - d
        k0 = pl.multiple_of(jnp.maximum(kt, 0) * tk, tk)
        kblk = kwn_ref[pl.ds(k0, tk), :]
        vblk = vw_ref[pl.ds(k0, tk), :]
        sc = _dot_t(q4, kblk)
        kpos = kt * tk + lax.broadcasted_iota(jnp.int32, (G * tq, tk), 1)
        tpos = t0 + (lax.broadcasted_iota(jnp.int32, (G * tq, tk), 0) & (tq - 1))
        dist = tpos - kpos
        sc = jnp.where((dist >= 0) & (dist < WINDOW) & (kpos >= 0), sc, NEG)
        m_new = jnp.maximum(m, jnp.max(sc, axis=1, keepdims=True))
        alpha = jnp.exp(m - m_new)
        pr = jnp.exp(sc - m_new)
        l = alpha * l + jnp.sum(pr, axis=1, keepdims=True)
        acc = alpha * acc + _dot(pr.astype(BF16), vblk)
        m = m_new
    o_win = acc / l

    gates = 1.0 / (1.0 + jnp.exp(-zg_ref[...].astype(F32)))
    lane = lax.broadcasted_iota(jnp.int32, (tq, LANE), 1)

    def gate_col(c):
        return jnp.sum(jnp.where(lane == c, gates, 0.0), axis=1, keepdims=True)

    n_heads = NSA_KV_HEADS * G
    for g in range(G):
        c = hkv * G + g
        rows = slice(g * tq, (g + 1) * tq)
        out = (gate_col(c) * o_cmp[rows] + gate_col(n_heads + c) * o_sel[rows]
               + gate_col(2 * n_heads + c) * o_win[rows])
        o_ref[:, g * HEAD_DIM:(g + 1) * HEAD_DIM] = out.astype(o_ref.dtype)


def _nsa(z, kc, vc, qnw, knw, B, S, tq=256):
    T = z.shape[0]
    nq = S // tq
    n_c = (S - CMP_LEN) // CMP_STRIDE + 1
    n_sel = S // SEL_LEN
    assert S % tq == 0 and tq & (tq - 1) == 0 and n_sel <= LANE and n_c <= LANE and WINDOW % tq == 0
    ci = np.arange(LANE)[:, None] * CMP_STRIDE
    sj = np.arange(LANE)[None, :] * SEL_LEN
    ovl = ((ci < sj + SEL_LEN) & (ci + CMP_LEN > sj) & (np.arange(LANE)[:, None] < n_c)
           & (np.arange(LANE)[None, :] < n_sel))
    ovl = jnp.asarray(ovl.astype(np.float32), BF16)
    expand = (np.arange(S)[:, None] // SEL_LEN) == np.arange(LANE)[None, :]
    expand = jnp.asarray(expand.astype(np.float32), BF16)

    kvblk = lambda c: pl.BlockSpec((S, HEAD_DIM), lambda b, h, i, c=c: (b, c + h))
    cblk = pl.BlockSpec((1, 1, LANE, HEAD_DIM), lambda b, h, i: (b, h, 0, 0))
    full = lambda a: pl.BlockSpec(a.shape, lambda b, h, i, nd=a.ndim: (0,) * nd)
    return pl.pallas_call(
        functools.partial(_nsa_kernel, tq=tq),
        grid=(B, NSA_KV_HEADS, nq),
        in_specs=[
            pl.BlockSpec((tq, NSA_GROUP * HEAD_DIM), lambda b, h, i: (b * nq + i, COL_Q // 512 + h)),
            pl.BlockSpec((tq, LANE), lambda b, h, i: (b * nq + i, COL_GATE // LANE)),
            kvblk(COL_KS // LANE), kvblk(COL_VS // LANE), kvblk(COL_KW // LANE), kvblk(COL_VW // LANE),
            cblk, cblk, full(qnw), full(knw), full(ovl), full(expand),
        ],
        out_specs=pl.BlockSpec((tq, NSA_GROUP * HEAD_DIM), lambda b, h, i: (b * nq + i, h)),
        out_shape=jax.ShapeDtypeStruct((T, NSA_WIDTH), BF16),
        scratch_shapes=[pltpu.VMEM((S, HEAD_DIM), BF16), pltpu.VMEM((S, HEAD_DIM), BF16)],
        compiler_params=_params(("parallel", "parallel", "arbitrary")),
        name="nsa",
    )(z, z, z, z, z, z, kc, vc, qnw, knw, ovl, expand)


def _outproj_router_kernel(yab_ref, yc_ref, wo_ref, x_ref, nw_ref, wrh_ref, wrl_ref, br_ref,
                           x1_ref, hn_ref, comb_ref):
    ka = yab_ref.shape[1]
    acc = _dot(yab_ref[...], wo_ref[0:ka, :]) + _dot(yc_ref[...], wo_ref[ka:, :])
    x1 = x_ref[...] + acc
    x1_ref[...] = x1
    hn = _rms(x1, nw_ref[...])
    hi = hn.astype(BF16)
    hn_ref[...] = hi
    lo = (hn - hi.astype(F32)).astype(BF16)
    lg = _dot(hi, wrh_ref[...]) + _dot(lo, wrh_ref[...]) + _dot(hi, wrl_ref[...]) + br_ref[...]

    tm = lg.shape[0]
    lane = lax.broadcasted_iota(jnp.int32, (tm, LANE), 1)
    lane_f = lane.astype(F32)
    big = float(LANE)

    def first_max(v):
        m = jnp.max(v, axis=1, keepdims=True)
        idx = jnp.min(jnp.where(v == m, lane_f, big), axis=1, keepdims=True)
        return m, idx

    is_grp = (lane >= N_EXPERTS) & (lane < N_EXPERTS + N_GROUPS_MOE)
    lgm = jnp.where(is_grp, lg, NEG)
    mg, grp_lane = first_max(lgm)
    p_grp = 1.0 / jnp.sum(jnp.where(is_grp, jnp.exp(lgm - mg), 0.0), axis=1, keepdims=True)
    grp = grp_lane - float(N_EXPERTS)
    in_grp = (lane < N_EXPERTS) & ((lane // EXPERTS_PER_GROUP).astype(F32) == grp)
    le = jnp.where(in_grp, lg, NEG)
    m1, i1 = first_max(le)
    le2 = jnp.where(lane_f == i1, NEG, le)
    m2, i2 = first_max(le2)
    e2 = jnp.exp(m2 - m1)
    den = 1.0 + e2
    w1 = p_grp * (1.0 / den)
    w2 = p_grp * (e2 / den)
    comb_ref[...] = jnp.where(lane_f == i1, w1, 0.0) + jnp.where(lane_f == i2, w2, 0.0)


def _outproj_router(yab, yc, wo, x2d, nw, wr_hi, wr_lo, br, tm=512):
    T, D = x2d.shape
    full = lambda a: pl.BlockSpec(a.shape, lambda i, nd=a.ndim: (0,) * nd)
    return pl.pallas_call(
        _outproj_router_kernel,
        grid=(T // tm,),
        in_specs=[
            pl.BlockSpec((tm, yab.shape[1]), lambda i: (i, 0)),
            pl.BlockSpec((tm, yc.shape[1]), lambda i: (i, 0)),
            full(wo),
            pl.BlockSpec((tm, D), lambda i: (i, 0)),
            full(nw), full(wr_hi), full(wr_lo), full(br),
        ],
        out_specs=[pl.BlockSpec((tm, D), lambda i: (i, 0)),
                   pl.BlockSpec((tm, D), lambda i: (i, 0)),
                   pl.BlockSpec((tm, LANE), lambda i: (i, 0))],
        out_shape=[jax.ShapeDtypeStruct((T, D), F32),
                   jax.ShapeDtypeStruct((T, D), BF16),
                   jax.ShapeDtypeStruct((T, LANE), F32)],
        compiler_params=_params(("parallel",)),
        name="outproj_router",
    )(yab, yc, wo, x2d, nw, wr_hi, wr_lo, br)


def _moe_kernel(hn_ref, comb_ref, x1_ref, wg_ref, wu_ref, wd_ref, o_ref):
    e = pl.program_id(1)

    @pl.when(e == 0)
    def _():
        o_ref[...] = x1_ref[...]

    x = hn_ref[...]
    hg = _dot(x, wg_ref[0])
    hu = _dot(x, wu_ref[0])
    lane = lax.broadcasted_iota(jnp.int32, comb_ref.shape, 1)
    c = jnp.sum(jnp.where(lane == e, comb_ref[...], 0.0), axis=1, keepdims=True)
    h = _silu(hg) * hu * c
    o_ref[...] += _dot(h.astype(BF16), wd_ref[0])


def _moe(hn, comb, x1, wg, wu, wd, tm=512):
    T, D = hn.shape
    E, _, F = wg.shape
    return pl.pallas_call(
        _moe_kernel,
        grid=(T // tm, E),
        in_specs=[
            pl.BlockSpec((tm, D), lambda i, e: (i, 0)),
            pl.BlockSpec((tm, LANE), lambda i, e: (i, 0)),
            pl.BlockSpec((tm, D), lambda i, e: (i, 0)),
            pl.BlockSpec((1, D, F), lambda i, e: (e, 0, 0)),
            pl.BlockSpec((1, D, F), lambda i, e: (e, 0, 0)),
            pl.BlockSpec((1, F, D), lambda i, e: (e, 0, 0)),
        ],
        out_specs=pl.BlockSpec((tm, D), lambda i, e: (i, 0)),
        out_shape=jax.ShapeDtypeStruct((T, D), F32),
        compiler_params=_params(("parallel", "arbitrary")),
        name="moe",
    )(hn, comb, x1, wg, wu, wd)


def kernel(x, norm1_w, w_in, pool_w, pool_scale, conv_w, cmp_pe_k, cmp_w1_k, cmp_w2_k, cmp_pe_v, cmp_w1_v, cmp_w2_v, q_norm_w, k_norm_w, w_out, norm2_w, router_grp_w, router_grp_b, router_exp_w, router_exp_b, exp_w_gate, exp_w_up, exp_w_down):
    B, S, D = x.shape
    depth = w_in.shape[0]
    T = B * S
    n16 = S // CMP_STRIDE
    xf = x.reshape(T, D)

    def w1_pair(w1):
        return jnp.concatenate([w1[:CMP_STRIDE], w1[CMP_STRIDE:]], axis=-1).astype(BF16)

    def pe_rows(pe):
        return jnp.broadcast_to(pe.reshape(1, CMP_LEN * HEAD_DIM), (8, CMP_LEN * HEAD_DIM)).astype(BF16)

    for l in range(depth):
        w_in_p = jnp.pad(w_in[l], ((0, 0), (0, D_IN_PAD - D_IN))).astype(BF16)
        z = _inproj(xf, norm1_w[l].reshape(1, D), w_in_p)

        yab = _mix_ab(z, pool_w[l].astype(BF16), pool_scale[l].reshape(1, POOL_WIDTH), conv_w[l], B, S)

        xkv = z[:, COL_KC:COL_KS].reshape(B, n16, CMP_STRIDE * 512)
        kc, vc = _compress(
            xkv, w1_pair(cmp_w1_k[l]), w1_pair(cmp_w1_v[l]), pe_rows(cmp_pe_k[l]), pe_rows(cmp_pe_v[l]),
            cmp_w1_k[l].reshape(CMP_LEN * HEAD_DIM, CMP_HIDDEN).astype(BF16),
            cmp_w1_v[l].reshape(CMP_LEN * HEAD_DIM, CMP_HIDDEN).astype(BF16),
            cmp_w2_k[l].astype(BF16), cmp_w2_v[l].astype(BF16), k_norm_w[l])
        yc = _nsa(z, kc, vc, q_norm_w[l].reshape(1, HEAD_DIM), k_norm_w[l], B, S)

        wr = jnp.concatenate([router_exp_w[l], router_grp_w[l]], axis=1)
        wr = jnp.pad(wr, ((0, 0), (0, LANE - wr.shape[1])))
        wr_hi = wr.astype(BF16)
        wr_lo = (wr - wr_hi.astype(F32)).astype(BF16)
        br = jnp.concatenate([router_exp_b[l], router_grp_b[l]])
        br = jnp.pad(br, (0, LANE - br.shape[0])).reshape(1, LANE)
        x1, hn, comb = _outproj_router(yab, yc, w_out[l].astype(BF16), xf, norm2_w[l].reshape(1, D),
                                       wr_hi, wr_lo, br)

        xf = _moe(hn, comb, x1, exp_w_gate[l].astype(BF16), exp_w_up[l].astype(BF16),
                  exp_w_down[l].astype(BF16))
    return xf.reshape(B, S, D)
```

```python
import functools

import numpy as np
import jax
import jax.numpy as jnp
from jax import lax
from jax.experimental import pallas as pl
from jax.experimental.pallas import tpu as pltpu

F32 = jnp.float32
BF16 = jnp.bfloat16

POOL_WINDOWS = (2, 4, 8, 16)
LANE = 128
POOL_WIDTH = 512
CONV_WIDTH = 512
CONV_K = 3
NSA_WIDTH = 1024
HEAD_DIM = 128
NSA_KV_HEADS = 2
NSA_GROUP = 4
N_BRANCH = 3
CMP_LEN = 32
CMP_STRIDE = 16
CMP_HIDDEN = 256
SEL_LEN = 64
N_SELECT = 16
SEL_FORCE = 1.0e4
WINDOW = 512
N_GROUPS_MOE = 4
EXPERTS_PER_GROUP = 8
N_EXPERTS = 32
D_EXPERT = 256
EPS = 1e-6
NEG = -1e30

COL_U = 0
COL_B = 512
COL_C = 1024
COL_V = 1536
COL_Q = 2048
COL_KC = 3072
COL_KS = 3584
COL_VS = 3840
COL_KW = 4096
COL_VW = 4352
COL_GATE = 4608
D_IN = 4632
D_IN_PAD = 5120

VMEM_LIMIT = 56 * 1024 * 1024


def _params(sem):
    return pltpu.CompilerParams(dimension_semantics=sem, vmem_limit_bytes=VMEM_LIMIT)


def _rms(x, w):
    return x * lax.rsqrt(jnp.mean(x * x, axis=-1, keepdims=True) + EPS) * w


def _silu(x):
    return x / (1.0 + jnp.exp(-x))


def _dot(a, b):
    return jnp.dot(a, b, preferred_element_type=F32)


def _dot_t(a, b):
    return lax.dot_general(a, b, (((1,), (1,)), ((), ())), preferred_element_type=F32)


def _split3_dot(a, b):
    hi = a.astype(BF16)
    r1 = a - hi.astype(F32)
    mid = r1.astype(BF16)
    lo = (r1 - mid.astype(F32)).astype(BF16)
    return _dot(hi, b) + _dot(mid, b) + _dot(lo, b)


def _inproj_kernel(x_ref, nw_ref, w_ref, o_ref, xn_ref):
    @pl.when(pl.program_id(1) == 0)
    def _():
        xn_ref[...] = _rms(x_ref[...], nw_ref[...]).astype(BF16)

    o_ref[...] = _dot(xn_ref[...], w_ref[...]).astype(o_ref.dtype)


def _inproj(x2d, nw, w_bf16, tm=512, tn=512):
    T, D = x2d.shape
    N = w_bf16.shape[1]
    return pl.pallas_call(
        _inproj_kernel,
        grid=(T // tm, N // tn),
        in_specs=[
            pl.BlockSpec((tm, D), lambda i, j: (i, 0)),
            pl.BlockSpec((1, D), lambda i, j: (0, 0)),
            pl.BlockSpec((D, tn), lambda i, j: (0, j)),
        ],
        out_specs=pl.BlockSpec((tm, tn), lambda i, j: (i, j)),
        out_shape=jax.ShapeDtypeStruct((T, N), BF16),
        scratch_shapes=[pltpu.VMEM((tm, D), BF16)],
        compiler_params=_params(("parallel", "arbitrary")),
        name="inproj",
    )(x2d, nw, w_bf16)


def _mix_ab_kernel(u_ref, b_ref, c_ref, v_ref, pw_ref, ps_ref, cw_ref, o_ref):
    S = u_ref.shape[0]
    row = lax.broadcasted_iota(jnp.int32, (S, LANE), 0)

    def shift(a, k):
        return jnp.where(row >= k, pltpu.roll(a, k, axis=0), 0.0)

    for g, w in enumerate(POOL_WINDOWS):
        sl = slice(g * LANE, (g + 1) * LANE)
        u = u_ref[:, sl].astype(F32)
        s = u
        k = 1
        while k < w:
            s = s + shift(s, k)
            k *= 2
        cnt = jnp.minimum(row + 1, w).astype(F32)
        mixed = (s / cnt - u).astype(BF16)
        y = _dot(mixed, pw_ref[g]) * ps_ref[:, sl]
        o_ref[:, sl] = y.astype(o_ref.dtype)

    for h in range(CONV_WIDTH // LANE):
        sl = slice(h * LANE, (h + 1) * LANE)
        u2 = c_ref[:, sl].astype(F32) * v_ref[:, sl].astype(F32)
        y = cw_ref[0:1, sl] * shift(u2, 2)
        y = y + cw_ref[1:2, sl] * shift(u2, 1)
        y = y + cw_ref[2:3, sl] * u2
        y = b_ref[:, sl].astype(F32) * y
        o_ref[:, POOL_WIDTH + h * LANE:POOL_WIDTH + (h + 1) * LANE] = y.astype(o_ref.dtype)


def _mix_ab(z, pool_w_bf16, pool_scale, conv_w, B, S):
    T = z.shape[0]
    blk = lambda c: pl.BlockSpec((S, 512), lambda b, c=c: (b, c))
    return pl.pallas_call(
        _mix_ab_kernel,
        grid=(B,),
        in_specs=[
            blk(COL_U // 512), blk(COL_B // 512), blk(COL_C // 512), blk(COL_V // 512),
            pl.BlockSpec((4, LANE, LANE), lambda b: (0, 0, 0)),
            pl.BlockSpec((1, POOL_WIDTH), lambda b: (0, 0)),
            pl.BlockSpec((CONV_K, CONV_WIDTH), lambda b: (0, 0)),
        ],
        out_specs=pl.BlockSpec((S, POOL_WIDTH + CONV_WIDTH), lambda b: (b, 0)),
        out_shape=jax.ShapeDtypeStruct((T, POOL_WIDTH + CONV_WIDTH), BF16),
        compiler_params=_params(("parallel",)),
        name="mix_ab",
    )(z, z, z, z, pool_w_bf16, pool_scale, conv_w)


def _compress_kernel(x_ref, w1k_ref, w1v_ref, pek_ref, pev_ref, w1kf_ref, w1vf_ref,
                     w2k_ref, w2v_ref, knw_ref, kc_ref, vc_ref):
    n_half = CMP_LEN // CMP_STRIDE
    streams = ((w1k_ref, pek_ref, w1kf_ref, w2k_ref, kc_ref),
               (w1v_ref, pev_ref, w1vf_ref, w2v_ref, vc_ref))
    for which, (w1_ref, pe_ref, w1f_ref, w2_ref, out_ref) in enumerate(streams):
        pe_term = _dot(pe_ref[...], w1f_ref[...])[0:1, :]
        for h in range(NSA_KV_HEADS):
            acc = None
            for l in range(CMP_STRIDE):
                c0 = ((l * 2 + which) * NSA_KV_HEADS + h) * HEAD_DIM
                part = _dot(x_ref[0, :, c0:c0 + HEAD_DIM], w1_ref[l])
                acc = part if acc is None else acc + part
            n16 = acc.shape[0]
            first = acc[:, :CMP_HIDDEN]
            second = acc[:, CMP_HIDDEN:]
            assert n_half == 2
            hid = first + pltpu.roll(second, n16 - 1, axis=0) + pe_term
            out = _dot(_silu(hid).astype(BF16), w2_ref[...])
            if which == 0:
                out = _rms(out, knw_ref[0:1, :])
            out_ref[0, h] = out.astype(out_ref.dtype)


def _compress(xkv, w1k, w1v, pek8, pev8, w1kf, w1vf, w2k, w2v, knw):
    B, n16, W = xkv.shape
    full = lambda a: pl.BlockSpec(a.shape, lambda b, nd=a.ndim: (0,) * nd)
    out_sds = jax.ShapeDtypeStruct((B, NSA_KV_HEADS, n16, HEAD_DIM), BF16)
    out_spec = pl.BlockSpec((1, NSA_KV_HEADS, n16, HEAD_DIM), lambda b: (b, 0, 0, 0))
    return pl.pallas_call(
        _compress_kernel,
        grid=(B,),
        in_specs=[pl.BlockSpec((1, n16, W), lambda b: (b, 0, 0)),
                  full(w1k), full(w1v), full(pek8), full(pev8), full(w1kf), full(w1vf),
                  full(w2k), full(w2v), full(knw)],
        out_specs=[out_spec, out_spec],
        out_shape=[out_sds, out_sds],
        compiler_params=_params(("parallel",)),
        name="compress",
    )(xkv, w1k, w1v, pek8, pev8, w1kf, w1vf, w2k, w2v, knw)


def _nsa_kernel(zq_ref, zg_ref, ks_ref, vs_ref, kw_ref, vw_ref, kc_ref, vc_ref,
                qnw_ref, knw_ref, ovl_ref, exp_ref, o_ref, ksn_ref, kwn_ref, *, tq):
    G = NSA_GROUP
    hkv = pl.program_id(1)
    i = pl.program_id(2)
    tk = tq

    @pl.when(i == 0)
    def _():
        ksn_ref[...] = _rms(ks_ref[...].astype(F32), knw_ref[1:2, :]).astype(BF16)
        kwn_ref[...] = _rms(kw_ref[...].astype(F32), knw_ref[2:3, :]).astype(BF16)

    qs = []
    for g in range(G):
        qg = zq_ref[:, g * HEAD_DIM:(g + 1) * HEAD_DIM].astype(F32)
        qg = _rms(qg, qnw_ref[...]) * (HEAD_DIM ** -0.5)
        qs.append(qg.astype(BF16))
    q4 = jnp.concatenate(qs, axis=0)

    t0 = i * tq
    tpos_c = t0 + (lax.broadcasted_iota(jnp.int32, (G * tq, LANE), 0) & (tq - 1))

    s = _dot_t(q4, kc_ref[0, 0])
    n_idx = lax.broadcasted_iota(jnp.int32, (G * tq, LANE), 1)
    cm = (n_idx * CMP_STRIDE + (CMP_LEN - 1)) <= tpos_c
    sm = jnp.where(cm, s, NEG)
    e = jnp.exp(sm - jnp.max(sm, axis=1, keepdims=True))
    p = e / jnp.sum(e, axis=1, keepdims=True)
    p = jnp.where(cm, p, 0.0)
    o_cmp = _dot(p.astype(BF16), vc_ref[0, 0])

    psum = p[0:tq]
    for g in range(1, G):
        psum = psum + p[g * tq:(g + 1) * tq]
    imp = _split3_dot(psum, ovl_ref[...])
    n_sel = exp_ref.shape[0] // SEL_LEN
    n_top = min(N_SELECT, n_sel)
    j_idx = lax.broadcasted_iota(jnp.int32, (tq, LANE), 1)
    t_row = t0 + lax.broadcasted_iota(jnp.int32, (tq, LANE), 0)
    forced = (j_idx == (t_row // SEL_LEN)) | (j_idx == 0)
    valid = (j_idx * SEL_LEN) <= t_row
    score = jnp.where(forced, SEL_FORCE, jnp.where(valid, imp, -1.0))
    score = jnp.where(j_idx < n_sel, score, -2.0)
    rank = jnp.zeros((tq, LANE), F32)
    for c in range(n_sel):
        col = jnp.broadcast_to(score[:, c:c + 1], (tq, LANE))
        beats = jnp.where(col > score, 1.0, jnp.where((col == score) & (j_idx > c), 1.0, 0.0))
        rank = rank + beats
    selm = jnp.where((rank < n_top) & (j_idx < n_sel), 1.0, 0.0).astype(BF16)

    def sel_body(kt, carry):
        m, l, acc = carry
        k0 = pl.multiple_of(kt * tk, tk)
        kblk = ksn_ref[pl.ds(k0, tk), :]
        vblk = vs_ref[pl.ds(k0, tk), :]
        sc = _dot_t(q4, kblk)
        em = _dot_t(selm, exp_ref[pl.ds(k0, tk), :])
        em4 = jnp.concatenate([em] * G, axis=0)
        kpos = k0 + lax.broadcasted_iota(jnp.int32, (G * tq, tk), 1)
        tpos = t0 + (lax.broadcasted_iota(jnp.int32, (G * tq, tk), 0) & (tq - 1))
        sc = jnp.where((em4 > 0.5) & (kpos <= tpos), sc, NEG)
        m_new = jnp.maximum(m, jnp.max(sc, axis=1, keepdims=True))
        alpha = jnp.exp(m - m_new)
        pr = jnp.exp(sc - m_new)
        l = alpha * l + jnp.sum(pr, axis=1, keepdims=True)
        acc = alpha * acc + _dot(pr.astype(BF16), vblk)
        return m_new, l, acc

    init = (jnp.full((G * tq, 1), NEG, F32), jnp.zeros((G * tq, 1), F32),
            jnp.zeros((G * tq, HEAD_DIM), F32))
    _, l_s, acc_s = lax.fori_loop(0, i + 1, sel_body, init)
    o_sel = acc_s / l_s

    m, l, acc = init
    for d in range(WINDOW // tk + 1):
        kt = i - d
        k0 = pl.multiple_of(jnp.maximum(kt, 0) * tk, tk)
        kblk = kwn_ref[pl.ds(k0, tk), :]
        vblk = vw_ref[pl.ds(k0, tk), :]
        sc = _dot_t(q4, kblk)
        kpos = kt * tk + lax.broadcasted_iota(jnp.int32, (G * tq, tk), 1)
        tpos = t0 + (lax.broadcasted_iota(jnp.int32, (G * tq, tk), 0) & (tq - 1))
        dist = tpos - kpos
        sc = jnp.where((dist >= 0) & (dist < WINDOW) & (kpos >= 0), sc, NEG)
        m_new = jnp.maximum(m, jnp.max(sc, axis=1, keepdims=True))
        alpha = jnp.exp(m - m_new)
        pr = jnp.exp(sc - m_new)
        l = alpha * l + jnp.sum(pr, axis=1, keepdims=True)
        acc = alpha * acc + _dot(pr.astype(BF16), vblk)
        m = m_new
    o_win = acc / l

    gates = 1.0 / (1.0 + jnp.exp(-zg_ref[...].astype(F32)))
    lane = lax.broadcasted_iota(jnp.int32, (tq, LANE), 1)

    def gate_col(c):
        return jnp.sum(jnp.where(lane == c, gates, 0.0), axis=1, keepdims=True)

    n_heads = NSA_KV_HEADS * G
    for g in range(G):
        c = hkv * G + g
        rows = slice(g * tq, (g + 1) * tq)
        out = (gate_col(c) * o_cmp[rows] + gate_col(n_heads + c) * o_sel[rows]
               + gate_col(2 * n_heads + c) * o_win[rows])
        o_ref[:, g * HEAD_DIM:(g + 1) * HEAD_DIM] = out.astype(o_ref.dtype)


def _nsa(z, kc, vc, qnw, knw, B, S, tq=256):
    T = z.shape[0]
    nq = S // tq
    n_c = (S - CMP_LEN) // CMP_STRIDE + 1
    n_sel = S // SEL_LEN
    assert S % tq == 0 and tq & (tq - 1) == 0 and n_sel <= LANE and n_c <= LANE and WINDOW % tq == 0
    ci = np.arange(LANE)[:, None] * CMP_STRIDE
    sj = np.arange(LANE)[None, :] * SEL_LEN
    ovl = ((ci < sj + SEL_LEN) & (ci + CMP_LEN > sj) & (np.arange(LANE)[:, None] < n_c)
           & (np.arange(LANE)[None, :] < n_sel))
    ovl = jnp.asarray(ovl.astype(np.float32), BF16)
    expand = (np.arange(S)[:, None] // SEL_LEN) == np.arange(LANE)[None, :]
    expand = jnp.asarray(expand.astype(np.float32), BF16)

    kvblk = lambda c: pl.BlockSpec((S, HEAD_DIM), lambda b, h, i, c=c: (b, c + h))
    cblk = pl.BlockSpec((1, 1, LANE, HEAD_DIM), lambda b, h, i: (b, h, 0, 0))
    full = lambda a: pl.BlockSpec(a.shape, lambda b, h, i, nd=a.ndim: (0,) * nd)
    return pl.pallas_call(
        functools.partial(_nsa_kernel, tq=tq),
        grid=(B, NSA_KV_HEADS, nq),
        in_specs=[
            pl.BlockSpec((tq, NSA_GROUP * HEAD_DIM), lambda b, h, i: (b * nq + i, COL_Q // 512 + h)),
            pl.BlockSpec((tq, LANE), lambda b, h, i: (b * nq + i, COL_GATE // LANE)),
            kvblk(COL_KS // LANE), kvblk(COL_VS // LANE), kvblk(COL_KW // LANE), kvblk(COL_VW // LANE),
            cblk, cblk, full(qnw), full(knw), full(ovl), full(expand),
        ],
        out_specs=pl.BlockSpec((tq, NSA_GROUP * HEAD_DIM), lambda b, h, i: (b * nq + i, h)),
        out_shape=jax.ShapeDtypeStruct((T, NSA_WIDTH), BF16),
        scratch_shapes=[pltpu.VMEM((S, HEAD_DIM), BF16), pltpu.VMEM((S, HEAD_DIM), BF16)],
        compiler_params=_params(("parallel", "parallel", "arbitrary")),
        name="nsa",
    )(z, z, z, z, z, z, kc, vc, qnw, knw, ovl, expand)


def _pack_rows(x):
    w = x.shape[1] // 2
    return pltpu.pack_elementwise([x[:, :w], x[:, w:]], packed_dtype=BF16)


def _unpack_rows(p, dtype):
    lo = pltpu.unpack_elementwise(p, index=0, packed_dtype=BF16, unpacked_dtype=F32)
    hi = pltpu.unpack_elementwise(p, index=1, packed_dtype=BF16, unpacked_dtype=F32)
    return jnp.concatenate([lo.astype(dtype), hi.astype(dtype)], axis=1)


ROUTE_E = 0
ROUTE_W = 2
ROUTE_RANK = 4


def _outproj_router_kernel(yab_ref, yc_ref, wo_ref, x_ref, nw_ref, wrh_ref, wrl_ref, br_ref,
                           x1_ref, hnp_ref, route_ref, counts_ref, cnt_ref):
    @pl.when(pl.program_id(0) == 0)
    def _():
        cnt_ref[...] = jnp.zeros_like(cnt_ref)

    ka = yab_ref.shape[1]
    acc = _dot(yab_ref[...], wo_ref[0:ka, :]) + _dot(yc_ref[...], wo_ref[ka:, :])
    x1 = x_ref[...] + acc
    x1_ref[...] = x1
    hn = _rms(x1, nw_ref[...])
    hnp_ref[...] = _pack_rows(hn)
    hi = hn.astype(BF16)
    lo = (hn - hi.astype(F32)).astype(BF16)
    lg = _dot(hi, wrh_ref[...]) + _dot(lo, wrh_ref[...]) + _dot(hi, wrl_ref[...]) + br_ref[...]

    tm = lg.shape[0]
    lane = lax.broadcasted_iota(jnp.int32, (tm, LANE), 1)
    lane_f = lane.astype(F32)
    big = float(LANE)

    def first_max(v):
        m = jnp.max(v, axis=1, keepdims=True)
        idx = jnp.min(jnp.where(v == m, lane_f, big), axis=1, keepdims=True)
        return m, idx

    is_grp = (lane >= N_EXPERTS) & (lane < N_EXPERTS + N_GROUPS_MOE)
    lgm = jnp.where(is_grp, lg, NEG)
    mg, grp_lane = first_max(lgm)
    p_grp = 1.0 / jnp.sum(jnp.where(is_grp, jnp.exp(lgm - mg), 0.0), axis=1, keepdims=True)
    grp = grp_lane - float(N_EXPERTS)
    in_grp = (lane < N_EXPERTS) & ((lane // EXPERTS_PER_GROUP).astype(F32) == grp)
    le = jnp.where(in_grp, lg, NEG)
    m1, i1 = first_max(le)
    le2 = jnp.where(lane_f == i1, NEG, le)
    m2, i2 = first_max(le2)
    e2 = jnp.exp(m2 - m1)
    den = 1.0 + e2
    w1 = p_grp * (1.0 / den)
    w2 = p_grp * (e2 / den)

    onehot = jnp.where((lane_f == i1) | (lane_f == i2), 1.0, 0.0)
    r_i = lax.broadcasted_iota(jnp.int32, (tm, tm), 0)
    c_i = lax.broadcasted_iota(jnp.int32, (tm, tm), 1)
    before = jnp.where(c_i < r_i, 1.0, 0.0).astype(BF16)
    base = cnt_ref[0:1, :] + _dot(before, onehot.astype(BF16))
    r1 = jnp.sum(jnp.where(lane_f == i1, base, 0.0), axis=1, keepdims=True)
    r2 = jnp.sum(jnp.where(lane_f == i2, base, 0.0), axis=1, keepdims=True)
    cnt_ref[0:1, :] = cnt_ref[0:1, :] + jnp.sum(onehot, axis=0, keepdims=True)
    counts_ref[...] = jnp.broadcast_to(cnt_ref[0:1, :], counts_ref.shape)

    route = jnp.zeros((tm, LANE), F32)
    for k, v in enumerate((i1, i2, w1, w2, r1, r2)):
        route = jnp.where(lane == k, v, route)
    route_ref[...] = route


def _outproj_router(yab, yc, wo, x2d, nw, wr_hi, wr_lo, br, tm=512):
    T, D = x2d.shape
    full = lambda a: pl.BlockSpec(a.shape, lambda i, nd=a.ndim: (0,) * nd)
    return pl.pallas_call(
        _outproj_router_kernel,
        grid=(T // tm,),
        in_specs=[
            pl.BlockSpec((tm, yab.shape[1]), lambda i: (i, 0)),
            pl.BlockSpec((tm, yc.shape[1]), lambda i: (i, 0)),
            full(wo),
            pl.BlockSpec((tm, D), lambda i: (i, 0)),
            full(nw), full(wr_hi), full(wr_lo), full(br),
        ],
        out_specs=[pl.BlockSpec((tm, D), lambda i: (i, 0)),
                   pl.BlockSpec((tm, D // 2), lambda i: (i, 0)),
                   pl.BlockSpec((tm, LANE), lambda i: (i, 0)),
                   pl.BlockSpec((8, LANE), lambda i: (0, 0))],
        out_shape=[jax.ShapeDtypeStruct((T, D), F32),
                   jax.ShapeDtypeStruct((T, D // 2), jnp.uint32),
                   jax.ShapeDtypeStruct((T, LANE), F32),
                   jax.ShapeDtypeStruct((8, LANE), F32)],
        scratch_shapes=[pltpu.VMEM((8, LANE), F32)],
        compiler_params=_params(("arbitrary",)),
        name="outproj_router",
    )(yab, yc, wo, x2d, nw, wr_hi, wr_lo, br)


def _row_copy(src_hbm, src_row, dst_ref, dst_row, sem):
    return pltpu.make_async_copy(src_hbm.at[pl.ds(src_row, 1), :], dst_ref.at[pl.ds(dst_row, 1), :], sem)


def _dispatch_kernel(dest_ref, hnp_hbm, xs_hbm, sem, *, chunk):
    i = pl.program_id(0)
    n = pl.num_programs(0)

    def drain(slot):
        for _ in range(2):
            pltpu.make_async_copy(hnp_hbm.at[pl.ds(0, chunk), :], xs_hbm.at[pl.ds(0, chunk), :],
                                  sem.at[slot]).wait()

    def body(j, carry):
        t = i * chunk + j
        _row_copy(hnp_hbm, t, xs_hbm, dest_ref[2 * t], sem.at[i % 2]).start()
        _row_copy(hnp_hbm, t, xs_hbm, dest_ref[2 * t + 1], sem.at[i % 2]).start()
        return carry

    lax.fori_loop(0, chunk, body, 0, unroll=8)

    @pl.when(i > 0)
    def _():
        drain((i + 1) % 2)

    @pl.when(i == n - 1)
    def _():
        drain(i % 2)


def _dispatch(dest, hnp, chunk=512):
    T, W = hnp.shape
    return pl.pallas_call(
        functools.partial(_dispatch_kernel, chunk=chunk),
        grid_spec=pltpu.PrefetchScalarGridSpec(
            num_scalar_prefetch=1,
            grid=(T // chunk,),
            in_specs=[pl.BlockSpec(memory_space=pl.ANY)],
            out_specs=pl.BlockSpec(memory_space=pl.ANY),
            scratch_shapes=[pltpu.SemaphoreType.DMA((2,))],
        ),
        out_shape=jax.ShapeDtypeStruct((2 * T, W), hnp.dtype),
        compiler_params=_params(("arbitrary",)),
        name="dispatch",
    )(dest, hnp)


def _experts_kernel(tile_ref, exp_ref, lo_ref, hi_ref, flag_ref, xs_ref, wg_ref, wu_ref, wd_ref,
                    ys_ref, acc_ref, wgb_ref, wub_ref, wdb_ref, *, tm):
    w = pl.program_id(0)
    lo = lo_ref[w]
    hi = hi_ref[w]
    flags = flag_ref[w]

    @pl.when((flags & 4) != 0)
    def _():
        wgb_ref[...] = wg_ref[0].astype(BF16)
        wub_ref[...] = wu_ref[0].astype(BF16)
        wdb_ref[...] = wd_ref[0].astype(BF16)

    @pl.when((flags & 1) != 0)
    def _():
        acc_ref[...] = jnp.zeros_like(acc_ref)

    @pl.when(hi > lo)
    def _():
        x = _unpack_rows(xs_ref[...], BF16)
        hg = _dot(x, wgb_ref[...])
        hu = _dot(x, wub_ref[...])
        row = tile_ref[w] * tm + lax.broadcasted_iota(jnp.int32, hg.shape, 0)
        h = jnp.where((row >= lo) & (row < hi), _silu(hg) * hu, 0.0)
        acc_ref[...] += _dot(h.astype(BF16), wdb_ref[...])

    @pl.when((flags & 2) != 0)
    def _():
        ys_ref[...] = _pack_rows(acc_ref[...])


def _experts(meta, xs, wg, wu, wd, tm):
    N, W = xs.shape
    E, D, F = wg.shape
    tile_w, exp_w, lo_w, hi_w, flag_w = meta
    n_work = tile_w.shape[0]
    return pl.pallas_call(
        functools.partial(_experts_kernel, tm=tm),
        grid_spec=pltpu.PrefetchScalarGridSpec(
            num_scalar_prefetch=5,
            grid=(n_work,),
            in_specs=[
                pl.BlockSpec((tm, W), lambda w, t, e, lo, hi, f: (t[w], 0)),
                pl.BlockSpec((1, D, F), lambda w, t, e, lo, hi, f: (e[w], 0, 0)),
                pl.BlockSpec((1, D, F), lambda w, t, e, lo, hi, f: (e[w], 0, 0)),
                pl.BlockSpec((1, F, D), lambda w, t, e, lo, hi, f: (e[w], 0, 0)),
            ],
            out_specs=pl.BlockSpec((tm, W), lambda w, t, e, lo, hi, f: (t[w], 0)),
            scratch_shapes=[pltpu.VMEM((tm, D), F32), pltpu.VMEM((D, F), BF16),
                            pltpu.VMEM((D, F), BF16), pltpu.VMEM((F, D), BF16)],
        ),
        out_shape=jax.ShapeDtypeStruct((N, W), xs.dtype),
        compiler_params=_params(("arbitrary",)),
        name="experts",
    )(tile_w, exp_w, lo_w, hi_w, flag_w, xs, wg, wu, wd)


def _work_items(counts, n_rows, tm):
    E = counts.shape[0]
    n_tiles = n_rows // tm
    n_work = n_tiles + E - 1
    start = jnp.cumsum(counts) - counts
    end = start + counts
    first_tile = start // tm
    last_tile = jnp.maximum(end - 1, 0) // tm
    n_e = jnp.where(counts > 0, last_tile - first_tile + 1, 0)
    wend = jnp.cumsum(n_e)
    wstart = wend - n_e
    total = wend[-1]
    w = jnp.arange(n_work, dtype=jnp.int32)
    wc = jnp.minimum(w, total - 1)
    ew = jnp.sum((wc[:, None] >= wend[None, :]).astype(jnp.int32), axis=1)
    tile_w = first_tile[ew] + (wc - wstart[ew])
    valid = w < total
    lo = jnp.where(valid, jnp.maximum(start[ew], tile_w * tm), 0)
    hi = jnp.where(valid, jnp.minimum(end[ew], (tile_w + 1) * tm), 0)
    prev_tile = jnp.concatenate([jnp.full((1,), -1, jnp.int32), tile_w[:-1]])
    next_tile = jnp.concatenate([tile_w[1:], jnp.full((1,), -1, jnp.int32)])
    prev_e = jnp.concatenate([jnp.full((1,), -1, jnp.int32), ew[:-1]])
    first = valid & (tile_w != prev_tile)
    last = valid & ((tile_w != next_tile) | (w == total - 1))
    new_e = ew != prev_e
    flags = first.astype(jnp.int32) + 2 * last.astype(jnp.int32) + 4 * new_e.astype(jnp.int32)
    i32 = lambda a: a.astype(jnp.int32)
    return i32(tile_w), i32(ew), i32(lo), i32(hi), i32(flags)


def _combine_kernel(dest_ref, x1_ref, route_ref, ys_hbm, o_ref, buf_ref, sem, *, tt):
    i = pl.program_id(0)
    n = pl.num_programs(0)

    def issue(step, slot):
        def body(j, carry):
            t = step * tt + j
            _row_copy(ys_hbm, dest_ref[2 * t], buf_ref.at[slot, 0], j, sem.at[slot]).start()
            _row_copy(ys_hbm, dest_ref[2 * t + 1], buf_ref.at[slot, 1], j, sem.at[slot]).start()
            return carry
        lax.fori_loop(0, tt, body, 0, unroll=8)

    @pl.when(i == 0)
    def _():
        issue(0, 0)

    @pl.when(i + 1 < n)
    def _():
        issue(i + 1, (i + 1) % 2)

    slot = i % 2
    for k in range(2):
        pltpu.make_async_copy(ys_hbm.at[pl.ds(0, tt), :], buf_ref.at[slot, k], sem.at[slot]).wait()

    lane = lax.broadcasted_iota(jnp.int32, route_ref.shape, 1)
    route = route_ref[...]
    w0 = jnp.sum(jnp.where(lane == ROUTE_W, route, 0.0), axis=1, keepdims=True)
    w1 = jnp.sum(jnp.where(lane == ROUTE_W + 1, route, 0.0), axis=1, keepdims=True)
    y0 = _unpack_rows(buf_ref[slot, 0], F32)
    y1 = _unpack_rows(buf_ref[slot, 1], F32)
    o_ref[...] = x1_ref[...] + (w0 * y0 + w1 * y1)


def _combine(dest, x1, route, ys, tt=256):
    T, D = x1.shape
    W = ys.shape[1]
    return pl.pallas_call(
        functools.partial(_combine_kernel, tt=tt),
        grid_spec=pltpu.PrefetchScalarGridSpec(
            num_scalar_prefetch=1,
            grid=(T // tt,),
            in_specs=[
                pl.BlockSpec((tt, D), lambda i, d: (i, 0)),
                pl.BlockSpec((tt, LANE), lambda i, d: (i, 0)),
                pl.BlockSpec(memory_space=pl.ANY),
            ],
            out_specs=pl.BlockSpec((tt, D), lambda i, d: (i, 0)),
            scratch_shapes=[pltpu.VMEM((2, 2, tt, W), ys.dtype), pltpu.SemaphoreType.DMA((2,))],
        ),
        out_shape=jax.ShapeDtypeStruct((T, D), F32),
        compiler_params=_params(("arbitrary",)),
        name="combine",
    )(dest, x1, route, ys)


def _moe(x1, hnp, route, counts8, wg, wu, wd, tm=256):
    T = x1.shape[0]
    E = wg.shape[0]
    counts = counts8[0, :E].astype(jnp.int32)
    eid = route[:, ROUTE_E:ROUTE_E + 2].astype(jnp.int32)
    rank = route[:, ROUTE_RANK:ROUTE_RANK + 2].astype(jnp.int32)
    start = jnp.cumsum(counts) - counts
    onehot = eid[..., None] == jnp.arange(E, dtype=jnp.int32)
    dest = (jnp.sum(jnp.where(onehot, start, 0), axis=-1) + rank).reshape(2 * T)
    xs = _dispatch(dest, hnp)
    ys = _experts(_work_items(counts, 2 * T, tm), xs, wg, wu, wd, tm)
    return _combine(dest, x1, route, ys)


def kernel(x, norm1_w, w_in, pool_w, pool_scale, conv_w, cmp_pe_k, cmp_w1_k, cmp_w2_k, cmp_pe_v, cmp_w1_v, cmp_w2_v, q_norm_w, k_norm_w, w_out, norm2_w, router_grp_w, router_grp_b, router_exp_w, router_exp_b, exp_w_gate, exp_w_up, exp_w_down):
    B, S, D = x.shape
    depth = w_in.shape[0]
    T = B * S
    n16 = S // CMP_STRIDE
    xf = x.reshape(T, D)

    def w1_pair(w1):
        return jnp.concatenate([w1[:CMP_STRIDE], w1[CMP_STRIDE:]], axis=-1).astype(BF16)

    def pe_rows(pe):
        return jnp.broadcast_to(pe.reshape(1, CMP_LEN * HEAD_DIM), (8, CMP_LEN * HEAD_DIM)).astype(BF16)

    for l in range(depth):
        w_in_p = jnp.pad(w_in[l], ((0, 0), (0, D_IN_PAD - D_IN))).astype(BF16)
        z = _inproj(xf, norm1_w[l].reshape(1, D), w_in_p)

        yab = _mix_ab(z, pool_w[l].astype(BF16), pool_scale[l].reshape(1, POOL_WIDTH), conv_w[l], B, S)

        xkv = z[:, COL_KC:COL_KS].reshape(B, n16, CMP_STRIDE * 512)
        kc, vc = _compress(
            xkv, w1_pair(cmp_w1_k[l]), w1_pair(cmp_w1_v[l]), pe_rows(cmp_pe_k[l]), pe_rows(cmp_pe_v[l]),
            cmp_w1_k[l].reshape(CMP_LEN * HEAD_DIM, CMP_HIDDEN).astype(BF16),
            cmp_w1_v[l].reshape(CMP_LEN * HEAD_DIM, CMP_HIDDEN).astype(BF16),
            cmp_w2_k[l].astype(BF16), cmp_w2_v[l].astype(BF16), k_norm_w[l])
        yc = _nsa(z, kc, vc, q_norm_w[l].reshape(1, HEAD_DIM), k_norm_w[l], B, S)

        wr = jnp.concatenate([router_exp_w[l], router_grp_w[l]], axis=1)
        wr = jnp.pad(wr, ((0, 0), (0, LANE - wr.shape[1])))
        wr_hi = wr.astype(BF16)
        wr_lo = (wr - wr_hi.astype(F32)).astype(BF16)
        br = jnp.concatenate([router_exp_b[l], router_grp_b[l]])
        br = jnp.pad(br, (0, LANE - br.shape[0])).reshape(1, LANE)
        x1, hnp, route, counts8 = _outproj_router(yab, yc, w_out[l].astype(BF16), xf,
                                                  norm2_w[l].reshape(1, D), wr_hi, wr_lo, br)

        xf = _moe(x1, hnp, route, counts8, exp_w_gate[l], exp_w_up[l], exp_w_down[l])
    return xf.reshape(B, S, D)
```

```python
import functools

import numpy as np
import jax
import jax.numpy as jnp
from jax import lax
from jax.experimental import pallas as pl
from jax.experimental.pallas import tpu as pltpu

F32 = jnp.float32
BF16 = jnp.bfloat16

POOL_WINDOWS = (2, 4, 8, 16)
LANE = 128
POOL_WIDTH = 512
CONV_WIDTH = 512
CONV_K = 3
NSA_WIDTH = 1024
HEAD_DIM = 128
NSA_KV_HEADS = 2
NSA_GROUP = 4
N_BRANCH = 3
CMP_LEN = 32
CMP_STRIDE = 16
CMP_HIDDEN = 256
SEL_LEN = 64
N_SELECT = 16
SEL_FORCE = 1.0e4
WINDOW = 512
N_GROUPS_MOE = 4
EXPERTS_PER_GROUP = 8
N_EXPERTS = 32
D_EXPERT = 256
EPS = 1e-6
NEG = -1e30

COL_U = 0
COL_B = 512
COL_C = 1024
COL_V = 1536
COL_Q = 2048
COL_KC = 3072
COL_KS = 3584
COL_VS = 3840
COL_KW = 4096
COL_VW = 4352
COL_GATE = 4608
D_IN = 4632
D_IN_PAD = 5120

VMEM_LIMIT = 56 * 1024 * 1024


def _params(sem):
    return pltpu.CompilerParams(dimension_semantics=sem, vmem_limit_bytes=VMEM_LIMIT)


def _rms(x, w):
    return x * lax.rsqrt(jnp.mean(x * x, axis=-1, keepdims=True) + EPS) * w


def _silu(x):
    return x / (1.0 + jnp.exp(-x))


def _dot(a, b):
    return jnp.dot(a, b, preferred_element_type=F32)


def _dot_t(a, b):
    return lax.dot_general(a, b, (((1,), (1,)), ((), ())), preferred_element_type=F32)


def _split3_dot(a, b):
    hi = a.astype(BF16)
    r1 = a - hi.astype(F32)
    mid = r1.astype(BF16)
    lo = (r1 - mid.astype(F32)).astype(BF16)
    return _dot(hi, b) + _dot(mid, b) + _dot(lo, b)


def _inproj_kernel(x_ref, nw_ref, w_ref, o_ref, xn_ref):
    @pl.when(pl.program_id(1) == 0)
    def _():
        xn_ref[...] = _rms(x_ref[...], nw_ref[...]).astype(BF16)

    o_ref[...] = _dot(xn_ref[...], w_ref[...]).astype(o_ref.dtype)


def _inproj(x2d, nw, w_bf16, tm=512, tn=512):
    T, D = x2d.shape
    N = w_bf16.shape[1]
    return pl.pallas_call(
        _inproj_kernel,
        grid=(T // tm, N // tn),
        in_specs=[
            pl.BlockSpec((tm, D), lambda i, j: (i, 0)),
            pl.BlockSpec((1, D), lambda i, j: (0, 0)),
            pl.BlockSpec((D, tn), lambda i, j: (0, j)),
        ],
        out_specs=pl.BlockSpec((tm, tn), lambda i, j: (i, j)),
        out_shape=jax.ShapeDtypeStruct((T, N), BF16),
        scratch_shapes=[pltpu.VMEM((tm, D), BF16)],
        compiler_params=_params(("parallel", "arbitrary")),
        name="inproj",
    )(x2d, nw, w_bf16)


def _mix_ab_kernel(u_ref, b_ref, c_ref, v_ref, pw_ref, ps_ref, cw_ref, o_ref):
    S = u_ref.shape[0]
    row = lax.broadcasted_iota(jnp.int32, (S, LANE), 0)

    def shift(a, k):
        return jnp.where(row >= k, pltpu.roll(a, k, axis=0), 0.0)

    for g, w in enumerate(POOL_WINDOWS):
        sl = slice(g * LANE, (g + 1) * LANE)
        u = u_ref[:, sl].astype(F32)
        s = u
        k = 1
        while k < w:
            s = s + shift(s, k)
            k *= 2
        cnt = jnp.minimum(row + 1, w).astype(F32)
        mixed = (s / cnt - u).astype(BF16)
        y = _dot(mixed, pw_ref[g]) * ps_ref[:, sl]
        o_ref[:, sl] = y.astype(o_ref.dtype)

    for h in range(CONV_WIDTH // LANE):
        sl = slice(h * LANE, (h + 1) * LANE)
        u2 = c_ref[:, sl].astype(F32) * v_ref[:, sl].astype(F32)
        y = cw_ref[0:1, sl] * shift(u2, 2)
        y = y + cw_ref[1:2, sl] * shift(u2, 1)
        y = y + cw_ref[2:3, sl] * u2
        y = b_ref[:, sl].astype(F32) * y
        o_ref[:, POOL_WIDTH + h * LANE:POOL_WIDTH + (h + 1) * LANE] = y.astype(o_ref.dtype)


def _mix_ab(z, pool_w_bf16, pool_scale, conv_w, B, S):
    T = z.shape[0]
    blk = lambda c: pl.BlockSpec((S, 512), lambda b, c=c: (b, c))
    return pl.pallas_call(
        _mix_ab_kernel,
        grid=(B,),
        in_specs=[
            blk(COL_U // 512), blk(COL_B // 512), blk(COL_C // 512), blk(COL_V // 512),
            pl.BlockSpec((4, LANE, LANE), lambda b: (0, 0, 0)),
            pl.BlockSpec((1, POOL_WIDTH), lambda b: (0, 0)),
            pl.BlockSpec((CONV_K, CONV_WIDTH), lambda b: (0, 0)),
        ],
        out_specs=pl.BlockSpec((S, POOL_WIDTH + CONV_WIDTH), lambda b: (b, 0)),
        out_shape=jax.ShapeDtypeStruct((T, POOL_WIDTH + CONV_WIDTH), BF16),
        compiler_params=_params(("parallel",)),
        name="mix_ab",
    )(z, z, z, z, pool_w_bf16, pool_scale, conv_w)


def _compress_kernel(x_ref, w1k_ref, w1v_ref, pek_ref, pev_ref, w1kf_ref, w1vf_ref,
                     w2k_ref, w2v_ref, knw_ref, kc_ref, vc_ref):
    n_half = CMP_LEN // CMP_STRIDE
    streams = ((w1k_ref, pek_ref, w1kf_ref, w2k_ref, kc_ref),
               (w1v_ref, pev_ref, w1vf_ref, w2v_ref, vc_ref))
    for which, (w1_ref, pe_ref, w1f_ref, w2_ref, out_ref) in enumerate(streams):
        pe_term = _dot(pe_ref[...], w1f_ref[...])[0:1, :]
        for h in range(NSA_KV_HEADS):
            acc = None
            for l in range(CMP_STRIDE):
                c0 = ((l * 2 + which) * NSA_KV_HEADS + h) * HEAD_DIM
                part = _dot(x_ref[0, :, c0:c0 + HEAD_DIM], w1_ref[l])
                acc = part if acc is None else acc + part
            n16 = acc.shape[0]
            first = acc[:, :CMP_HIDDEN]
            second = acc[:, CMP_HIDDEN:]
            assert n_half == 2
            hid = first + pltpu.roll(second, n16 - 1, axis=0) + pe_term
            out = _dot(_silu(hid).astype(BF16), w2_ref[...])
            if which == 0:
                out = _rms(out, knw_ref[0:1, :])
            out_ref[0, h] = out.astype(out_ref.dtype)


def _compress(xkv, w1k, w1v, pek8, pev8, w1kf, w1vf, w2k, w2v, knw):
    B, n16, W = xkv.shape
    full = lambda a: pl.BlockSpec(a.shape, lambda b, nd=a.ndim: (0,) * nd)
    out_sds = jax.ShapeDtypeStruct((B, NSA_KV_HEADS, n16, HEAD_DIM), BF16)
    out_spec = pl.BlockSpec((1, NSA_KV_HEADS, n16, HEAD_DIM), lambda b: (b, 0, 0, 0))
    return pl.pallas_call(
        _compress_kernel,
        grid=(B,),
        in_specs=[pl.BlockSpec((1, n16, W), lambda b: (b, 0, 0)),
                  full(w1k), full(w1v), full(pek8), full(pev8), full(w1kf), full(w1vf),
                  full(w2k), full(w2v), full(knw)],
        out_specs=[out_spec, out_spec],
        out_shape=[out_sds, out_sds],
        compiler_params=_params(("parallel",)),
        name="compress",
    )(xkv, w1k, w1v, pek8, pev8, w1kf, w1vf, w2k, w2v, knw)


def _nsa_kernel(zq_ref, zg_ref, ks_ref, vs_ref, kw_ref, vw_ref, kc_ref, vc_ref,
                qnw_ref, knw_ref, ovl_ref, exp_ref, o_ref, ksn_ref, kwn_ref, *, tq):
    G = NSA_GROUP
    hkv = pl.program_id(1)
    i = pl.program_id(2)
    tk = tq

    @pl.when(i == 0)
    def _():
        ksn_ref[...] = _rms(ks_ref[...].astype(F32), knw_ref[1:2, :]).astype(BF16)
        kwn_ref[...] = _rms(kw_ref[...].astype(F32), knw_ref[2:3, :]).astype(BF16)

    qs = []
    for g in range(G):
        qg = zq_ref[:, g * HEAD_DIM:(g + 1) * HEAD_DIM].astype(F32)
        qg = _rms(qg, qnw_ref[...]) * (HEAD_DIM ** -0.5)
        qs.append(qg.astype(BF16))
    q4 = jnp.concatenate(qs, axis=0)

    t0 = i * tq
    tpos_c = t0 + (lax.broadcasted_iota(jnp.int32, (G * tq, LANE), 0) & (tq - 1))

    s = _dot_t(q4, kc_ref[0, 0])
    n_idx = lax.broadcasted_iota(jnp.int32, (G * tq, LANE), 1)
    cm = (n_idx * CMP_STRIDE + (CMP_LEN - 1)) <= tpos_c
    sm = jnp.where(cm, s, NEG)
    e = jnp.exp(sm - jnp.max(sm, axis=1, keepdims=True))
    p = e / jnp.sum(e, axis=1, keepdims=True)
    p = jnp.where(cm, p, 0.0)
    o_cmp = _dot(p.astype(BF16), vc_ref[0, 0])

    psum = p[0:tq]
    for g in range(1, G):
        psum = psum + p[g * tq:(g + 1) * tq]
    imp = _split3_dot(psum, ovl_ref[...])
    n_sel = exp_ref.shape[0] // SEL_LEN
    n_top = min(N_SELECT, n_sel)
    j_idx = lax.broadcasted_iota(jnp.int32, (tq, LANE), 1)
    t_row = t0 + lax.broadcasted_iota(jnp.int32, (tq, LANE), 0)
    forced = (j_idx == (t_row // SEL_LEN)) | (j_idx == 0)
    valid = (j_idx * SEL_LEN) <= t_row
    score = jnp.where(forced, SEL_FORCE, jnp.where(valid, imp, -1.0))
    score = jnp.where(j_idx < n_sel, score, -2.0)
    rank = jnp.zeros((tq, LANE), F32)
    for c in range(n_sel):
        col = jnp.broadcast_to(score[:, c:c + 1], (tq, LANE))
        beats = jnp.where(col > score, 1.0, jnp.where((col == score) & (j_idx > c), 1.0, 0.0))
        rank = rank + beats
    selm = jnp.where((rank < n_top) & (j_idx < n_sel), 1.0, 0.0).astype(BF16)

    q_loc = lax.broadcasted_iota(jnp.int32, (tq, tk), 0)
    k_loc = lax.broadcasted_iota(jnp.int32, (tq, tk), 1)

    def sel_tile(kt, carry, diagonal):
        m, l, acc = carry
        k0 = pl.multiple_of(kt * tk, tk)
        kblk = ksn_ref[pl.ds(k0, tk), :]
        vblk = vs_ref[pl.ds(k0, tk), :]
        em = _dot_t(selm, exp_ref[pl.ds(k0, tk), :])
        bias = (em - 1.0) * (-NEG)
        if diagonal:
            bias = jnp.where(k_loc <= q_loc, bias, NEG)
        sc = _dot_t(q4, kblk).reshape(G, tq, tk) + bias[None]
        m_new = jnp.maximum(m, jnp.max(sc, axis=2, keepdims=True))
        alpha = jnp.exp(m - m_new)
        pr = jnp.exp(sc - m_new)
        l = alpha * l + jnp.sum(pr, axis=2, keepdims=True)
        acc = alpha.reshape(G * tq, 1) * acc + _dot(pr.reshape(G * tq, tk).astype(BF16), vblk)
        return m_new, l, acc

    init = (jnp.full((G, tq, 1), NEG, F32), jnp.zeros((G, tq, 1), F32),
            jnp.zeros((G * tq, HEAD_DIM), F32))
    carry = lax.fori_loop(0, i, lambda kt, c: sel_tile(kt, c, False), init)
    _, l_s, acc_s = sel_tile(i, carry, True)
    o_sel = acc_s / l_s.reshape(G * tq, 1)

    wk = WINDOW + tq
    ws = pl.multiple_of(jnp.maximum(t0 - WINDOW, 0), tq)
    kpos = ws + lax.broadcasted_iota(jnp.int32, (tq, wk), 1)
    dist = (t0 + lax.broadcasted_iota(jnp.int32, (tq, wk), 0)) - kpos
    bias_w = jnp.where((dist >= 0) & (dist < WINDOW), 0.0, NEG)
    sc = _dot_t(q4, kwn_ref[pl.ds(ws, wk), :]).reshape(G, tq, wk) + bias_w[None]
    pr = jnp.exp(sc - jnp.max(sc, axis=2, keepdims=True))
    l_w = jnp.sum(pr, axis=2, keepdims=True)
    acc_w = _dot(pr.reshape(G * tq, wk).astype(BF16), vw_ref[pl.ds(ws, wk), :])
    o_win = acc_w / l_w.reshape(G * tq, 1)

    gates = 1.0 / (1.0 + jnp.exp(-zg_ref[...].astype(F32)))
    lane = lax.broadcasted_iota(jnp.int32, (tq, LANE), 1)

    def gate_col(c):
        return jnp.sum(jnp.where(lane == c, gates, 0.0), axis=1, keepdims=True)

    n_heads = NSA_KV_HEADS * G
    for g in range(G):
        c = hkv * G + g
        rows = slice(g * tq, (g + 1) * tq)
        out = (gate_col(c) * o_cmp[rows] + gate_col(n_heads + c) * o_sel[rows]
               + gate_col(2 * n_heads + c) * o_win[rows])
        o_ref[:, g * HEAD_DIM:(g + 1) * HEAD_DIM] = out.astype(o_ref.dtype)


def _nsa(z, kc, vc, qnw, knw, B, S, tq=256):
    T = z.shape[0]
    nq = S // tq
    n_c = (S - CMP_LEN) // CMP_STRIDE + 1
    n_sel = S // SEL_LEN
    assert S % tq == 0 and tq & (tq - 1) == 0 and n_sel <= LANE and n_c <= LANE and WINDOW % tq == 0
    assert S >= WINDOW + tq
    ci = np.arange(LANE)[:, None] * CMP_STRIDE
    sj = np.arange(LANE)[None, :] * SEL_LEN
    ovl = ((ci < sj + SEL_LEN) & (ci + CMP_LEN > sj) & (np.arange(LANE)[:, None] < n_c)
           & (np.arange(LANE)[None, :] < n_sel))
    ovl = jnp.asarray(ovl.astype(np.float32), BF16)
    expand = (np.arange(S)[:, None] // SEL_LEN) == np.arange(LANE)[None, :]
    expand = jnp.asarray(expand.astype(np.float32), BF16)

    kvblk = lambda c: pl.BlockSpec((S, HEAD_DIM), lambda b, h, i, c=c: (b, c + h))
    cblk = pl.BlockSpec((1, 1, LANE, HEAD_DIM), lambda b, h, i: (b, h, 0, 0))
    full = lambda a: pl.BlockSpec(a.shape, lambda b, h, i, nd=a.ndim: (0,) * nd)
    return pl.pallas_call(
        functools.partial(_nsa_kernel, tq=tq),
        grid=(B, NSA_KV_HEADS, nq),
        in_specs=[
            pl.BlockSpec((tq, NSA_GROUP * HEAD_DIM), lambda b, h, i: (b * nq + i, COL_Q // 512 + h)),
            pl.BlockSpec((tq, LANE), lambda b, h, i: (b * nq + i, COL_GATE // LANE)),
            kvblk(COL_KS // LANE), kvblk(COL_VS // LANE), kvblk(COL_KW // LANE), kvblk(COL_VW // LANE),
            cblk, cblk, full(qnw), full(knw), full(ovl), full(expand),
        ],
        out_specs=pl.BlockSpec((tq, NSA_GROUP * HEAD_DIM), lambda b, h, i: (b * nq + i, h)),
        out_shape=jax.ShapeDtypeStruct((T, NSA_WIDTH), BF16),
        scratch_shapes=[pltpu.VMEM((S, HEAD_DIM), BF16), pltpu.VMEM((S, HEAD_DIM), BF16)],
        compiler_params=_params(("parallel", "parallel", "arbitrary")),
        name="nsa",
    )(z, z, z, z, z, z, kc, vc, qnw, knw, ovl, expand)


def _pack_rows(x):
    w = x.shape[1] // 2
    return pltpu.pack_elementwise([x[:, :w], x[:, w:]], packed_dtype=BF16)


def _unpack_rows(p, dtype):
    lo = pltpu.unpack_elementwise(p, index=0, packed_dtype=BF16, unpacked_dtype=F32)
    hi = pltpu.unpack_elementwise(p, index=1, packed_dtype=BF16, unpacked_dtype=F32)
    return jnp.concatenate([lo.astype(dtype), hi.astype(dtype)], axis=1)


ROUTE_E = 0
ROUTE_W = 2
ROUTE_RANK = 4


def _outproj_router_kernel(yab_ref, yc_ref, wo_ref, x_ref, nw_ref, wrh_ref, wrl_ref, br_ref,
                           x1_ref, hnp_ref, route_ref, counts_ref, cnt_ref):
    @pl.when(pl.program_id(0) == 0)
    def _():
        cnt_ref[...] = jnp.zeros_like(cnt_ref)

    ka = yab_ref.shape[1]
    acc = _dot(yab_ref[...], wo_ref[0:ka, :]) + _dot(yc_ref[...], wo_ref[ka:, :])
    x1 = x_ref[...] + acc
    x1_ref[...] = x1
    hn = _rms(x1, nw_ref[...])
    hnp_ref[...] = _pack_rows(hn)
    hi = hn.astype(BF16)
    lo = (hn - hi.astype(F32)).astype(BF16)
    lg = _dot(hi, wrh_ref[...]) + _dot(lo, wrh_ref[...]) + _dot(hi, wrl_ref[...]) + br_ref[...]

    tm = lg.shape[0]
    lane = lax.broadcasted_iota(jnp.int32, (tm, LANE), 1)
    lane_f = lane.astype(F32)
    big = float(LANE)

    def first_max(v):
        m = jnp.max(v, axis=1, keepdims=True)
        idx = jnp.min(jnp.where(v == m, lane_f, big), axis=1, keepdims=True)
        return m, idx

    is_grp = (lane >= N_EXPERTS) & (lane < N_EXPERTS + N_GROUPS_MOE)
    lgm = jnp.where(is_grp, lg, NEG)
    mg, grp_lane = first_max(lgm)
    p_grp = 1.0 / jnp.sum(jnp.where(is_grp, jnp.exp(lgm - mg), 0.0), axis=1, keepdims=True)
    grp = grp_lane - float(N_EXPERTS)
    in_grp = (lane < N_EXPERTS) & ((lane // EXPERTS_PER_GROUP).astype(F32) == grp)
    le = jnp.where(in_grp, lg, NEG)
    m1, i1 = first_max(le)
    le2 = jnp.where(lane_f == i1, NEG, le)
    m2, i2 = first_max(le2)
    e2 = jnp.exp(m2 - m1)
    den = 1.0 + e2
    w1 = p_grp * (1.0 / den)
    w2 = p_grp * (e2 / den)

    onehot = jnp.where((lane_f == i1) | (lane_f == i2), 1.0, 0.0)
    r_i = lax.broadcasted_iota(jnp.int32, (tm, tm), 0)
    c_i = lax.broadcasted_iota(jnp.int32, (tm, tm), 1)
    before = jnp.where(c_i < r_i, 1.0, 0.0).astype(BF16)
    base = cnt_ref[0:1, :] + _dot(before, onehot.astype(BF16))
    r1 = jnp.sum(jnp.where(lane_f == i1, base, 0.0), axis=1, keepdims=True)
    r2 = jnp.sum(jnp.where(lane_f == i2, base, 0.0), axis=1, keepdims=True)
    cnt_ref[0:1, :] = cnt_ref[0:1, :] + jnp.sum(onehot, axis=0, keepdims=True)
    counts_ref[...] = jnp.broadcast_to(cnt_ref[0:1, :], counts_ref.shape)

    route = jnp.zeros((tm, LANE), F32)
    for k, v in enumerate((i1, i2, w1, w2, r1, r2)):
        route = jnp.where(lane == k, v, route)
    route_ref[...] = route


def _outproj_router(yab, yc, wo, x2d, nw, wr_hi, wr_lo, br, tm=512):
    T, D = x2d.shape
    full = lambda a: pl.BlockSpec(a.shape, lambda i, nd=a.ndim: (0,) * nd)
    return pl.pallas_call(
        _outproj_router_kernel,
        grid=(T // tm,),
        in_specs=[
            pl.BlockSpec((tm, yab.shape[1]), lambda i: (i, 0)),
            pl.BlockSpec((tm, yc.shape[1]), lambda i: (i, 0)),
            full(wo),
            pl.BlockSpec((tm, D), lambda i: (i, 0)),
            full(nw), full(wr_hi), full(wr_lo), full(br),
        ],
        out_specs=[pl.BlockSpec((tm, D), lambda i: (i, 0)),
                   pl.BlockSpec((tm, D // 2), lambda i: (i, 0)),
                   pl.BlockSpec((tm, LANE), lambda i: (i, 0)),
                   pl.BlockSpec((8, LANE), lambda i: (0, 0))],
        out_shape=[jax.ShapeDtypeStruct((T, D), F32),
                   jax.ShapeDtypeStruct((T, D // 2), jnp.uint32),
                   jax.ShapeDtypeStruct((T, LANE), F32),
                   jax.ShapeDtypeStruct((8, LANE), F32)],
        scratch_shapes=[pltpu.VMEM((8, LANE), F32)],
        compiler_params=_params(("arbitrary",)),
        name="outproj_router",
    )(yab, yc, wo, x2d, nw, wr_hi, wr_lo, br)


def _row_copy(src_hbm, src_row, dst_ref, dst_row, sem):
    return pltpu.make_async_copy(src_hbm.at[pl.ds(src_row, 1), :], dst_ref.at[pl.ds(dst_row, 1), :], sem)


def _dispatch_kernel(dest_ref, hnp_ref, xs_hbm, sem, *, chunk):
    i = pl.program_id(0)

    def body(j, carry):
        t = i * chunk + j
        _row_copy(hnp_ref, j, xs_hbm, dest_ref[2 * t], sem).start()
        _row_copy(hnp_ref, j, xs_hbm, dest_ref[2 * t + 1], sem).start()
        return carry

    lax.fori_loop(0, chunk, body, 0, unroll=8)
    for _ in range(2):
        pltpu.make_async_copy(hnp_ref, xs_hbm.at[pl.ds(0, chunk), :], sem).wait()


def _dispatch(dest, hnp, chunk=1024):
    T, W = hnp.shape
    return pl.pallas_call(
        functools.partial(_dispatch_kernel, chunk=chunk),
        grid_spec=pltpu.PrefetchScalarGridSpec(
            num_scalar_prefetch=1,
            grid=(T // chunk,),
            in_specs=[pl.BlockSpec((chunk, W), lambda i, d: (i, 0))],
            out_specs=pl.BlockSpec(memory_space=pl.ANY),
            scratch_shapes=[pltpu.SemaphoreType.DMA(())],
        ),
        out_shape=jax.ShapeDtypeStruct((2 * T, W), hnp.dtype),
        compiler_params=_params(("arbitrary",)),
        name="dispatch",
    )(dest, hnp)


def _experts_kernel(tile_ref, exp_ref, lo_ref, hi_ref, flag_ref, xs_ref, wg_ref, wu_ref, wd_ref,
                    ys_ref, acc_ref, wgb_ref, wub_ref, wdb_ref, *, tm):
    w = pl.program_id(0)
    lo = lo_ref[w]
    hi = hi_ref[w]
    flags = flag_ref[w]

    @pl.when((flags & 4) != 0)
    def _():
        wgb_ref[...] = wg_ref[0].astype(BF16)
        wub_ref[...] = wu_ref[0].astype(BF16)
        wdb_ref[...] = wd_ref[0].astype(BF16)

    @pl.when((flags & 1) != 0)
    def _():
        acc_ref[...] = jnp.zeros_like(acc_ref)

    @pl.when(hi > lo)
    def _():
        x = _unpack_rows(xs_ref[...], BF16)
        hg = _dot(x, wgb_ref[...])
        hu = _dot(x, wub_ref[...])
        row = tile_ref[w] * tm + lax.broadcasted_iota(jnp.int32, hg.shape, 0)
        h = jnp.where((row >= lo) & (row < hi), _silu(hg) * hu, 0.0)
        acc_ref[...] += _dot(h.astype(BF16), wdb_ref[...])

    @pl.when((flags & 2) != 0)
    def _():
        ys_ref[...] = _pack_rows(acc_ref[...])


def _experts(meta, xs, wg, wu, wd, tm):
    N, W = xs.shape
    E, D, F = wg.shape
    tile_w, exp_w, lo_w, hi_w, flag_w = meta
    n_work = tile_w.shape[0]
    return pl.pallas_call(
        functools.partial(_experts_kernel, tm=tm),
        grid_spec=pltpu.PrefetchScalarGridSpec(
            num_scalar_prefetch=5,
            grid=(n_work,),
            in_specs=[
                pl.BlockSpec((tm, W), lambda w, t, e, lo, hi, f: (t[w], 0)),
                pl.BlockSpec((1, D, F), lambda w, t, e, lo, hi, f: (e[w], 0, 0)),
                pl.BlockSpec((1, D, F), lambda w, t, e, lo, hi, f: (e[w], 0, 0)),
                pl.BlockSpec((1, F, D), lambda w, t, e, lo, hi, f: (e[w], 0, 0)),
            ],
            out_specs=pl.BlockSpec((tm, W), lambda w, t, e, lo, hi, f: (t[w], 0)),
            scratch_shapes=[pltpu.VMEM((tm, D), F32), pltpu.VMEM((D, F), BF16),
                            pltpu.VMEM((D, F), BF16), pltpu.VMEM((F, D), BF16)],
        ),
        out_shape=jax.ShapeDtypeStruct((N, W), xs.dtype),
        compiler_params=_params(("arbitrary",)),
        name="experts",
    )(tile_w, exp_w, lo_w, hi_w, flag_w, xs, wg, wu, wd)


def _work_items(counts, n_rows, tm):
    E = counts.shape[0]
    n_tiles = n_rows // tm
    n_work = n_tiles + E - 1
    start = jnp.cumsum(counts) - counts
    end = start + counts
    first_tile = start // tm
    last_tile = jnp.maximum(end - 1, 0) // tm
    n_e = jnp.where(counts > 0, last_tile - first_tile + 1, 0)
    wend = jnp.cumsum(n_e)
    wstart = wend - n_e
    total = wend[-1]
    w = jnp.arange(n_work, dtype=jnp.int32)
    wc = jnp.minimum(w, total - 1)
    ew = jnp.sum((wc[:, None] >= wend[None, :]).astype(jnp.int32), axis=1)
    tile_w = first_tile[ew] + (wc - wstart[ew])
    valid = w < total
    lo = jnp.where(valid, jnp.maximum(start[ew], tile_w * tm), 0)
    hi = jnp.where(valid, jnp.minimum(end[ew], (tile_w + 1) * tm), 0)
    prev_tile = jnp.concatenate([jnp.full((1,), -1, jnp.int32), tile_w[:-1]])
    next_tile = jnp.concatenate([tile_w[1:], jnp.full((1,), -1, jnp.int32)])
    prev_e = jnp.concatenate([jnp.full((1,), -1, jnp.int32), ew[:-1]])
    first = valid & (tile_w != prev_tile)
    last = valid & ((tile_w != next_tile) | (w == total - 1))
    new_e = ew != prev_e
    flags = first.astype(jnp.int32) + 2 * last.astype(jnp.int32) + 4 * new_e.astype(jnp.int32)
    i32 = lambda a: a.astype(jnp.int32)
    return i32(tile_w), i32(ew), i32(lo), i32(hi), i32(flags)


def _combine_kernel(dest_ref, x1_ref, route_ref, ys_hbm, o_ref, buf_ref, sem, *, tt):
    i = pl.program_id(0)
    n = pl.num_programs(0)

    def issue(step, slot):
        def body(j, carry):
            t = step * tt + j
            _row_copy(ys_hbm, dest_ref[2 * t], buf_ref.at[slot, 0], j, sem.at[slot]).start()
            _row_copy(ys_hbm, dest_ref[2 * t + 1], buf_ref.at[slot, 1], j, sem.at[slot]).start()
            return carry
        lax.fori_loop(0, tt, body, 0, unroll=8)

    @pl.when(i == 0)
    def _():
        issue(0, 0)

    @pl.when(i + 1 < n)
    def _():
        issue(i + 1, (i + 1) % 2)

    slot = i % 2
    for k in range(2):
        pltpu.make_async_copy(ys_hbm.at[pl.ds(0, tt), :], buf_ref.at[slot, k], sem.at[slot]).wait()

    lane = lax.broadcasted_iota(jnp.int32, route_ref.shape, 1)
    route = route_ref[...]
    w0 = jnp.sum(jnp.where(lane == ROUTE_W, route, 0.0), axis=1, keepdims=True)
    w1 = jnp.sum(jnp.where(lane == ROUTE_W + 1, route, 0.0), axis=1, keepdims=True)
    y0 = _unpack_rows(buf_ref[slot, 0], F32)
    y1 = _unpack_rows(buf_ref[slot, 1], F32)
    o_ref[...] = x1_ref[...] + (w0 * y0 + w1 * y1)


def _combine(dest, x1, route, ys, tt=256):
    T, D = x1.shape
    W = ys.shape[1]
    return pl.pallas_call(
        functools.partial(_combine_kernel, tt=tt),
        grid_spec=pltpu.PrefetchScalarGridSpec(
            num_scalar_prefetch=1,
            grid=(T // tt,),
            in_specs=[
                pl.BlockSpec((tt, D), lambda i, d: (i, 0)),
                pl.BlockSpec((tt, LANE), lambda i, d: (i, 0)),
                pl.BlockSpec(memory_space=pl.ANY),
            ],
            out_specs=pl.BlockSpec((tt, D), lambda i, d: (i, 0)),
            scratch_shapes=[pltpu.VMEM((2, 2, tt, W), ys.dtype), pltpu.SemaphoreType.DMA((2,))],
        ),
        out_shape=jax.ShapeDtypeStruct((T, D), F32),
        compiler_params=_params(("arbitrary",)),
        name="combine",
    )(dest, x1, route, ys)


def _moe(x1, hnp, route, counts8, wg, wu, wd, tm=256):
    T = x1.shape[0]
    E = wg.shape[0]
    counts = counts8[0, :E].astype(jnp.int32)
    eid = route[:, ROUTE_E:ROUTE_E + 2].astype(jnp.int32)
    rank = route[:, ROUTE_RANK:ROUTE_RANK + 2].astype(jnp.int32)
    start = jnp.cumsum(counts) - counts
    onehot = eid[..., None] == jnp.arange(E, dtype=jnp.int32)
    dest = (jnp.sum(jnp.where(onehot, start, 0), axis=-1) + rank).reshape(2 * T)
    xs = _dispatch(dest, hnp)
    ys = _experts(_work_items(counts, 2 * T, tm), xs, wg, wu, wd, tm)
    return _combine(dest, x1, route, ys)


def kernel(x, norm1_w, w_in, pool_w, pool_scale, conv_w, cmp_pe_k, cmp_w1_k, cmp_w2_k, cmp_pe_v, cmp_w1_v, cmp_w2_v, q_norm_w, k_norm_w, w_out, norm2_w, router_grp_w, router_grp_b, router_exp_w, router_exp_b, exp_w_gate, exp_w_up, exp_w_down):
    B, S, D = x.shape
    depth = w_in.shape[0]
    T = B * S
    n16 = S // CMP_STRIDE
    xf = x.reshape(T, D)

    def w1_pair(w1):
        return jnp.concatenate([w1[:CMP_STRIDE], w1[CMP_STRIDE:]], axis=-1).astype(BF16)

    def pe_rows(pe):
        return jnp.broadcast_to(pe.reshape(1, CMP_LEN * HEAD_DIM), (8, CMP_LEN * HEAD_DIM)).astype(BF16)

    for l in range(depth):
        w_in_p = jnp.pad(w_in[l], ((0, 0), (0, D_IN_PAD - D_IN))).astype(BF16)
        z = _inproj(xf, norm1_w[l].reshape(1, D), w_in_p)

        yab = _mix_ab(z, pool_w[l].astype(BF16), pool_scale[l].reshape(1, POOL_WIDTH), conv_w[l], B, S)

        xkv = z[:, COL_KC:COL_KS].reshape(B, n16, CMP_STRIDE * 512)
        kc, vc = _compress(
            xkv, w1_pair(cmp_w1_k[l]), w1_pair(cmp_w1_v[l]), pe_rows(cmp_pe_k[l]), pe_rows(cmp_pe_v[l]),
            cmp_w1_k[l].reshape(CMP_LEN * HEAD_DIM, CMP_HIDDEN).astype(BF16),
            cmp_w1_v[l].reshape(CMP_LEN * HEAD_DIM, CMP_HIDDEN).astype(BF16),
            cmp_w2_k[l].astype(BF16), cmp_w2_v[l].astype(BF16), k_norm_w[l])
        yc = _nsa(z, kc, vc, q_norm_w[l].reshape(1, HEAD_DIM), k_norm_w[l], B, S)

        wr = jnp.concatenate([router_exp_w[l], router_grp_w[l]], axis=1)
        wr = jnp.pad(wr, ((0, 0), (0, LANE - wr.shape[1])))
        wr_hi = wr.astype(BF16)
        wr_lo = (wr - wr_hi.astype(F32)).astype(BF16)
        br = jnp.concatenate([router_exp_b[l], router_grp_b[l]])
        br = jnp.pad(br, (0, LANE - br.shape[0])).reshape(1, LANE)
        x1, hnp, route, counts8 = _outproj_router(yab, yc, w_out[l].astype(BF16), xf,
                                                  norm2_w[l].reshape(1, D), wr_hi, wr_lo, br)

        xf = _moe(x1, hnp, route, counts8, exp_w_gate[l], exp_w_up[l], exp_w_down[l])
    return xf.reshape(B, S, D)
```

```python
import functools

import numpy as np
import jax
import jax.numpy as jnp
from jax import lax
from jax.experimental import pallas as pl
from jax.experimental.pallas import tpu as pltpu

F32 = jnp.float32
BF16 = jnp.bfloat16

POOL_WINDOWS = (2, 4, 8, 16)
LANE = 128
POOL_WIDTH = 512
CONV_WIDTH = 512
CONV_K = 3
NSA_WIDTH = 1024
HEAD_DIM = 128
NSA_KV_HEADS = 2
NSA_GROUP = 4
N_BRANCH = 3
CMP_LEN = 32
CMP_STRIDE = 16
CMP_HIDDEN = 256
SEL_LEN = 64
N_SELECT = 16
SEL_FORCE = 1.0e4
WINDOW = 512
N_GROUPS_MOE = 4
EXPERTS_PER_GROUP = 8
N_EXPERTS = 32
D_EXPERT = 256
EPS = 1e-6
NEG = -1e30

COL_U = 0
COL_B = 512
COL_C = 1024
COL_V = 1536
COL_Q = 2048
COL_KC = 3072
COL_KS = 3584
COL_VS = 3840
COL_KW = 4096
COL_VW = 4352
COL_GATE = 4608
D_IN = 4632
D_IN_PAD = 5120

VMEM_LIMIT = 56 * 1024 * 1024


def _params(sem):
    return pltpu.CompilerParams(dimension_semantics=sem, vmem_limit_bytes=VMEM_LIMIT)


def _rms(x, w):
    return x * lax.rsqrt(jnp.mean(x * x, axis=-1, keepdims=True) + EPS) * w


def _silu(x):
    return x / (1.0 + jnp.exp(-x))


def _dot(a, b):
    return jnp.dot(a, b, preferred_element_type=F32)


def _split3(a):
    hi = a.astype(BF16)
    r1 = a - hi.astype(F32)
    mid = r1.astype(BF16)
    lo = (r1 - mid.astype(F32)).astype(BF16)
    return hi, mid, lo


def _inproj_kernel(x_ref, nw_ref, w_ref, o_ref, xn_ref):
    @pl.when(pl.program_id(1) == 0)
    def _():
        xn_ref[...] = _rms(x_ref[...], nw_ref[...]).astype(BF16)

    o_ref[...] = _dot(xn_ref[...], w_ref[0]).astype(o_ref.dtype)


def _inproj(x2d, nw, w_all, layer, tm=1024, tn=1024):
    T, D = x2d.shape
    N = w_all.shape[2]
    return pl.pallas_call(
        _inproj_kernel,
        grid=(T // tm, N // tn),
        in_specs=[
            pl.BlockSpec((tm, D), lambda i, j: (i, 0)),
            pl.BlockSpec((1, D), lambda i, j: (0, 0)),
            pl.BlockSpec((1, D, tn), lambda i, j: (layer, 0, j)),
        ],
        out_specs=pl.BlockSpec((tm, tn), lambda i, j: (i, j)),
        out_shape=jax.ShapeDtypeStruct((T, N), BF16),
        scratch_shapes=[pltpu.VMEM((tm, D), BF16)],
        compiler_params=_params(("parallel", "arbitrary")),
        name="inproj",
    )(x2d, nw, w_all)


def _mix_ab_kernel(u_ref, b_ref, c_ref, v_ref, pw_ref, ps_ref, cw_ref, o_ref):
    S = u_ref.shape[0]
    row = lax.broadcasted_iota(jnp.int32, (S, LANE), 0)

    def shift(a, k):
        return jnp.where(row >= k, pltpu.roll(a, k, axis=0), 0.0)

    for g, w in enumerate(POOL_WINDOWS):
        sl = slice(g * LANE, (g + 1) * LANE)
        u = u_ref[:, sl].astype(F32)
        s = u
        k = 1
        while k < w:
            s = s + shift(s, k)
            k *= 2
        cnt = jnp.minimum(row + 1, w).astype(F32)
        mixed = (s / cnt - u).astype(BF16)
        y = _dot(mixed, pw_ref[g]) * ps_ref[:, sl]
        o_ref[:, sl] = y.astype(o_ref.dtype)

    for h in range(CONV_WIDTH // LANE):
        sl = slice(h * LANE, (h + 1) * LANE)
        u2 = c_ref[:, sl].astype(F32) * v_ref[:, sl].astype(F32)
        y = cw_ref[0:1, sl] * shift(u2, 2)
        y = y + cw_ref[1:2, sl] * shift(u2, 1)
        y = y + cw_ref[2:3, sl] * u2
        y = b_ref[:, sl].astype(F32) * y
        o_ref[:, POOL_WIDTH + h * LANE:POOL_WIDTH + (h + 1) * LANE] = y.astype(o_ref.dtype)


def _mix_ab(z, pool_w_bf16, pool_scale, conv_w, B, S):
    T = z.shape[0]
    blk = lambda c: pl.BlockSpec((S, 512), lambda b, c=c: (b, c))
    return pl.pallas_call(
        _mix_ab_kernel,
        grid=(B,),
        in_specs=[
            blk(COL_U // 512), blk(COL_B // 512), blk(COL_C // 512), blk(COL_V // 512),
            pl.BlockSpec((4, LANE, LANE), lambda b: (0, 0, 0)),
            pl.BlockSpec((1, POOL_WIDTH), lambda b: (0, 0)),
            pl.BlockSpec((CONV_K, CONV_WIDTH), lambda b: (0, 0)),
        ],
        out_specs=pl.BlockSpec((S, POOL_WIDTH + CONV_WIDTH), lambda b: (b, 0)),
        out_shape=jax.ShapeDtypeStruct((T, POOL_WIDTH + CONV_WIDTH), BF16),
        compiler_params=_params(("parallel",)),
        name="mix_ab",
    )(z, z, z, z, pool_w_bf16, pool_scale, conv_w)


def _compress_kernel(z_ref, w1k_ref, w1v_ref, pek_ref, pev_ref, w1kf_ref, w1vf_ref,
                     w2k_ref, w2v_ref, knw_ref, kc_ref, vc_ref, xf_ref):
    assert CMP_LEN == 2 * CMP_STRIDE
    n16 = z_ref.shape[0] // CMP_STRIDE
    for cg in range(z_ref.shape[1] // HEAD_DIM):
        xf_ref[cg] = z_ref[:, cg * HEAD_DIM:(cg + 1) * HEAD_DIM].astype(F32)
    streams = ((w1k_ref, pek_ref, w1kf_ref, w2k_ref, kc_ref),
               (w1v_ref, pev_ref, w1vf_ref, w2v_ref, vc_ref))
    for which, (w1_ref, pe_ref, w1f_ref, w2_ref, out_ref) in enumerate(streams):
        pe_term = _dot(pe_ref[...], w1f_ref[...])[0:1, :]
        for h in range(NSA_KV_HEADS):
            cg = which * NSA_KV_HEADS + h
            acc = None
            for l in range(CMP_STRIDE):
                rows = xf_ref[cg, pl.ds(l, n16, stride=CMP_STRIDE), :]
                part = _dot(rows.astype(BF16), w1_ref[l])
                acc = part if acc is None else acc + part
            first = acc[:, :CMP_HIDDEN]
            second = acc[:, CMP_HIDDEN:]
            hid = first + pltpu.roll(second, n16 - 1, axis=0) + pe_term
            out = _dot(_silu(hid).astype(BF16), w2_ref[...])
            if which == 0:
                out = _rms(out, knw_ref[0:1, :])
            out_ref[0, h] = out.astype(out_ref.dtype)


def _compress(z, w1k, w1v, pek8, pev8, w1kf, w1vf, w2k, w2v, knw, B, S):
    n16 = S // CMP_STRIDE
    full = lambda a: pl.BlockSpec(a.shape, lambda b, nd=a.ndim: (0,) * nd)
    out_sds = jax.ShapeDtypeStruct((B, NSA_KV_HEADS, n16, HEAD_DIM), BF16)
    out_spec = pl.BlockSpec((1, NSA_KV_HEADS, n16, HEAD_DIM), lambda b: (b, 0, 0, 0))
    return pl.pallas_call(
        _compress_kernel,
        grid=(B,),
        in_specs=[pl.BlockSpec((S, 512), lambda b: (b, COL_KC // 512)),
                  full(w1k), full(w1v), full(pek8), full(pev8), full(w1kf), full(w1vf),
                  full(w2k), full(w2v), full(knw)],
        out_specs=[out_spec, out_spec],
        out_shape=[out_sds, out_sds],
        scratch_shapes=[pltpu.VMEM((512 // HEAD_DIM, S, HEAD_DIM), F32)],
        compiler_params=_params(("parallel",)),
        name="compress",
    )(z, w1k, w1v, pek8, pev8, w1kf, w1vf, w2k, w2v, knw)


def _nsa_kernel(zq_ref, zg_ref, ks_ref, vs_ref, kw_ref, vw_ref, kc_ref, vc_ref,
                qnw_ref, knw_ref, ovlt_ref, exp_ref, o_ref,
                ksn_ref, kwn_ref, vst_ref, vwt_ref, vct_ref, gt_ref, m_ref, l_ref, acc_ref,
                ocmp_ref, owin_ref, sbuf_ref, bbuf_ref, *, tq):
    G = NSA_GROUP
    S = ks_ref.shape[0]
    hkv = pl.program_id(1)
    i = pl.program_id(2)
    tk = tq
    t0 = i * tq

    def transpose_to_bf16(a):
        return a.astype(F32).T.astype(BF16)

    @pl.when(i == 0)
    def _():
        ksn_ref[...] = _rms(ks_ref[...].astype(F32), knw_ref[1:2, :]).astype(BF16)
        kwn_ref[...] = _rms(kw_ref[...].astype(F32), knw_ref[2:3, :]).astype(BF16)
        for j in range(S // tk):
            vst_ref[j] = transpose_to_bf16(vs_ref[j * tk:(j + 1) * tk, :])
            vwt_ref[j] = transpose_to_bf16(vw_ref[j * tk:(j + 1) * tk, :])
        vct_ref[...] = transpose_to_bf16(vc_ref[0, 0])

    qts = []
    for g in range(G):
        qt = zq_ref[:, g * HEAD_DIM:(g + 1) * HEAD_DIM].astype(F32).T
        ms = jnp.mean(qt * qt, axis=0, keepdims=True)
        qt = qt * lax.rsqrt(ms + EPS) * qnw_ref[...] * (HEAD_DIM ** -0.5)
        qts.append(qt.astype(BF16))
    q4t = jnp.concatenate(qts, axis=1)
    per_head = tq // LANE
    n_ch = G * per_head
    cols = [slice(c * LANE, (c + 1) * LANE) for c in range(n_ch)]
    qcols = [slice((c % per_head) * LANE, (c % per_head + 1) * LANE) for c in range(n_ch)]

    sc = _dot(kc_ref[0, 0], q4t)
    n_sub = lax.broadcasted_iota(jnp.int32, (LANE, LANE), 0)
    pts = []
    psums = [None] * per_head
    for c in range(n_ch):
        t_lane = t0 + (c % per_head) * LANE + lax.broadcasted_iota(jnp.int32, (LANE, LANE), 1)
        cm = (n_sub * CMP_STRIDE + (CMP_LEN - 1)) <= t_lane
        sg = jnp.where(cm, sc[:, cols[c]], NEG)
        e = jnp.exp(sg - jnp.max(sg, axis=0, keepdims=True))
        p = e / jnp.sum(e, axis=0, keepdims=True)
        p = jnp.where(cm, p, 0.0)
        pts.append(p.astype(BF16))
        k = c % per_head
        psums[k] = p if psums[k] is None else psums[k] + p
    ocmp_ref[...] = _dot(vct_ref[...], jnp.concatenate(pts, axis=1))

    hi, mid, lo = _split3(jnp.concatenate(psums, axis=1))
    imp = _dot(ovlt_ref[...], hi) + _dot(ovlt_ref[...], mid) + _dot(ovlt_ref[...], lo)
    n_sel = S // SEL_LEN
    n_top = min(N_SELECT, n_sel)
    j_sub = lax.broadcasted_iota(jnp.int32, (n_sel, tq), 0)
    t_sel = t0 + lax.broadcasted_iota(jnp.int32, (n_sel, tq), 1)
    forced = (j_sub == (t_sel // SEL_LEN)) | (j_sub == 0)
    valid = (j_sub * SEL_LEN) <= t_sel
    score = jnp.where(forced, SEL_FORCE, jnp.where(valid, imp[0:n_sel, :], -1.0))
    rank = jnp.zeros((n_sel, tq), F32)
    for c in range(n_sel):
        other = score[c:c + 1, :]
        beats = jnp.where(other > score, 1.0, jnp.where((other == score) & (j_sub > c), 1.0, 0.0))
        rank = rank + beats
    sel = jnp.where(rank < n_top, 1.0, 0.0)
    selt = jnp.concatenate([sel, jnp.zeros((LANE - n_sel, tq), F32)], axis=0).astype(BF16)

    m_ref[...] = jnp.full(m_ref.shape, NEG, F32)
    l_ref[...] = jnp.zeros(l_ref.shape, F32)
    acc_ref[...] = jnp.zeros(acc_ref.shape, F32)
    k_sub = lax.broadcasted_iota(jnp.int32, (tk, LANE), 0)
    q_lane = lax.broadcasted_iota(jnp.int32, (tk, LANE), 1)

    def sel_scores(kt, slot):
        k0 = pl.multiple_of(kt * tk, tk)
        sbuf_ref[slot] = _dot(ksn_ref[pl.ds(k0, tk), :], q4t)
        bbuf_ref[slot] = (_dot(exp_ref[pl.ds(k0, tk), :], selt) - 1.0) * (-NEG)

    def sel_update(kt, slot, diagonal):
        vt = vst_ref[kt]
        scores = sbuf_ref.at[slot]
        biases = bbuf_ref.at[slot]
        for c in range(n_ch):
            bias = biases[:, qcols[c]]
            if diagonal:
                bias = jnp.where(k_sub <= q_lane + (c % per_head) * LANE, bias, NEG)
            sg = scores[:, cols[c]] + bias
            m_old = m_ref[:, cols[c]]
            m_new = jnp.maximum(m_old, jnp.max(sg, axis=0, keepdims=True))
            alpha = jnp.exp(m_old - m_new)
            p = jnp.exp(sg - m_new)
            l_ref[:, cols[c]] = alpha * l_ref[:, cols[c]] + jnp.sum(p, axis=0, keepdims=True)
            acc_ref[:, cols[c]] = alpha * acc_ref[:, cols[c]] + _dot(vt, p.astype(BF16))
            m_ref[:, cols[c]] = m_new

    def sel_body(kt, carry):
        sel_update(kt, kt % 2, False)
        sel_scores(kt + 1, (kt + 1) % 2)
        return carry

    sel_scores(0, 0)
    lax.fori_loop(0, i, sel_body, 0)
    sel_update(i, i % 2, True)

    wk = WINDOW + tq
    ws = pl.multiple_of(jnp.maximum(t0 - WINDOW, 0), tq)
    kpos = ws + lax.broadcasted_iota(jnp.int32, (wk, tq), 0)
    dist = (t0 + lax.broadcasted_iota(jnp.int32, (wk, tq), 1)) - kpos
    bias_w = jnp.where((dist >= 0) & (dist < WINDOW), 0.0, NEG)
    sw = _dot(kwn_ref[pl.ds(ws, wk), :], q4t)
    for c in range(n_ch):
        sg = sw[:, cols[c]] + bias_w[:, qcols[c]]
        p = jnp.exp(sg - jnp.max(sg, axis=0, keepdims=True))
        l_w = jnp.sum(p, axis=0, keepdims=True)
        pb = p.astype(BF16)
        acc_w = None
        for j in range(wk // tk):
            part = _dot(vwt_ref[ws // tk + j], pb[j * tk:(j + 1) * tk, :])
            acc_w = part if acc_w is None else acc_w + part
        owin_ref[:, cols[c]] = acc_w / l_w

    gt_ref[...] = (1.0 / (1.0 + jnp.exp(-zg_ref[...].astype(F32)))).T
    n_heads = NSA_KV_HEADS * G
    for c in range(n_ch):
        g = c // per_head
        col = hkv * G + g
        gate = lambda branch: gt_ref[pl.ds(branch * n_heads + col, 1), :][:, qcols[c]]
        out = (gate(0) * ocmp_ref[:, cols[c]]
               + gate(1) * (acc_ref[:, cols[c]] / l_ref[:, cols[c]])
               + gate(2) * owin_ref[:, cols[c]])
        o_ref[qcols[c], g * HEAD_DIM:(g + 1) * HEAD_DIM] = out.T.astype(o_ref.dtype)


def _nsa(z, kc, vc, qnw, knw, B, S, tq=256):
    T = z.shape[0]
    nq = S // tq
    G = NSA_GROUP
    n_c = (S - CMP_LEN) // CMP_STRIDE + 1
    n_sel = S // SEL_LEN
    assert S % tq == 0 and n_sel <= LANE and n_sel % 8 == 0 and n_c <= LANE and WINDOW % tq == 0
    assert S >= WINDOW + tq
    ci = np.arange(LANE)[None, :] * CMP_STRIDE
    sj = np.arange(LANE)[:, None] * SEL_LEN
    ovlt = ((ci < sj + SEL_LEN) & (ci + CMP_LEN > sj) & (np.arange(LANE)[None, :] < n_c)
            & (np.arange(LANE)[:, None] < n_sel))
    ovlt = jnp.asarray(ovlt.astype(np.float32), BF16)
    expand = (np.arange(S)[:, None] // SEL_LEN) == np.arange(LANE)[None, :]
    expand = jnp.asarray(expand.astype(np.float32), BF16)
    qnw_b = jnp.broadcast_to(qnw.reshape(HEAD_DIM, 1), (HEAD_DIM, tq))

    kvblk = lambda c: pl.BlockSpec((S, HEAD_DIM), lambda b, h, i, c=c: (b, c + h))
    cblk = pl.BlockSpec((1, 1, LANE, HEAD_DIM), lambda b, h, i: (b, h, 0, 0))
    full = lambda a: pl.BlockSpec(a.shape, lambda b, h, i, nd=a.ndim: (0,) * nd)
    return pl.pallas_call(
        functools.partial(_nsa_kernel, tq=tq),
        grid=(B, NSA_KV_HEADS, nq),
        in_specs=[
            pl.BlockSpec((tq, G * HEAD_DIM), lambda b, h, i: (b * nq + i, COL_Q // 512 + h)),
            pl.BlockSpec((tq, LANE), lambda b, h, i: (b * nq + i, COL_GATE // LANE)),
            kvblk(COL_KS // LANE), kvblk(COL_VS // LANE), kvblk(COL_KW // LANE), kvblk(COL_VW // LANE),
            cblk, cblk, full(qnw_b), full(knw), full(ovlt), full(expand),
        ],
        out_specs=pl.BlockSpec((tq, G * HEAD_DIM), lambda b, h, i: (b * nq + i, h)),
        out_shape=jax.ShapeDtypeStruct((T, NSA_WIDTH), BF16),
        scratch_shapes=[
            pltpu.VMEM((S, HEAD_DIM), BF16), pltpu.VMEM((S, HEAD_DIM), BF16),
            pltpu.VMEM((S // tq, HEAD_DIM, tq), BF16), pltpu.VMEM((S // tq, HEAD_DIM, tq), BF16),
            pltpu.VMEM((HEAD_DIM, LANE), BF16), pltpu.VMEM((LANE, tq), F32),
            pltpu.VMEM((1, G * tq), F32), pltpu.VMEM((1, G * tq), F32),
            pltpu.VMEM((HEAD_DIM, G * tq), F32), pltpu.VMEM((HEAD_DIM, G * tq), F32),
            pltpu.VMEM((HEAD_DIM, G * tq), F32),
            pltpu.VMEM((2, tq, G * tq), F32), pltpu.VMEM((2, tq, tq), F32),
        ],
        compiler_params=_params(("parallel", "parallel", "arbitrary")),
        name="nsa",
    )(z, z, z, z, z, z, kc, vc, qnw_b, knw, ovlt, expand)


def _pack_rows(x):
    w = x.shape[1] // 2
    return pltpu.pack_elementwise([x[:, :w], x[:, w:]], packed_dtype=BF16)


def _unpack_rows(p, dtype):
    lo = pltpu.unpack_elementwise(p, index=0, packed_dtype=BF16, unpacked_dtype=F32)
    hi = pltpu.unpack_elementwise(p, index=1, packed_dtype=BF16, unpacked_dtype=F32)
    return jnp.concatenate([lo.astype(dtype), hi.astype(dtype)], axis=1)


ROUTE_E = 0
ROUTE_W = 2
ROUTE_RANK = 4


def _outproj_router_kernel(yab_ref, yc_ref, wo_ref, x_ref, nw_ref, wrh_ref, wrl_ref, br_ref,
                           x1_ref, hnp_ref, route_ref, counts_ref, cnt_ref):
    @pl.when(pl.program_id(0) == 0)
    def _():
        cnt_ref[...] = jnp.zeros_like(cnt_ref)

    ka = yab_ref.shape[1]
    acc = _dot(yab_ref[...], wo_ref[0, 0:ka, :]) + _dot(yc_ref[...], wo_ref[0, ka:, :])
    x1 = x_ref[...] + acc
    x1_ref[...] = x1
    hn = _rms(x1, nw_ref[...])
    hnp_ref[...] = _pack_rows(hn)
    hi = hn.astype(BF16)
    lo = (hn - hi.astype(F32)).astype(BF16)
    lg = _dot(hi, wrh_ref[...]) + _dot(lo, wrh_ref[...]) + _dot(hi, wrl_ref[...]) + br_ref[...]

    tm = lg.shape[0]
    lane = lax.broadcasted_iota(jnp.int32, (tm, LANE), 1)
    lane_f = lane.astype(F32)
    big = float(LANE)

    def first_max(v):
        m = jnp.max(v, axis=1, keepdims=True)
        idx = jnp.min(jnp.where(v == m, lane_f, big), axis=1, keepdims=True)
        return m, idx

    is_grp = (lane >= N_EXPERTS) & (lane < N_EXPERTS + N_GROUPS_MOE)
    lgm = jnp.where(is_grp, lg, NEG)
    mg, grp_lane = first_max(lgm)
    p_grp = 1.0 / jnp.sum(jnp.where(is_grp, jnp.exp(lgm - mg), 0.0), axis=1, keepdims=True)
    grp = grp_lane - float(N_EXPERTS)
    in_grp = (lane < N_EXPERTS) & ((lane // EXPERTS_PER_GROUP).astype(F32) == grp)
    le = jnp.where(in_grp, lg, NEG)
    m1, i1 = first_max(le)
    le2 = jnp.where(lane_f == i1, NEG, le)
    m2, i2 = first_max(le2)
    e2 = jnp.exp(m2 - m1)
    den = 1.0 + e2
    w1 = p_grp * (1.0 / den)
    w2 = p_grp * (e2 / den)

    onehot = jnp.where((lane_f == i1) | (lane_f == i2), 1.0, 0.0)
    r_i = lax.broadcasted_iota(jnp.int32, (tm, tm), 0)
    c_i = lax.broadcasted_iota(jnp.int32, (tm, tm), 1)
    before = jnp.where(c_i < r_i, 1.0, 0.0).astype(BF16)
    base = cnt_ref[0:1, :] + _dot(before, onehot.astype(BF16))
    r1 = jnp.sum(jnp.where(lane_f == i1, base, 0.0), axis=1, keepdims=True)
    r2 = jnp.sum(jnp.where(lane_f == i2, base, 0.0), axis=1, keepdims=True)
    cnt_ref[0:1, :] = cnt_ref[0:1, :] + jnp.sum(onehot, axis=0, keepdims=True)
    counts_ref[...] = jnp.broadcast_to(cnt_ref[0:1, :], counts_ref.shape)

    route = jnp.zeros((tm, LANE), F32)
    for k, v in enumerate((i1, i2, w1, w2, r1, r2)):
        route = jnp.where(lane == k, v, route)
    route_ref[...] = route


def _outproj_router(yab, yc, wo_all, layer, x2d, nw, wr_hi, wr_lo, br, tm=512):
    T, D = x2d.shape
    full = lambda a: pl.BlockSpec(a.shape, lambda i, nd=a.ndim: (0,) * nd)
    return pl.pallas_call(
        _outproj_router_kernel,
        grid=(T // tm,),
        in_specs=[
            pl.BlockSpec((tm, yab.shape[1]), lambda i: (i, 0)),
            pl.BlockSpec((tm, yc.shape[1]), lambda i: (i, 0)),
            pl.BlockSpec((1,) + wo_all.shape[1:], lambda i: (layer, 0, 0)),
            pl.BlockSpec((tm, D), lambda i: (i, 0)),
            full(nw), full(wr_hi), full(wr_lo), full(br),
        ],
        out_specs=[pl.BlockSpec((tm, D), lambda i: (i, 0)),
                   pl.BlockSpec((tm, D // 2), lambda i: (i, 0)),
                   pl.BlockSpec((tm, LANE), lambda i: (i, 0)),
                   pl.BlockSpec((8, LANE), lambda i: (0, 0))],
        out_shape=[jax.ShapeDtypeStruct((T, D), F32),
                   jax.ShapeDtypeStruct((T, D // 2), jnp.uint32),
                   jax.ShapeDtypeStruct((T, LANE), F32),
                   jax.ShapeDtypeStruct((8, LANE), F32)],
        scratch_shapes=[pltpu.VMEM((8, LANE), F32)],
        compiler_params=_params(("arbitrary",)),
        name="outproj_router",
    )(yab, yc, wo_all, x2d, nw, wr_hi, wr_lo, br)


def _row_copy(src_ref, src_row, dst_ref, dst_row, sem):
    return pltpu.make_async_copy(src_ref.at[pl.ds(src_row, 1), :], dst_ref.at[pl.ds(dst_row, 1), :], sem)


def _dispatch_kernel(dest_ref, hnp_ref, xs_hbm, sem, *, chunk):
    i = pl.program_id(0)

    def body(j, carry):
        t = i * chunk + j
        _row_copy(hnp_ref, j, xs_hbm, dest_ref[2 * t], sem).start()
        _row_copy(hnp_ref, j, xs_hbm, dest_ref[2 * t + 1], sem).start()
        return carry

    lax.fori_loop(0, chunk, body, 0, unroll=8)
    for _ in range(2):
        pltpu.make_async_copy(hnp_ref, xs_hbm.at[pl.ds(0, chunk), :], sem).wait()


def _dispatch(dest, hnp, chunk=1024):
    T, W = hnp.shape
    return pl.pallas_call(
        functools.partial(_dispatch_kernel, chunk=chunk),
        grid_spec=pltpu.PrefetchScalarGridSpec(
            num_scalar_prefetch=1,
            grid=(T // chunk,),
            in_specs=[pl.BlockSpec((chunk, W), lambda i, d: (i, 0))],
            out_specs=pl.BlockSpec(memory_space=pl.ANY),
            scratch_shapes=[pltpu.SemaphoreType.DMA(())],
        ),
        out_shape=jax.ShapeDtypeStruct((2 * T, W), hnp.dtype),
        compiler_params=_params(("arbitrary",)),
        name="dispatch",
    )(dest, hnp)


def _experts_kernel(tile_ref, exp_ref, lo_ref, hi_ref, flag_ref, xs_ref, wg_ref, wu_ref, wd_ref,
                    ys_ref, acc_ref, wgb_ref, wub_ref, wdb_ref, *, tm):
    w = pl.program_id(0)
    lo = lo_ref[w]
    hi = hi_ref[w]
    flags = flag_ref[w]

    @pl.when((flags & 4) != 0)
    def _():
        wgb_ref[...] = wg_ref[0, 0].astype(BF16)
        wub_ref[...] = wu_ref[0, 0].astype(BF16)
        wdb_ref[...] = wd_ref[0, 0].astype(BF16)

    @pl.when((flags & 1) != 0)
    def _():
        acc_ref[...] = jnp.zeros_like(acc_ref)

    @pl.when(hi > lo)
    def _():
        x = _unpack_rows(xs_ref[...], BF16)
        hg = _dot(x, wgb_ref[...])
        hu = _dot(x, wub_ref[...])
        row = tile_ref[w] * tm + lax.broadcasted_iota(jnp.int32, hg.shape, 0)
        h = jnp.where((row >= lo) & (row < hi), _silu(hg) * hu, 0.0)
        acc_ref[...] += _dot(h.astype(BF16), wdb_ref[...])

    @pl.when((flags & 2) != 0)
    def _():
        ys_ref[...] = _pack_rows(acc_ref[...])


def _experts(meta, xs, wg_all, wu_all, wd_all, layer, tm):
    N, W = xs.shape
    _, E, D, F = wg_all.shape
    tile_w, exp_w, lo_w, hi_w, flag_w = meta
    n_work = tile_w.shape[0]
    wspec = lambda r, c: pl.BlockSpec((1, 1, r, c), lambda w, t, e, lo, hi, f: (layer, e[w], 0, 0))
    return pl.pallas_call(
        functools.partial(_experts_kernel, tm=tm),
        grid_spec=pltpu.PrefetchScalarGridSpec(
            num_scalar_prefetch=5,
            grid=(n_work,),
            in_specs=[
                pl.BlockSpec((tm, W), lambda w, t, e, lo, hi, f: (t[w], 0)),
                wspec(D, F), wspec(D, F), wspec(F, D),
            ],
            out_specs=pl.BlockSpec((tm, W), lambda w, t, e, lo, hi, f: (t[w], 0)),
            scratch_shapes=[pltpu.VMEM((tm, D), F32), pltpu.VMEM((D, F), BF16),
                            pltpu.VMEM((D, F), BF16), pltpu.VMEM((F, D), BF16)],
        ),
        out_shape=jax.ShapeDtypeStruct((N, W), xs.dtype),
        compiler_params=_params(("arbitrary",)),
        name="experts",
    )(tile_w, exp_w, lo_w, hi_w, flag_w, xs, wg_all, wu_all, wd_all)


def _work_items(counts, n_rows, tm):
    E = counts.shape[0]
    n_tiles = n_rows // tm
    n_work = n_tiles + E - 1
    start = jnp.cumsum(counts) - counts
    end = start + counts
    first_tile = start // tm
    last_tile = jnp.maximum(end - 1, 0) // tm
    n_e = jnp.where(counts > 0, last_tile - first_tile + 1, 0)
    wend = jnp.cumsum(n_e)
    wstart = wend - n_e
    total = wend[-1]
    w = jnp.arange(n_work, dtype=jnp.int32)
    wc = jnp.minimum(w, total - 1)
    ew = jnp.sum((wc[:, None] >= wend[None, :]).astype(jnp.int32), axis=1)
    tile_w = first_tile[ew] + (wc - wstart[ew])
    valid = w < total
    lo = jnp.where(valid, jnp.maximum(start[ew], tile_w * tm), 0)
    hi = jnp.where(valid, jnp.minimum(end[ew], (tile_w + 1) * tm), 0)
    prev_tile = jnp.concatenate([jnp.full((1,), -1, jnp.int32), tile_w[:-1]])
    next_tile = jnp.concatenate([tile_w[1:], jnp.full((1,), -1, jnp.int32)])
    prev_e = jnp.concatenate([jnp.full((1,), -1, jnp.int32), ew[:-1]])
    first = valid & (tile_w != prev_tile)
    last = valid & ((tile_w != next_tile) | (w == total - 1))
    new_e = ew != prev_e
    flags = first.astype(jnp.int32) + 2 * last.astype(jnp.int32) + 4 * new_e.astype(jnp.int32)
    i32 = lambda a: a.astype(jnp.int32)
    return i32(tile_w), i32(ew), i32(lo), i32(hi), i32(flags)


def _combine_kernel(dest_ref, x1_ref, route_ref, ys_hbm, o_ref, buf_ref, sem, *, tt):
    i = pl.program_id(0)
    n = pl.num_programs(0)

    def issue(step, slot):
        def body(j, carry):
            t = step * tt + j
            _row_copy(ys_hbm, dest_ref[2 * t], buf_ref.at[slot, 0], j, sem.at[slot]).start()
            _row_copy(ys_hbm, dest_ref[2 * t + 1], buf_ref.at[slot, 1], j, sem.at[slot]).start()
            return carry
        lax.fori_loop(0, tt, body, 0, unroll=8)

    @pl.when(i == 0)
    def _():
        issue(0, 0)

    @pl.when(i + 1 < n)
    def _():
        issue(i + 1, (i + 1) % 2)

    slot = i % 2
    for k in range(2):
        pltpu.make_async_copy(ys_hbm.at[pl.ds(0, tt), :], buf_ref.at[slot, k], sem.at[slot]).wait()

    lane = lax.broadcasted_iota(jnp.int32, route_ref.shape, 1)
    route = route_ref[...]
    w0 = jnp.sum(jnp.where(lane == ROUTE_W, route, 0.0), axis=1, keepdims=True)
    w1 = jnp.sum(jnp.where(lane == ROUTE_W + 1, route, 0.0), axis=1, keepdims=True)
    y0 = _unpack_rows(buf_ref[slot, 0], F32)
    y1 = _unpack_rows(buf_ref[slot, 1], F32)
    o_ref[...] = x1_ref[...] + (w0 * y0 + w1 * y1)


def _combine(dest, x1, route, ys, tt=256):
    T, D = x1.shape
    W = ys.shape[1]
    return pl.pallas_call(
        functools.partial(_combine_kernel, tt=tt),
        grid_spec=pltpu.PrefetchScalarGridSpec(
            num_scalar_prefetch=1,
            grid=(T // tt,),
            in_specs=[
                pl.BlockSpec((tt, D), lambda i, d: (i, 0)),
                pl.BlockSpec((tt, LANE), lambda i, d: (i, 0)),
                pl.BlockSpec(memory_space=pl.ANY),
            ],
            out_specs=pl.BlockSpec((tt, D), lambda i, d: (i, 0)),
            scratch_shapes=[pltpu.VMEM((2, 2, tt, W), ys.dtype), pltpu.SemaphoreType.DMA((2,))],
        ),
        out_shape=jax.ShapeDtypeStruct((T, D), F32),
        compiler_params=_params(("arbitrary",)),
        name="combine",
    )(dest, x1, route, ys)


def _moe(x1, hnp, route, counts8, wg_all, wu_all, wd_all, layer, tm=512):
    T = x1.shape[0]
    E = wg_all.shape[1]
    counts = counts8[0, :E].astype(jnp.int32)
    eid = route[:, ROUTE_E:ROUTE_E + 2].astype(jnp.int32)
    rank = route[:, ROUTE_RANK:ROUTE_RANK + 2].astype(jnp.int32)
    start = jnp.cumsum(counts) - counts
    onehot = eid[..., None] == jnp.arange(E, dtype=jnp.int32)
    dest = (jnp.sum(jnp.where(onehot, start, 0), axis=-1) + rank).reshape(2 * T)
    xs = _dispatch(dest, hnp)
    ys = _experts(_work_items(counts, 2 * T, tm), xs, wg_all, wu_all, wd_all, layer, tm)
    return _combine(dest, x1, route, ys)


def kernel(x, norm1_w, w_in, pool_w, pool_scale, conv_w, cmp_pe_k, cmp_w1_k, cmp_w2_k, cmp_pe_v, cmp_w1_v, cmp_w2_v, q_norm_w, k_norm_w, w_out, norm2_w, router_grp_w, router_grp_b, router_exp_w, router_exp_b, exp_w_gate, exp_w_up, exp_w_down):
    B, S, D = x.shape
    depth = w_in.shape[0]
    T = B * S
    xf = x.reshape(T, D)

    def w1_pair(w1):
        return jnp.concatenate([w1[:CMP_STRIDE], w1[CMP_STRIDE:]], axis=-1).astype(BF16)

    def pe_rows(pe):
        return jnp.broadcast_to(pe.reshape(1, CMP_LEN * HEAD_DIM), (8, CMP_LEN * HEAD_DIM)).astype(BF16)

    w_in_all = jnp.pad(w_in, ((0, 0), (0, 0), (0, D_IN_PAD - D_IN))).astype(BF16)
    w_out_all = w_out.astype(BF16)

    for l in range(depth):
        z = _inproj(xf, norm1_w[l].reshape(1, D), w_in_all, l)

        yab = _mix_ab(z, pool_w[l].astype(BF16), pool_scale[l].reshape(1, POOL_WIDTH), conv_w[l], B, S)

        kc, vc = _compress(
            z, w1_pair(cmp_w1_k[l]), w1_pair(cmp_w1_v[l]), pe_rows(cmp_pe_k[l]), pe_rows(cmp_pe_v[l]),
            cmp_w1_k[l].reshape(CMP_LEN * HEAD_DIM, CMP_HIDDEN).astype(BF16),
            cmp_w1_v[l].reshape(CMP_LEN * HEAD_DIM, CMP_HIDDEN).astype(BF16),
            cmp_w2_k[l].astype(BF16), cmp_w2_v[l].astype(BF16), k_norm_w[l], B, S)
        yc = _nsa(z, kc, vc, q_norm_w[l], k_norm_w[l], B, S)

        wr = jnp.concatenate([router_exp_w[l], router_grp_w[l]], axis=1)
        wr = jnp.pad(wr, ((0, 0), (0, LANE - wr.shape[1])))
        wr_hi = wr.astype(BF16)
        wr_lo = (wr - wr_hi.astype(F32)).astype(BF16)
        br = jnp.concatenate([router_exp_b[l], router_grp_b[l]])
        br = jnp.pad(br, (0, LANE - br.shape[0])).reshape(1, LANE)
        x1, hnp, route, counts8 = _outproj_router(yab, yc, w_out_all, l, xf, norm2_w[l].reshape(1, D),
                                                  wr_hi, wr_lo, br)

        xf = _moe(x1, hnp, route, counts8, exp_w_gate, exp_w_up, exp_w_down, l)
    return xf.reshape(B, S, D)
```

```python
import functools

import numpy as np
import jax
import jax.numpy as jnp
from jax import lax
from jax.experimental import pallas as pl
from jax.experimental.pallas import tpu as pltpu

F32 = jnp.float32
BF16 = jnp.bfloat16

POOL_WINDOWS = (2, 4, 8, 16)
LANE = 128
POOL_WIDTH = 512
CONV_WIDTH = 512
CONV_K = 3
NSA_WIDTH = 1024
HEAD_DIM = 128
NSA_KV_HEADS = 2
NSA_GROUP = 4
N_BRANCH = 3
CMP_LEN = 32
CMP_STRIDE = 16
CMP_HIDDEN = 256
SEL_LEN = 64
N_SELECT = 16
SEL_FORCE = 1.0e4
WINDOW = 512
N_GROUPS_MOE = 4
EXPERTS_PER_GROUP = 8
N_EXPERTS = 32
D_EXPERT = 256
EPS = 1e-6
NEG = -1e30

COL_U = 0
COL_B = 512
COL_C = 1024
COL_V = 1536
COL_Q = 2048
COL_KC = 3072
COL_KS = 3584
COL_VS = 3840
COL_KW = 4096
COL_VW = 4352
COL_GATE = 4608
D_IN = 4632

VMEM_LIMIT = 56 * 1024 * 1024


def _params(sem):
    return pltpu.CompilerParams(dimension_semantics=sem, vmem_limit_bytes=VMEM_LIMIT)


def _rms(x, w):
    return x * lax.rsqrt(jnp.mean(x * x, axis=-1, keepdims=True) + EPS) * w


def _silu(x):
    return x / (1.0 + jnp.exp(-x))


def _dot(a, b):
    return jnp.dot(a, b, preferred_element_type=F32)


def _split3(a):
    hi = a.astype(BF16)
    r1 = a - hi.astype(F32)
    mid = r1.astype(BF16)
    lo = (r1 - mid.astype(F32)).astype(BF16)
    return hi, mid, lo


def _inproj_kernel(x_ref, nw_ref, w_ref, wgate_ref, o_ref, gate_ref, xn_ref):
    @pl.when(pl.program_id(1) == 0)
    def _():
        xn = _rms(x_ref[...], nw_ref[...]).astype(BF16)
        xn_ref[...] = xn
        gate_ref[...] = _dot(xn, wgate_ref[0])

    o_ref[...] = _dot(xn_ref[...], w_ref[0]).astype(o_ref.dtype)


def _inproj(x2d, nw, w_all, wgate_all, layer, tm=1024, tn=1152):
    T, D = x2d.shape
    N = w_all.shape[2]
    assert N % tn == 0
    return pl.pallas_call(
        _inproj_kernel,
        grid=(T // tm, N // tn),
        in_specs=[
            pl.BlockSpec((tm, D), lambda i, j: (i, 0)),
            pl.BlockSpec((1, D), lambda i, j: (0, 0)),
            pl.BlockSpec((1, D, tn), lambda i, j: (layer, 0, j)),
            pl.BlockSpec((1, D, LANE), lambda i, j: (layer, 0, 0)),
        ],
        out_specs=[pl.BlockSpec((tm, tn), lambda i, j: (i, j)),
                   pl.BlockSpec((tm, LANE), lambda i, j: (i, 0))],
        out_shape=[jax.ShapeDtypeStruct((T, N), BF16), jax.ShapeDtypeStruct((T, LANE), F32)],
        scratch_shapes=[pltpu.VMEM((tm, D), BF16)],
        compiler_params=_params(("parallel", "arbitrary")),
        name="inproj",
    )(x2d, nw, w_all, wgate_all)


def _mix_ab_kernel(u_ref, b_ref, c_ref, v_ref, pw_ref, ps_ref, cw_ref, o_ref):
    S = u_ref.shape[0]
    row = lax.broadcasted_iota(jnp.int32, (S, LANE), 0)

    def shift(a, k):
        return jnp.where(row >= k, pltpu.roll(a, k, axis=0), 0.0)

    for g, w in enumerate(POOL_WINDOWS):
        sl = slice(g * LANE, (g + 1) * LANE)
        u = u_ref[:, sl].astype(F32)
        s = u
        k = 1
        while k < w:
            s = s + shift(s, k)
            k *= 2
        cnt = jnp.minimum(row + 1, w).astype(F32)
        mixed = (s / cnt - u).astype(BF16)
        y = _dot(mixed, pw_ref[g]) * ps_ref[:, sl]
        o_ref[:, sl] = y.astype(o_ref.dtype)

    for h in range(CONV_WIDTH // LANE):
        sl = slice(h * LANE, (h + 1) * LANE)
        u2 = c_ref[:, sl].astype(F32) * v_ref[:, sl].astype(F32)
        y = cw_ref[0:1, sl] * shift(u2, 2)
        y = y + cw_ref[1:2, sl] * shift(u2, 1)
        y = y + cw_ref[2:3, sl] * u2
        y = b_ref[:, sl].astype(F32) * y
        o_ref[:, POOL_WIDTH + h * LANE:POOL_WIDTH + (h + 1) * LANE] = y.astype(o_ref.dtype)


def _mix_ab(z, pool_w_bf16, pool_scale, conv_w, B, S):
    T = z.shape[0]
    blk = lambda c: pl.BlockSpec((S, 512), lambda b, c=c: (b, c))
    return pl.pallas_call(
        _mix_ab_kernel,
        grid=(B,),
        in_specs=[
            blk(COL_U // 512), blk(COL_B // 512), blk(COL_C // 512), blk(COL_V // 512),
            pl.BlockSpec((4, LANE, LANE), lambda b: (0, 0, 0)),
            pl.BlockSpec((1, POOL_WIDTH), lambda b: (0, 0)),
            pl.BlockSpec((CONV_K, CONV_WIDTH), lambda b: (0, 0)),
        ],
        out_specs=pl.BlockSpec((S, POOL_WIDTH + CONV_WIDTH), lambda b: (b, 0)),
        out_shape=jax.ShapeDtypeStruct((T, POOL_WIDTH + CONV_WIDTH), BF16),
        compiler_params=_params(("parallel",)),
        name="mix_ab",
    )(z, z, z, z, pool_w_bf16, pool_scale, conv_w)


def _compress_kernel(z_ref, w1k_ref, w1v_ref, pek_ref, pev_ref, w1kf_ref, w1vf_ref,
                     w2k_ref, w2v_ref, knw_ref, kc_ref, vc_ref, xf_ref):
    assert CMP_LEN == 2 * CMP_STRIDE
    n16 = z_ref.shape[0] // CMP_STRIDE
    for cg in range(z_ref.shape[1] // HEAD_DIM):
        xf_ref[cg] = z_ref[:, cg * HEAD_DIM:(cg + 1) * HEAD_DIM].astype(F32)
    streams = ((w1k_ref, pek_ref, w1kf_ref, w2k_ref, kc_ref),
               (w1v_ref, pev_ref, w1vf_ref, w2v_ref, vc_ref))
    for which, (w1_ref, pe_ref, w1f_ref, w2_ref, out_ref) in enumerate(streams):
        pe_term = _dot(pe_ref[...], w1f_ref[...])[0:1, :]
        for h in range(NSA_KV_HEADS):
            cg = which * NSA_KV_HEADS + h
            acc = None
            for l in range(CMP_STRIDE):
                rows = xf_ref[cg, pl.ds(l, n16, stride=CMP_STRIDE), :]
                part = _dot(rows.astype(BF16), w1_ref[l])
                acc = part if acc is None else acc + part
            first = acc[:, :CMP_HIDDEN]
            second = acc[:, CMP_HIDDEN:]
            hid = first + pltpu.roll(second, n16 - 1, axis=0) + pe_term
            out = _dot(_silu(hid).astype(BF16), w2_ref[...])
            if which == 0:
                out = _rms(out, knw_ref[0:1, :])
            out_ref[0, h] = out.astype(out_ref.dtype)


def _compress(z, w1k, w1v, pek8, pev8, w1kf, w1vf, w2k, w2v, knw, B, S):
    n16 = S // CMP_STRIDE
    full = lambda a: pl.BlockSpec(a.shape, lambda b, nd=a.ndim: (0,) * nd)
    out_sds = jax.ShapeDtypeStruct((B, NSA_KV_HEADS, n16, HEAD_DIM), BF16)
    out_spec = pl.BlockSpec((1, NSA_KV_HEADS, n16, HEAD_DIM), lambda b: (b, 0, 0, 0))
    return pl.pallas_call(
        _compress_kernel,
        grid=(B,),
        in_specs=[pl.BlockSpec((S, 512), lambda b: (b, COL_KC // 512)),
                  full(w1k), full(w1v), full(pek8), full(pev8), full(w1kf), full(w1vf),
                  full(w2k), full(w2v), full(knw)],
        out_specs=[out_spec, out_spec],
        out_shape=[out_sds, out_sds],
        scratch_shapes=[pltpu.VMEM((512 // HEAD_DIM, S, HEAD_DIM), F32)],
        compiler_params=_params(("parallel",)),
        name="compress",
    )(z, w1k, w1v, pek8, pev8, w1kf, w1vf, w2k, w2v, knw)


V_ROWS = HEAD_DIM + 16

def _nsa_kernel(zq_ref, zg_ref, ks_ref, vs_ref, kw_ref, vw_ref, kc_ref, vc_ref,
                qnw_ref, knw_ref, ovlt_ref, negexp_ref, o_ref,
                kaug_ref, kwn_ref, vst_ref, vwt_ref, vct_ref, gt_ref, m_ref, acc_ref,
                ocmp_ref, owin_ref, qaug_ref, sbuf_ref, *, tq):
    G = NSA_GROUP
    S = ks_ref.shape[0]
    hkv = pl.program_id(1)
    i = pl.program_id(2)
    tk = tq
    t0 = i * tq

    def transpose_to_bf16(a):
        return a.astype(F32).T.astype(BF16)

    @pl.when(i == 0)
    def _():
        kaug_ref[:, 0:HEAD_DIM] = _rms(ks_ref[...].astype(F32), knw_ref[1:2, :]).astype(BF16)
        kaug_ref[:, HEAD_DIM:] = negexp_ref[...]
        kwn_ref[...] = _rms(kw_ref[...].astype(F32), knw_ref[2:3, :]).astype(BF16)
        ones_row = jnp.where(lax.broadcasted_iota(jnp.int32, (V_ROWS - HEAD_DIM, tk), 0) == 0, 1.0, 0.0)
        for j in range(S // tk):
            vst_ref[j, 0:HEAD_DIM, :] = transpose_to_bf16(vs_ref[j * tk:(j + 1) * tk, :])
            vwt_ref[j, 0:HEAD_DIM, :] = transpose_to_bf16(vw_ref[j * tk:(j + 1) * tk, :])
            vst_ref[j, HEAD_DIM:, :] = ones_row.astype(BF16)
            vwt_ref[j, HEAD_DIM:, :] = ones_row.astype(BF16)
        vct_ref[...] = transpose_to_bf16(vc_ref[0, 0])

    qts = []
    for g in range(G):
        qt = zq_ref[:, g * HEAD_DIM:(g + 1) * HEAD_DIM].astype(F32).T
        ms = jnp.mean(qt * qt, axis=0, keepdims=True)
        qt = qt * lax.rsqrt(ms + EPS) * qnw_ref[...] * (HEAD_DIM ** -0.5)
        qts.append(qt.astype(BF16))
    q4t = jnp.concatenate(qts, axis=1)
    per_head = tq // LANE
    n_ch = G * per_head
    cols = [slice(c * LANE, (c + 1) * LANE) for c in range(n_ch)]
    qcols = [slice((c % per_head) * LANE, (c % per_head + 1) * LANE) for c in range(n_ch)]

    sc = _dot(kc_ref[0, 0], q4t)
    n_sub = lax.broadcasted_iota(jnp.int32, (LANE, LANE), 0)
    pts = []
    psums = [None] * per_head
    for c in range(n_ch):
        t_lane = t0 + (c % per_head) * LANE + lax.broadcasted_iota(jnp.int32, (LANE, LANE), 1)
        cm = (n_sub * CMP_STRIDE + (CMP_LEN - 1)) <= t_lane
        sg = jnp.where(cm, sc[:, cols[c]], NEG)
        e = jnp.exp(sg - jnp.max(sg, axis=0, keepdims=True))
        p = e / jnp.sum(e, axis=0, keepdims=True)
        p = jnp.where(cm, p, 0.0)
        pts.append(p.astype(BF16))
        k = c % per_head
        psums[k] = p if psums[k] is None else psums[k] + p
    ocmp_ref[...] = _dot(vct_ref[...], jnp.concatenate(pts, axis=1))

    hi, mid, lo = _split3(jnp.concatenate(psums, axis=1))
    imp = _dot(ovlt_ref[...], hi) + _dot(ovlt_ref[...], mid) + _dot(ovlt_ref[...], lo)
    n_sel = S // SEL_LEN
    n_top = min(N_SELECT, n_sel)
    j_sub = lax.broadcasted_iota(jnp.int32, (n_sel, tq), 0)
    t_sel = t0 + lax.broadcasted_iota(jnp.int32, (n_sel, tq), 1)
    forced = (j_sub == (t_sel // SEL_LEN)) | (j_sub == 0)
    valid = (j_sub * SEL_LEN) <= t_sel
    score = jnp.where(forced, SEL_FORCE, jnp.where(valid, imp[0:n_sel, :], -1.0))
    rank = jnp.zeros((n_sel, tq), F32)
    for c in range(n_sel):
        other = score[c:c + 1, :]
        beats = jnp.where(other > score, 1.0, jnp.where((other == score) & (j_sub > c), 1.0, 0.0))
        rank = rank + beats
    unsel = jnp.where(rank < n_top, 0.0, 1.0)
    unsel = jnp.concatenate([unsel, jnp.zeros((LANE - n_sel, tq), F32)], axis=0).astype(BF16)
    qaug_ref[0:HEAD_DIM, :] = q4t
    qaug_ref[HEAD_DIM:, :] = jnp.concatenate([unsel] * G, axis=1)

    m_ref[...] = jnp.full(m_ref.shape, NEG, F32)
    acc_ref[...] = jnp.zeros(acc_ref.shape, F32)
    k_sub = lax.broadcasted_iota(jnp.int32, (tk, LANE), 0)
    q_lane = lax.broadcasted_iota(jnp.int32, (tk, LANE), 1)

    def sel_scores(kt, slot):
        k0 = pl.multiple_of(kt * tk, tk)
        sbuf_ref[slot] = _dot(kaug_ref[pl.ds(k0, tk), :], qaug_ref[...])

    def sel_update(kt, slot, diagonal):
        vt = vst_ref[kt]
        scores = sbuf_ref.at[slot]
        for c in range(n_ch):
            sg = scores[:, cols[c]]
            if diagonal:
                sg = jnp.where(k_sub <= q_lane + (c % per_head) * LANE, sg, NEG)
            m_old = m_ref[:, cols[c]]
            m_new = jnp.maximum(m_old, jnp.max(sg, axis=0, keepdims=True))
            alpha = jnp.exp(m_old - m_new)
            p = jnp.exp(sg - m_new).astype(BF16)
            acc_ref[:, cols[c]] = alpha * acc_ref[:, cols[c]] + _dot(vt, p)
            m_ref[:, cols[c]] = m_new

    def sel_body(kt, carry):
        sel_update(kt, kt % 2, False)
        sel_scores(kt + 1, (kt + 1) % 2)
        return carry

    sel_scores(0, 0)
    lax.fori_loop(0, i, sel_body, 0)
    sel_update(i, i % 2, True)

    wk = WINDOW + tq
    ws = pl.multiple_of(jnp.maximum(t0 - WINDOW, 0), tq)
    kpos = ws + lax.broadcasted_iota(jnp.int32, (wk, LANE), 0)
    bias_w = []
    for k in range(per_head):
        dist = (t0 + k * LANE + lax.broadcasted_iota(jnp.int32, (wk, LANE), 1)) - kpos
        bias_w.append(jnp.where((dist >= 0) & (dist < WINDOW), 0.0, NEG))
    sw = _dot(kwn_ref[pl.ds(ws, wk), :], q4t)
    for c in range(n_ch):
        sg = sw[:, cols[c]] + bias_w[c % per_head]
        p = jnp.exp(sg - jnp.max(sg, axis=0, keepdims=True)).astype(BF16)
        acc_w = None
        for j in range(wk // tk):
            part = _dot(vwt_ref[ws // tk + j], p[j * tk:(j + 1) * tk, :])
            acc_w = part if acc_w is None else acc_w + part
        owin_ref[:, cols[c]] = acc_w[0:HEAD_DIM] / acc_w[HEAD_DIM:HEAD_DIM + 1]

    gt_ref[...] = (1.0 / (1.0 + jnp.exp(-zg_ref[...].astype(F32)))).T
    n_heads = NSA_KV_HEADS * G
    for c in range(n_ch):
        g = c // per_head
        col = hkv * G + g
        gate = lambda branch: gt_ref[pl.ds(branch * n_heads + col, 1), :][:, qcols[c]]
        o_sel = acc_ref[0:HEAD_DIM, cols[c]] / acc_ref[HEAD_DIM:HEAD_DIM + 1, cols[c]]
        out = gate(0) * ocmp_ref[:, cols[c]] + gate(1) * o_sel + gate(2) * owin_ref[:, cols[c]]
        o_ref[qcols[c], g * HEAD_DIM:(g + 1) * HEAD_DIM] = out.T.astype(o_ref.dtype)


def _nsa(z, zgate, kc, vc, qnw, knw, B, S, tq=256):
    T = z.shape[0]
    nq = S // tq
    G = NSA_GROUP
    n_c = (S - CMP_LEN) // CMP_STRIDE + 1
    n_sel = S // SEL_LEN
    assert S % tq == 0 and n_sel <= LANE and n_sel % 8 == 0 and n_c <= LANE and WINDOW % tq == 0
    assert S >= WINDOW + tq
    ci = np.arange(LANE)[None, :] * CMP_STRIDE
    sj = np.arange(LANE)[:, None] * SEL_LEN
    ovlt = ((ci < sj + SEL_LEN) & (ci + CMP_LEN > sj) & (np.arange(LANE)[None, :] < n_c)
            & (np.arange(LANE)[:, None] < n_sel))
    ovlt = jnp.asarray(ovlt.astype(np.float32), BF16)
    in_block = (np.arange(S)[:, None] // SEL_LEN) == np.arange(LANE)[None, :]
    negexp = jnp.asarray(in_block.astype(np.float32) * NEG, BF16)
    qnw_b = jnp.broadcast_to(qnw.reshape(HEAD_DIM, 1), (HEAD_DIM, tq))

    kvblk = lambda c: pl.BlockSpec((S, HEAD_DIM), lambda b, h, i, c=c: (b, c + h))
    cblk = pl.BlockSpec((1, 1, LANE, HEAD_DIM), lambda b, h, i: (b, h, 0, 0))
    full = lambda a: pl.BlockSpec(a.shape, lambda b, h, i, nd=a.ndim: (0,) * nd)
    return pl.pallas_call(
        functools.partial(_nsa_kernel, tq=tq),
        grid=(B, NSA_KV_HEADS, nq),
        in_specs=[
            pl.BlockSpec((tq, G * HEAD_DIM), lambda b, h, i: (b * nq + i, COL_Q // 512 + h)),
            pl.BlockSpec((tq, LANE), lambda b, h, i: (b * nq + i, 0)),
            kvblk(COL_KS // LANE), kvblk(COL_VS // LANE), kvblk(COL_KW // LANE), kvblk(COL_VW // LANE),
            cblk, cblk, full(qnw_b), full(knw), full(ovlt), full(negexp),
        ],
        out_specs=pl.BlockSpec((tq, G * HEAD_DIM), lambda b, h, i: (b * nq + i, h)),
        out_shape=jax.ShapeDtypeStruct((T, NSA_WIDTH), BF16),
        scratch_shapes=[
            pltpu.VMEM((S, 2 * HEAD_DIM), BF16), pltpu.VMEM((S, HEAD_DIM), BF16),
            pltpu.VMEM((S // tq, V_ROWS, tq), BF16), pltpu.VMEM((S // tq, V_ROWS, tq), BF16),
            pltpu.VMEM((HEAD_DIM, LANE), BF16), pltpu.VMEM((LANE, tq), F32),
            pltpu.VMEM((1, G * tq), F32), pltpu.VMEM((V_ROWS, G * tq), F32),
            pltpu.VMEM((HEAD_DIM, G * tq), F32), pltpu.VMEM((HEAD_DIM, G * tq), F32),
            pltpu.VMEM((2 * HEAD_DIM, G * tq), BF16), pltpu.VMEM((2, tq, G * tq), F32),
        ],
        compiler_params=_params(("parallel", "parallel", "arbitrary")),
        name="nsa",
    )(z, zgate, z, z, z, z, kc, vc, qnw_b, knw, ovlt, negexp)


def _pack_rows(x):
    w = x.shape[1] // 2
    return pltpu.pack_elementwise([x[:, :w], x[:, w:]], packed_dtype=BF16)


def _unpack_rows(p, dtype):
    lo = pltpu.unpack_elementwise(p, index=0, packed_dtype=BF16, unpacked_dtype=F32)
    hi = pltpu.unpack_elementwise(p, index=1, packed_dtype=BF16, unpacked_dtype=F32)
    return jnp.concatenate([lo.astype(dtype), hi.astype(dtype)], axis=1)


ROUTE_E = 0
ROUTE_W = 2
ROUTE_RANK = 4


def _outproj_router_kernel(yab_ref, yc_ref, wo_ref, x_ref, nw_ref, wrh_ref, wrl_ref, br_ref,
                           x1_ref, hnp_ref, route_ref, counts_ref, cnt_ref):
    @pl.when(pl.program_id(0) == 0)
    def _():
        cnt_ref[...] = jnp.zeros_like(cnt_ref)

    ka = yab_ref.shape[1]
    acc = _dot(yab_ref[...], wo_ref[0, 0:ka, :]) + _dot(yc_ref[...], wo_ref[0, ka:, :])
    x1 = x_ref[...] + acc
    x1_ref[...] = x1
    hn = _rms(x1, nw_ref[...])
    hnp_ref[...] = _pack_rows(hn)
    hi = hn.astype(BF16)
    lo = (hn - hi.astype(F32)).astype(BF16)
    lg = _dot(hi, wrh_ref[...]) + _dot(lo, wrh_ref[...]) + _dot(hi, wrl_ref[...]) + br_ref[...]

    tm = lg.shape[0]
    lane = lax.broadcasted_iota(jnp.int32, (tm, LANE), 1)
    lane_f = lane.astype(F32)
    big = float(LANE)

    def first_max(v):
        m = jnp.max(v, axis=1, keepdims=True)
        idx = jnp.min(jnp.where(v == m, lane_f, big), axis=1, keepdims=True)
        return m, idx

    is_grp = (lane >= N_EXPERTS) & (lane < N_EXPERTS + N_GROUPS_MOE)
    lgm = jnp.where(is_grp, lg, NEG)
    mg, grp_lane = first_max(lgm)
    p_grp = 1.0 / jnp.sum(jnp.where(is_grp, jnp.exp(lgm - mg), 0.0), axis=1, keepdims=True)
    grp = grp_lane - float(N_EXPERTS)
    in_grp = (lane < N_EXPERTS) & ((lane // EXPERTS_PER_GROUP).astype(F32) == grp)
    le = jnp.where(in_grp, lg, NEG)
    m1, i1 = first_max(le)
    le2 = jnp.where(lane_f == i1, NEG, le)
    m2, i2 = first_max(le2)
    e2 = jnp.exp(m2 - m1)
    den = 1.0 + e2
    w1 = p_grp * (1.0 / den)
    w2 = p_grp * (e2 / den)

    onehot = jnp.where((lane_f == i1) | (lane_f == i2), 1.0, 0.0)
    r_i = lax.broadcasted_iota(jnp.int32, (tm, tm), 0)
    c_i = lax.broadcasted_iota(jnp.int32, (tm, tm), 1)
    before = jnp.where(c_i < r_i, 1.0, 0.0).astype(BF16)
    base = cnt_ref[0:1, :] + _dot(before, onehot.astype(BF16))
    r1 = jnp.sum(jnp.where(lane_f == i1, base, 0.0), axis=1, keepdims=True)
    r2 = jnp.sum(jnp.where(lane_f == i2, base, 0.0), axis=1, keepdims=True)
    cnt_ref[0:1, :] = cnt_ref[0:1, :] + jnp.sum(onehot, axis=0, keepdims=True)
    counts_ref[...] = jnp.broadcast_to(cnt_ref[0:1, :], counts_ref.shape)

    route = jnp.zeros((tm, LANE), F32)
    for k, v in enumerate((i1, i2, w1, w2, r1, r2)):
        route = jnp.where(lane == k, v, route)
    route_ref[...] = route


def _outproj_router(yab, yc, wo_all, layer, x2d, nw, wr_hi, wr_lo, br, tm=512):
    T, D = x2d.shape
    full = lambda a: pl.BlockSpec(a.shape, lambda i, nd=a.ndim: (0,) * nd)
    return pl.pallas_call(
        _outproj_router_kernel,
        grid=(T // tm,),
        in_specs=[
            pl.BlockSpec((tm, yab.shape[1]), lambda i: (i, 0)),
            pl.BlockSpec((tm, yc.shape[1]), lambda i: (i, 0)),
            pl.BlockSpec((1,) + wo_all.shape[1:], lambda i: (layer, 0, 0)),
            pl.BlockSpec((tm, D), lambda i: (i, 0)),
            full(nw), full(wr_hi), full(wr_lo), full(br),
        ],
        out_specs=[pl.BlockSpec((tm, D), lambda i: (i, 0)),
                   pl.BlockSpec((tm, D // 2), lambda i: (i, 0)),
                   pl.BlockSpec((tm, LANE), lambda i: (i, 0)),
                   pl.BlockSpec((8, LANE), lambda i: (0, 0))],
        out_shape=[jax.ShapeDtypeStruct((T, D), F32),
                   jax.ShapeDtypeStruct((T, D // 2), jnp.uint32),
                   jax.ShapeDtypeStruct((T, LANE), F32),
                   jax.ShapeDtypeStruct((8, LANE), F32)],
        scratch_shapes=[pltpu.VMEM((8, LANE), F32)],
        compiler_params=_params(("arbitrary",)),
        name="outproj_router",
    )(yab, yc, wo_all, x2d, nw, wr_hi, wr_lo, br)


def _row_copy(src_ref, src_row, dst_ref, dst_row, sem):
    return pltpu.make_async_copy(src_ref.at[pl.ds(src_row, 1), :], dst_ref.at[pl.ds(dst_row, 1), :], sem)


def _dispatch_kernel(dest_ref, hnp_ref, xs_hbm, sem, *, chunk):
    i = pl.program_id(0)

    def body(j, carry):
        t = i * chunk + j
        _row_copy(hnp_ref, j, xs_hbm, dest_ref[2 * t], sem).start()
        _row_copy(hnp_ref, j, xs_hbm, dest_ref[2 * t + 1], sem).start()
        return carry

    lax.fori_loop(0, chunk, body, 0, unroll=8)
    for _ in range(2):
        pltpu.make_async_copy(hnp_ref, xs_hbm.at[pl.ds(0, chunk), :], sem).wait()


def _dispatch(dest, hnp, chunk=1024):
    T, W = hnp.shape
    return pl.pallas_call(
        functools.partial(_dispatch_kernel, chunk=chunk),
        grid_spec=pltpu.PrefetchScalarGridSpec(
            num_scalar_prefetch=1,
            grid=(T // chunk,),
            in_specs=[pl.BlockSpec((chunk, W), lambda i, d: (i, 0))],
            out_specs=pl.BlockSpec(memory_space=pl.ANY),
            scratch_shapes=[pltpu.SemaphoreType.DMA(())],
        ),
        out_shape=jax.ShapeDtypeStruct((2 * T, W), hnp.dtype),
        compiler_params=_params(("arbitrary",)),
        name="dispatch",
    )(dest, hnp)


def _experts_kernel(tile_ref, exp_ref, lo_ref, hi_ref, flag_ref, xs_ref, wg_ref, wu_ref, wd_ref,
                    ys_ref, acc_ref, wgb_ref, wub_ref, wdb_ref, *, tm):
    w = pl.program_id(0)
    lo = lo_ref[w]
    hi = hi_ref[w]
    flags = flag_ref[w]

    @pl.when((flags & 4) != 0)
    def _():
        wgb_ref[...] = wg_ref[0, 0].astype(BF16)
        wub_ref[...] = wu_ref[0, 0].astype(BF16)
        wdb_ref[...] = wd_ref[0, 0].astype(BF16)

    @pl.when((flags & 1) != 0)
    def _():
        acc_ref[...] = jnp.zeros_like(acc_ref)

    @pl.when(hi > lo)
    def _():
        x = _unpack_rows(xs_ref[...], BF16)
        hg = _dot(x, wgb_ref[...])
        hu = _dot(x, wub_ref[...])
        row = tile_ref[w] * tm + lax.broadcasted_iota(jnp.int32, hg.shape, 0)
        h = jnp.where((row >= lo) & (row < hi), _silu(hg) * hu, 0.0)
        acc_ref[...] += _dot(h.astype(BF16), wdb_ref[...])

    @pl.when((flags & 2) != 0)
    def _():
        ys_ref[...] = _pack_rows(acc_ref[...])


def _experts(meta, xs, wg_all, wu_all, wd_all, layer, tm):
    N, W = xs.shape
    _, E, D, F = wg_all.shape
    tile_w, exp_w, lo_w, hi_w, flag_w = meta
    n_work = tile_w.shape[0]
    wspec = lambda r, c: pl.BlockSpec((1, 1, r, c), lambda w, t, e, lo, hi, f: (layer, e[w], 0, 0))
    return pl.pallas_call(
        functools.partial(_experts_kernel, tm=tm),
        grid_spec=pltpu.PrefetchScalarGridSpec(
            num_scalar_prefetch=5,
            grid=(n_work,),
            in_specs=[
                pl.BlockSpec((tm, W), lambda w, t, e, lo, hi, f: (t[w], 0)),
                wspec(D, F), wspec(D, F), wspec(F, D),
            ],
            out_specs=pl.BlockSpec((tm, W), lambda w, t, e, lo, hi, f: (t[w], 0)),
            scratch_shapes=[pltpu.VMEM((tm, D), F32), pltpu.VMEM((D, F), BF16),
                            pltpu.VMEM((D, F), BF16), pltpu.VMEM((F, D), BF16)],
        ),
        out_shape=jax.ShapeDtypeStruct((N, W), xs.dtype),
        compiler_params=_params(("arbitrary",)),
        name="experts",
    )(tile_w, exp_w, lo_w, hi_w, flag_w, xs, wg_all, wu_all, wd_all)


def _work_items(counts, n_rows, tm):
    E = counts.shape[0]
    n_tiles = n_rows // tm
    n_work = n_tiles + E - 1
    start = jnp.cumsum(counts) - counts
    end = start + counts
    first_tile = start // tm
    last_tile = jnp.maximum(end - 1, 0) // tm
    n_e = jnp.where(counts > 0, last_tile - first_tile + 1, 0)
    wend = jnp.cumsum(n_e)
    wstart = wend - n_e
    total = wend[-1]
    w = jnp.arange(n_work, dtype=jnp.int32)
    wc = jnp.minimum(w, total - 1)
    ew = jnp.sum((wc[:, None] >= wend[None, :]).astype(jnp.int32), axis=1)
    tile_w = first_tile[ew] + (wc - wstart[ew])
    valid = w < total
    lo = jnp.where(valid, jnp.maximum(start[ew], tile_w * tm), 0)
    hi = jnp.where(valid, jnp.minimum(end[ew], (tile_w + 1) * tm), 0)
    prev_tile = jnp.concatenate([jnp.full((1,), -1, jnp.int32), tile_w[:-1]])
    next_tile = jnp.concatenate([tile_w[1:], jnp.full((1,), -1, jnp.int32)])
    prev_e = jnp.concatenate([jnp.full((1,), -1, jnp.int32), ew[:-1]])
    first = valid & (tile_w != prev_tile)
    last = valid & ((tile_w != next_tile) | (w == total - 1))
    new_e = ew != prev_e
    flags = first.astype(jnp.int32) + 2 * last.astype(jnp.int32) + 4 * new_e.astype(jnp.int32)
    i32 = lambda a: a.astype(jnp.int32)
    return i32(tile_w), i32(ew), i32(lo), i32(hi), i32(flags)


def _combine_kernel(dest_ref, x1_ref, route_ref, ys_hbm, o_ref, buf_ref, sem, *, tt):
    i = pl.program_id(0)
    n = pl.num_programs(0)

    def issue(step, slot):
        def body(j, carry):
            t = step * tt + j
            _row_copy(ys_hbm, dest_ref[2 * t], buf_ref.at[slot, 0], j, sem.at[slot]).start()
            _row_copy(ys_hbm, dest_ref[2 * t + 1], buf_ref.at[slot, 1], j, sem.at[slot]).start()
            return carry
        lax.fori_loop(0, tt, body, 0, unroll=8)

    @pl.when(i == 0)
    def _():
        issue(0, 0)

    @pl.when(i + 1 < n)
    def _():
        issue(i + 1, (i + 1) % 2)

    slot = i % 2
    for k in range(2):
        pltpu.make_async_copy(ys_hbm.at[pl.ds(0, tt), :], buf_ref.at[slot, k], sem.at[slot]).wait()

    lane = lax.broadcasted_iota(jnp.int32, route_ref.shape, 1)
    route = route_ref[...]
    w0 = jnp.sum(jnp.where(lane == ROUTE_W, route, 0.0), axis=1, keepdims=True)
    w1 = jnp.sum(jnp.where(lane == ROUTE_W + 1, route, 0.0), axis=1, keepdims=True)
    y0 = _unpack_rows(buf_ref[slot, 0], F32)
    y1 = _unpack_rows(buf_ref[slot, 1], F32)
    o_ref[...] = x1_ref[...] + (w0 * y0 + w1 * y1)


def _combine(dest, x1, route, ys, tt=256):
    T, D = x1.shape
    W = ys.shape[1]
    return pl.pallas_call(
        functools.partial(_combine_kernel, tt=tt),
        grid_spec=pltpu.PrefetchScalarGridSpec(
            num_scalar_prefetch=1,
            grid=(T // tt,),
            in_specs=[
                pl.BlockSpec((tt, D), lambda i, d: (i, 0)),
                pl.BlockSpec((tt, LANE), lambda i, d: (i, 0)),
                pl.BlockSpec(memory_space=pl.ANY),
            ],
            out_specs=pl.BlockSpec((tt, D), lambda i, d: (i, 0)),
            scratch_shapes=[pltpu.VMEM((2, 2, tt, W), ys.dtype), pltpu.SemaphoreType.DMA((2,))],
        ),
        out_shape=jax.ShapeDtypeStruct((T, D), F32),
        compiler_params=_params(("arbitrary",)),
        name="combine",
    )(dest, x1, route, ys)


def _moe(x1, hnp, route, counts8, wg_all, wu_all, wd_all, layer, tm=512):
    T = x1.shape[0]
    E = wg_all.shape[1]
    counts = counts8[0, :E].astype(jnp.int32)
    eid = route[:, ROUTE_E:ROUTE_E + 2].astype(jnp.int32)
    rank = route[:, ROUTE_RANK:ROUTE_RANK + 2].astype(jnp.int32)
    start = jnp.cumsum(counts) - counts
    onehot = eid[..., None] == jnp.arange(E, dtype=jnp.int32)
    dest = (jnp.sum(jnp.where(onehot, start, 0), axis=-1) + rank).reshape(2 * T)
    xs = _dispatch(dest, hnp)
    ys = _experts(_work_items(counts, 2 * T, tm), xs, wg_all, wu_all, wd_all, layer, tm)
    return _combine(dest, x1, route, ys)


def kernel(x, norm1_w, w_in, pool_w, pool_scale, conv_w, cmp_pe_k, cmp_w1_k, cmp_w2_k, cmp_pe_v, cmp_w1_v, cmp_w2_v, q_norm_w, k_norm_w, w_out, norm2_w, router_grp_w, router_grp_b, router_exp_w, router_exp_b, exp_w_gate, exp_w_up, exp_w_down):
    B, S, D = x.shape
    depth = w_in.shape[0]
    T = B * S
    xf = x.reshape(T, D)

    def w1_pair(w1):
        return jnp.concatenate([w1[:CMP_STRIDE], w1[CMP_STRIDE:]], axis=-1).astype(BF16)

    def pe_rows(pe):
        return jnp.broadcast_to(pe.reshape(1, CMP_LEN * HEAD_DIM), (8, CMP_LEN * HEAD_DIM)).astype(BF16)

    w_in_all = w_in[:, :, :COL_GATE].astype(BF16)
    w_gate_all = jnp.pad(w_in[:, :, COL_GATE:], ((0, 0), (0, 0), (0, LANE - (D_IN - COL_GATE)))).astype(BF16)
    w_out_all = w_out.astype(BF16)

    for l in range(depth):
        z, zgate = _inproj(xf, norm1_w[l].reshape(1, D), w_in_all, w_gate_all, l)

        yab = _mix_ab(z, pool_w[l].astype(BF16), pool_scale[l].reshape(1, POOL_WIDTH), conv_w[l], B, S)

        kc, vc = _compress(
            z, w1_pair(cmp_w1_k[l]), w1_pair(cmp_w1_v[l]), pe_rows(cmp_pe_k[l]), pe_rows(cmp_pe_v[l]),
            cmp_w1_k[l].reshape(CMP_LEN * HEAD_DIM, CMP_HIDDEN).astype(BF16),
            cmp_w1_v[l].reshape(CMP_LEN * HEAD_DIM, CMP_HIDDEN).astype(BF16),
            cmp_w2_k[l].astype(BF16), cmp_w2_v[l].astype(BF16), k_norm_w[l], B, S)
        yc = _nsa(z, zgate, kc, vc, q_norm_w[l], k_norm_w[l], B, S)

        wr = jnp.concatenate([router_exp_w[l], router_grp_w[l]], axis=1)
        wr = jnp.pad(wr, ((0, 0), (0, LANE - wr.shape[1])))
        wr_hi = wr.astype(BF16)
        wr_lo = (wr - wr_hi.astype(F32)).astype(BF16)
        br = jnp.concatenate([router_exp_b[l], router_grp_b[l]])
        br = jnp.pad(br, (0, LANE - br.shape[0])).reshape(1, LANE)
        x1, hnp, route, counts8 = _outproj_router(yab, yc, w_out_all, l, xf, norm2_w[l].reshape(1, D),
                                                  wr_hi, wr_lo, br)

        xf = _moe(x1, hnp, route, counts8, exp_w_gate, exp_w_up, exp_w_down, l)
    return xf.reshape(B, S, D)
```

```python
import functools

import numpy as np
import jax
import jax.numpy as jnp
from jax import lax
from jax.experimental import pallas as pl
from jax.experimental.pallas import tpu as pltpu

F32 = jnp.float32
BF16 = jnp.bfloat16

POOL_WINDOWS = (2, 4, 8, 16)
LANE = 128
POOL_WIDTH = 512
CONV_WIDTH = 512
CONV_K = 3
NSA_WIDTH = 1024
HEAD_DIM = 128
NSA_KV_HEADS = 2
NSA_GROUP = 4
N_BRANCH = 3
CMP_LEN = 32
CMP_STRIDE = 16
CMP_HIDDEN = 256
SEL_LEN = 64
N_SELECT = 16
SEL_FORCE = 1.0e4
WINDOW = 512
N_GROUPS_MOE = 4
EXPERTS_PER_GROUP = 8
N_EXPERTS = 32
D_EXPERT = 256
EPS = 1e-6
NEG = -1e30

COL_U = 0
COL_B = 512
COL_C = 1024
COL_V = 1536
COL_Q = 2048
COL_KC = 3072
COL_KS = 3584
COL_VS = 3840
COL_KW = 4096
COL_VW = 4352
COL_GATE = 4608
D_IN = 4632

VMEM_LIMIT = 56 * 1024 * 1024


def _params(sem):
    return pltpu.CompilerParams(dimension_semantics=sem, vmem_limit_bytes=VMEM_LIMIT)


def _rms(x, w):
    return x * lax.rsqrt(jnp.mean(x * x, axis=-1, keepdims=True) + EPS) * w


def _silu(x):
    return x / (1.0 + jnp.exp(-x))


def _dot(a, b):
    return jnp.dot(a, b, preferred_element_type=F32)


def _split3(a):
    hi = a.astype(BF16)
    r1 = a - hi.astype(F32)
    mid = r1.astype(BF16)
    lo = (r1 - mid.astype(F32)).astype(BF16)
    return hi, mid, lo


def _inproj_kernel(x_ref, nw_ref, w_ref, wgate_ref, o_ref, gate_ref, xn_ref):
    @pl.when(pl.program_id(1) == 0)
    def _():
        xn = _rms(x_ref[...], nw_ref[...]).astype(BF16)
        xn_ref[...] = xn
        gate_ref[...] = _dot(xn, wgate_ref[0])

    o_ref[...] = _dot(xn_ref[...], w_ref[0]).astype(o_ref.dtype)


def _inproj(x2d, nw, w_all, wgate_all, layer, tm=1024, tn=1536):
    T, D = x2d.shape
    N = COL_GATE
    assert N % tn == 0 and w_all.shape[2] >= N
    return pl.pallas_call(
        _inproj_kernel,
        grid=(T // tm, N // tn),
        in_specs=[
            pl.BlockSpec((tm, D), lambda i, j: (i, 0)),
            pl.BlockSpec((1, D), lambda i, j: (0, 0)),
            pl.BlockSpec((1, D, tn), lambda i, j: (layer, 0, j)),
            pl.BlockSpec((1, D, LANE), lambda i, j: (layer, 0, 0)),
        ],
        out_specs=[pl.BlockSpec((tm, tn), lambda i, j: (i, j)),
                   pl.BlockSpec((tm, LANE), lambda i, j: (i, 0))],
        out_shape=[jax.ShapeDtypeStruct((T, N), BF16), jax.ShapeDtypeStruct((T, LANE), F32)],
        scratch_shapes=[pltpu.VMEM((tm, D), BF16)],
        compiler_params=_params(("parallel", "arbitrary")),
        name="inproj",
    )(x2d, nw, w_all, wgate_all)


def _mix_ab_kernel(u_ref, b_ref, c_ref, v_ref, pw_ref, ps_ref, cw_ref, o_ref):
    S = u_ref.shape[0]
    row = lax.broadcasted_iota(jnp.int32, (S, LANE), 0)

    def shift(a, k):
        return jnp.where(row >= k, pltpu.roll(a, k, axis=0), 0.0)

    for g, w in enumerate(POOL_WINDOWS):
        sl = slice(g * LANE, (g + 1) * LANE)
        u = u_ref[:, sl].astype(F32)
        s = u
        k = 1
        while k < w:
            s = s + shift(s, k)
            k *= 2
        cnt = jnp.minimum(row + 1, w).astype(F32)
        mixed = (s / cnt - u).astype(BF16)
        y = _dot(mixed, pw_ref[g]) * ps_ref[:, sl]
        o_ref[:, sl] = y.astype(o_ref.dtype)

    for h in range(CONV_WIDTH // LANE):
        sl = slice(h * LANE, (h + 1) * LANE)
        u2 = c_ref[:, sl].astype(F32) * v_ref[:, sl].astype(F32)
        y = cw_ref[0:1, sl] * shift(u2, 2)
        y = y + cw_ref[1:2, sl] * shift(u2, 1)
        y = y + cw_ref[2:3, sl] * u2
        y = b_ref[:, sl].astype(F32) * y
        o_ref[:, POOL_WIDTH + h * LANE:POOL_WIDTH + (h + 1) * LANE] = y.astype(o_ref.dtype)


def _mix_ab(z, pool_w_bf16, pool_scale, conv_w, B, S):
    T = z.shape[0]
    blk = lambda c: pl.BlockSpec((S, 512), lambda b, c=c: (b, c))
    return pl.pallas_call(
        _mix_ab_kernel,
        grid=(B,),
        in_specs=[
            blk(COL_U // 512), blk(COL_B // 512), blk(COL_C // 512), blk(COL_V // 512),
            pl.BlockSpec((4, LANE, LANE), lambda b: (0, 0, 0)),
            pl.BlockSpec((1, POOL_WIDTH), lambda b: (0, 0)),
            pl.BlockSpec((CONV_K, CONV_WIDTH), lambda b: (0, 0)),
        ],
        out_specs=pl.BlockSpec((S, POOL_WIDTH + CONV_WIDTH), lambda b: (b, 0)),
        out_shape=jax.ShapeDtypeStruct((T, POOL_WIDTH + CONV_WIDTH), BF16),
        compiler_params=_params(("parallel",)),
        name="mix_ab",
    )(z, z, z, z, pool_w_bf16, pool_scale, conv_w)


def _compress_kernel(z_ref, w1k_ref, w1v_ref, pek_ref, pev_ref, w1kf_ref, w1vf_ref,
                     w2k_ref, w2v_ref, knw_ref, kc_ref, vc_ref, xf_ref):
    assert CMP_LEN == 2 * CMP_STRIDE
    n16 = z_ref.shape[0] // CMP_STRIDE
    for cg in range(z_ref.shape[1] // HEAD_DIM):
        xf_ref[cg] = z_ref[:, cg * HEAD_DIM:(cg + 1) * HEAD_DIM].astype(F32)
    streams = ((w1k_ref, pek_ref, w1kf_ref, w2k_ref, kc_ref),
               (w1v_ref, pev_ref, w1vf_ref, w2v_ref, vc_ref))
    for which, (w1_ref, pe_ref, w1f_ref, w2_ref, out_ref) in enumerate(streams):
        pe_term = _dot(pe_ref[...], w1f_ref[...])[0:1, :]
        for h in range(NSA_KV_HEADS):
            cg = which * NSA_KV_HEADS + h
            acc = None
            for l in range(CMP_STRIDE):
                rows = xf_ref[cg, pl.ds(l, n16, stride=CMP_STRIDE), :]
                part = _dot(rows.astype(BF16), w1_ref[l])
                acc = part if acc is None else acc + part
            first = acc[:, :CMP_HIDDEN]
            second = acc[:, CMP_HIDDEN:]
            hid = first + pltpu.roll(second, n16 - 1, axis=0) + pe_term
            out = _dot(_silu(hid).astype(BF16), w2_ref[...])
            if which == 0:
                out = _rms(out, knw_ref[0:1, :])
            out_ref[0, h] = out.astype(out_ref.dtype)


def _compress(z, w1k, w1v, pek8, pev8, w1kf, w1vf, w2k, w2v, knw, B, S):
    n16 = S // CMP_STRIDE
    full = lambda a: pl.BlockSpec(a.shape, lambda b, nd=a.ndim: (0,) * nd)
    out_sds = jax.ShapeDtypeStruct((B, NSA_KV_HEADS, n16, HEAD_DIM), BF16)
    out_spec = pl.BlockSpec((1, NSA_KV_HEADS, n16, HEAD_DIM), lambda b: (b, 0, 0, 0))
    return pl.pallas_call(
        _compress_kernel,
        grid=(B,),
        in_specs=[pl.BlockSpec((S, 512), lambda b: (b, COL_KC // 512)),
                  full(w1k), full(w1v), full(pek8), full(pev8), full(w1kf), full(w1vf),
                  full(w2k), full(w2v), full(knw)],
        out_specs=[out_spec, out_spec],
        out_shape=[out_sds, out_sds],
        scratch_shapes=[pltpu.VMEM((512 // HEAD_DIM, S, HEAD_DIM), F32)],
        compiler_params=_params(("parallel",)),
        name="compress",
    )(z, w1k, w1v, pek8, pev8, w1kf, w1vf, w2k, w2v, knw)


V_ROWS = HEAD_DIM + 16

def _nsa_kernel(zq_ref, zg_ref, ks_ref, vs_ref, kw_ref, vw_ref, kc_ref, vc_ref,
                qnw_ref, knw_ref, ovlt_ref, negexp_ref, o_ref,
                kaug_ref, kwn_ref, vst_ref, vwt_ref, vct_ref, gt_ref, m_ref, acc_ref,
                ocmp_ref, owin_ref, qaug_ref, sbuf_ref, *, tq):
    G = NSA_GROUP
    S = ks_ref.shape[0]
    hkv = pl.program_id(1)
    i = pl.program_id(2)
    tk = tq
    t0 = i * tq

    def transpose_to_bf16(a):
        return a.astype(F32).T.astype(BF16)

    @pl.when(i == 0)
    def _():
        kaug_ref[:, 0:HEAD_DIM] = _rms(ks_ref[...].astype(F32), knw_ref[1:2, :]).astype(BF16)
        kaug_ref[:, HEAD_DIM:] = negexp_ref[...]
        kwn_ref[...] = _rms(kw_ref[...].astype(F32), knw_ref[2:3, :]).astype(BF16)
        ones_row = jnp.where(lax.broadcasted_iota(jnp.int32, (V_ROWS - HEAD_DIM, tk), 0) == 0, 1.0, 0.0)
        for j in range(S // tk):
            vst_ref[j, 0:HEAD_DIM, :] = transpose_to_bf16(vs_ref[j * tk:(j + 1) * tk, :])
            vwt_ref[j, 0:HEAD_DIM, :] = transpose_to_bf16(vw_ref[j * tk:(j + 1) * tk, :])
            vst_ref[j, HEAD_DIM:, :] = ones_row.astype(BF16)
            vwt_ref[j, HEAD_DIM:, :] = ones_row.astype(BF16)
        vct_ref[...] = transpose_to_bf16(vc_ref[0, 0])

    qts = []
    for g in range(G):
        qt = zq_ref[:, g * HEAD_DIM:(g + 1) * HEAD_DIM].astype(F32).T
        ms = jnp.mean(qt * qt, axis=0, keepdims=True)
        qt = qt * lax.rsqrt(ms + EPS) * qnw_ref[...] * (HEAD_DIM ** -0.5)
        qts.append(qt.astype(BF16))
    q4t = jnp.concatenate(qts, axis=1)
    per_head = tq // LANE
    n_ch = G * per_head
    cols = [slice(c * LANE, (c + 1) * LANE) for c in range(n_ch)]
    qcols = [slice((c % per_head) * LANE, (c % per_head + 1) * LANE) for c in range(n_ch)]

    sc = _dot(kc_ref[0, 0], q4t)
    n_sub = lax.broadcasted_iota(jnp.int32, (LANE, LANE), 0)
    pts = []
    psums = [None] * per_head
    for c in range(n_ch):
        t_lane = t0 + (c % per_head) * LANE + lax.broadcasted_iota(jnp.int32, (LANE, LANE), 1)
        cm = (n_sub * CMP_STRIDE + (CMP_LEN - 1)) <= t_lane
        sg = jnp.where(cm, sc[:, cols[c]], NEG)
        e = jnp.exp(sg - jnp.max(sg, axis=0, keepdims=True))
        p = e / jnp.sum(e, axis=0, keepdims=True)
        p = jnp.where(cm, p, 0.0)
        pts.append(p.astype(BF16))
        k = c % per_head
        psums[k] = p if psums[k] is None else psums[k] + p
    ocmp_ref[...] = _dot(vct_ref[...], jnp.concatenate(pts, axis=1))

    hi, mid, lo = _split3(jnp.concatenate(psums, axis=1))
    imp = _dot(ovlt_ref[...], hi) + _dot(ovlt_ref[...], mid) + _dot(ovlt_ref[...], lo)
    n_sel = S // SEL_LEN
    n_top = min(N_SELECT, n_sel)
    j_sub = lax.broadcasted_iota(jnp.int32, (n_sel, tq), 0)
    t_sel = t0 + lax.broadcasted_iota(jnp.int32, (n_sel, tq), 1)
    forced = (j_sub == (t_sel // SEL_LEN)) | (j_sub == 0)
    valid = (j_sub * SEL_LEN) <= t_sel
    score = jnp.where(forced, SEL_FORCE, jnp.where(valid, imp[0:n_sel, :], -1.0))
    rank = jnp.zeros((n_sel, tq), F32)
    for c in range(n_sel):
        other = score[c:c + 1, :]
        beats = jnp.where(other > score, 1.0, jnp.where((other == score) & (j_sub > c), 1.0, 0.0))
        rank = rank + beats
    unsel = jnp.where(rank < n_top, 0.0, 1.0)
    unsel = jnp.concatenate([unsel, jnp.zeros((LANE - n_sel, tq), F32)], axis=0).astype(BF16)
    qaug_ref[0:HEAD_DIM, :] = q4t
    qaug_ref[HEAD_DIM:, :] = jnp.concatenate([unsel] * G, axis=1)

    m_ref[...] = jnp.full(m_ref.shape, NEG, F32)
    acc_ref[...] = jnp.zeros(acc_ref.shape, F32)
    k_sub = lax.broadcasted_iota(jnp.int32, (tk, LANE), 0)
    q_lane = lax.broadcasted_iota(jnp.int32, (tk, LANE), 1)

    def sel_scores(kt, slot):
        k0 = pl.multiple_of(kt * tk, tk)
        sbuf_ref[slot] = _dot(kaug_ref[pl.ds(k0, tk), :], qaug_ref[...])

    def sel_update(kt, slot, diagonal):
        vt = vst_ref[kt]
        scores = sbuf_ref.at[slot]
        for c in range(n_ch):
            sg = scores[:, cols[c]]
            if diagonal:
                sg = jnp.where(k_sub <= q_lane + (c % per_head) * LANE, sg, NEG)
            m_old = m_ref[:, cols[c]]
            m_new = jnp.maximum(m_old, jnp.max(sg, axis=0, keepdims=True))
            alpha = jnp.exp(m_old - m_new)
            p = jnp.exp(sg - m_new).astype(BF16)
            acc_ref[:, cols[c]] = alpha * acc_ref[:, cols[c]] + _dot(vt, p)
            m_ref[:, cols[c]] = m_new

    def sel_body(kt, carry):
        sel_update(kt, kt % 2, False)
        sel_scores(kt + 1, (kt + 1) % 2)
        return carry

    sel_scores(0, 0)
    lax.fori_loop(0, i, sel_body, 0)
    sel_update(i, i % 2, True)

    wk = WINDOW + tq
    ws = pl.multiple_of(jnp.maximum(t0 - WINDOW, 0), tq)
    kpos = ws + lax.broadcasted_iota(jnp.int32, (wk, LANE), 0)
    bias_w = []
    for k in range(per_head):
        dist = (t0 + k * LANE + lax.broadcasted_iota(jnp.int32, (wk, LANE), 1)) - kpos
        bias_w.append(jnp.where((dist >= 0) & (dist < WINDOW), 0.0, NEG))
    sw = _dot(kwn_ref[pl.ds(ws, wk), :], q4t)
    for c in range(n_ch):
        sg = sw[:, cols[c]] + bias_w[c % per_head]
        p = jnp.exp(sg - jnp.max(sg, axis=0, keepdims=True)).astype(BF16)
        acc_w = None
        for j in range(wk // tk):
            part = _dot(vwt_ref[ws // tk + j], p[j * tk:(j + 1) * tk, :])
            acc_w = part if acc_w is None else acc_w + part
        owin_ref[:, cols[c]] = acc_w[0:HEAD_DIM] / acc_w[HEAD_DIM:HEAD_DIM + 1]

    gt_ref[...] = (1.0 / (1.0 + jnp.exp(-zg_ref[...].astype(F32)))).T
    n_heads = NSA_KV_HEADS * G
    for c in range(n_ch):
        g = c // per_head
        col = hkv * G + g
        gate = lambda branch: gt_ref[pl.ds(branch * n_heads + col, 1), :][:, qcols[c]]
        o_sel = acc_ref[0:HEAD_DIM, cols[c]] / acc_ref[HEAD_DIM:HEAD_DIM + 1, cols[c]]
        out = gate(0) * ocmp_ref[:, cols[c]] + gate(1) * o_sel + gate(2) * owin_ref[:, cols[c]]
        o_ref[qcols[c], g * HEAD_DIM:(g + 1) * HEAD_DIM] = out.T.astype(o_ref.dtype)


def _nsa(z, zgate, kc, vc, qnw, knw, B, S, tq=256):
    T = z.shape[0]
    nq = S // tq
    G = NSA_GROUP
    n_c = (S - CMP_LEN) // CMP_STRIDE + 1
    n_sel = S // SEL_LEN
    assert S % tq == 0 and n_sel <= LANE and n_sel % 8 == 0 and n_c <= LANE and WINDOW % tq == 0
    assert S >= WINDOW + tq
    ci = np.arange(LANE)[None, :] * CMP_STRIDE
    sj = np.arange(LANE)[:, None] * SEL_LEN
    ovlt = ((ci < sj + SEL_LEN) & (ci + CMP_LEN > sj) & (np.arange(LANE)[None, :] < n_c)
            & (np.arange(LANE)[:, None] < n_sel))
    ovlt = jnp.asarray(ovlt.astype(np.float32), BF16)
    in_block = (np.arange(S)[:, None] // SEL_LEN) == np.arange(LANE)[None, :]
    negexp = jnp.asarray(in_block.astype(np.float32) * NEG, BF16)
    qnw_b = jnp.broadcast_to(qnw.reshape(HEAD_DIM, 1), (HEAD_DIM, tq))

    kvblk = lambda c: pl.BlockSpec((S, HEAD_DIM), lambda b, h, i, c=c: (b, c + h))
    cblk = pl.BlockSpec((1, 1, LANE, HEAD_DIM), lambda b, h, i: (b, h, 0, 0))
    full = lambda a: pl.BlockSpec(a.shape, lambda b, h, i, nd=a.ndim: (0,) * nd)
    return pl.pallas_call(
        functools.partial(_nsa_kernel, tq=tq),
        grid=(B, NSA_KV_HEADS, nq),
        in_specs=[
            pl.BlockSpec((tq, G * HEAD_DIM), lambda b, h, i: (b * nq + i, COL_Q // 512 + h)),
            pl.BlockSpec((tq, LANE), lambda b, h, i: (b * nq + i, 0)),
            kvblk(COL_KS // LANE), kvblk(COL_VS // LANE), kvblk(COL_KW // LANE), kvblk(COL_VW // LANE),
            cblk, cblk, full(qnw_b), full(knw), full(ovlt), full(negexp),
        ],
        out_specs=pl.BlockSpec((tq, G * HEAD_DIM), lambda b, h, i: (b * nq + i, h)),
        out_shape=jax.ShapeDtypeStruct((T, NSA_WIDTH), BF16),
        scratch_shapes=[
            pltpu.VMEM((S, 2 * HEAD_DIM), BF16), pltpu.VMEM((S, HEAD_DIM), BF16),
            pltpu.VMEM((S // tq, V_ROWS, tq), BF16), pltpu.VMEM((S // tq, V_ROWS, tq), BF16),
            pltpu.VMEM((HEAD_DIM, LANE), BF16), pltpu.VMEM((LANE, tq), F32),
            pltpu.VMEM((1, G * tq), F32), pltpu.VMEM((V_ROWS, G * tq), F32),
            pltpu.VMEM((HEAD_DIM, G * tq), F32), pltpu.VMEM((HEAD_DIM, G * tq), F32),
            pltpu.VMEM((2 * HEAD_DIM, G * tq), BF16), pltpu.VMEM((2, tq, G * tq), F32),
        ],
        compiler_params=_params(("parallel", "parallel", "arbitrary")),
        name="nsa",
    )(z, zgate, z, z, z, z, kc, vc, qnw_b, knw, ovlt, negexp)


def _pack_rows(x):
    w = x.shape[1] // 2
    return pltpu.pack_elementwise([x[:, :w], x[:, w:]], packed_dtype=BF16)


def _unpack_rows(p, dtype):
    lo = pltpu.unpack_elementwise(p, index=0, packed_dtype=BF16, unpacked_dtype=F32)
    hi = pltpu.unpack_elementwise(p, index=1, packed_dtype=BF16, unpacked_dtype=F32)
    return jnp.concatenate([lo.astype(dtype), hi.astype(dtype)], axis=1)


ROUTE_E = 0
ROUTE_W = 2
ROUTE_RANK = 4


def _outproj_router_kernel(yab_ref, yc_ref, wo_ref, x_ref, nw_ref, wrh_ref, wrl_ref, br_ref,
                           x1_ref, hnp_ref, route_ref, counts_ref, cnt_ref):
    @pl.when(pl.program_id(0) == 0)
    def _():
        cnt_ref[...] = jnp.zeros_like(cnt_ref)

    ka = yab_ref.shape[1]
    acc = _dot(yab_ref[...], wo_ref[0, 0:ka, :]) + _dot(yc_ref[...], wo_ref[0, ka:, :])
    x1 = x_ref[...] + acc
    x1_ref[...] = x1
    hn = _rms(x1, nw_ref[...])
    hnp_ref[...] = _pack_rows(hn)
    hi = hn.astype(BF16)
    lo = (hn - hi.astype(F32)).astype(BF16)
    lg = _dot(hi, wrh_ref[...]) + _dot(lo, wrh_ref[...]) + _dot(hi, wrl_ref[...]) + br_ref[...]

    tm = lg.shape[0]
    lane = lax.broadcasted_iota(jnp.int32, (tm, LANE), 1)
    lane_f = lane.astype(F32)
    big = float(LANE)

    def first_max(v):
        m = jnp.max(v, axis=1, keepdims=True)
        idx = jnp.min(jnp.where(v == m, lane_f, big), axis=1, keepdims=True)
        return m, idx

    is_grp = (lane >= N_EXPERTS) & (lane < N_EXPERTS + N_GROUPS_MOE)
    lgm = jnp.where(is_grp, lg, NEG)
    mg, grp_lane = first_max(lgm)
    p_grp = 1.0 / jnp.sum(jnp.where(is_grp, jnp.exp(lgm - mg), 0.0), axis=1, keepdims=True)
    grp = grp_lane - float(N_EXPERTS)
    in_grp = (lane < N_EXPERTS) & ((lane // EXPERTS_PER_GROUP).astype(F32) == grp)
    le = jnp.where(in_grp, lg, NEG)
    m1, i1 = first_max(le)
    le2 = jnp.where(lane_f == i1, NEG, le)
    m2, i2 = first_max(le2)
    e2 = jnp.exp(m2 - m1)
    den = 1.0 + e2
    w1 = p_grp * (1.0 / den)
    w2 = p_grp * (e2 / den)

    onehot = jnp.where((lane_f == i1) | (lane_f == i2), 1.0, 0.0)
    r_i = lax.broadcasted_iota(jnp.int32, (tm, tm), 0)
    c_i = lax.broadcasted_iota(jnp.int32, (tm, tm), 1)
    before = jnp.where(c_i < r_i, 1.0, 0.0).astype(BF16)
    base = cnt_ref[0:1, :] + _dot(before, onehot.astype(BF16))
    r1 = jnp.sum(jnp.where(lane_f == i1, base, 0.0), axis=1, keepdims=True)
    r2 = jnp.sum(jnp.where(lane_f == i2, base, 0.0), axis=1, keepdims=True)
    cnt_ref[0:1, :] = cnt_ref[0:1, :] + jnp.sum(onehot, axis=0, keepdims=True)
    counts_ref[...] = jnp.broadcast_to(cnt_ref[0:1, :], counts_ref.shape)

    route = jnp.zeros((tm, LANE), F32)
    for k, v in enumerate((i1, i2, w1, w2, r1, r2)):
        route = jnp.where(lane == k, v, route)
    route_ref[...] = route


def _outproj_router(yab, yc, wo_all, layer, x2d, nw, wr_hi, wr_lo, br, tm=512):
    T, D = x2d.shape
    full = lambda a: pl.BlockSpec(a.shape, lambda i, nd=a.ndim: (0,) * nd)
    return pl.pallas_call(
        _outproj_router_kernel,
        grid=(T // tm,),
        in_specs=[
            pl.BlockSpec((tm, yab.shape[1]), lambda i: (i, 0)),
            pl.BlockSpec((tm, yc.shape[1]), lambda i: (i, 0)),
            pl.BlockSpec((1,) + wo_all.shape[1:], lambda i: (layer, 0, 0)),
            pl.BlockSpec((tm, D), lambda i: (i, 0)),
            full(nw), full(wr_hi), full(wr_lo), full(br),
        ],
        out_specs=[pl.BlockSpec((tm, D), lambda i: (i, 0)),
                   pl.BlockSpec((tm, D // 2), lambda i: (i, 0)),
                   pl.BlockSpec((tm, LANE), lambda i: (i, 0)),
                   pl.BlockSpec((8, LANE), lambda i: (0, 0))],
        out_shape=[jax.ShapeDtypeStruct((T, D), F32),
                   jax.ShapeDtypeStruct((T, D // 2), jnp.uint32),
                   jax.ShapeDtypeStruct((T, LANE), F32),
                   jax.ShapeDtypeStruct((8, LANE), F32)],
        scratch_shapes=[pltpu.VMEM((8, LANE), F32)],
        compiler_params=_params(("arbitrary",)),
        name="outproj_router",
    )(yab, yc, wo_all, x2d, nw, wr_hi, wr_lo, br)


def _row_copy(src_ref, src_row, dst_ref, dst_row, sem):
    return pltpu.make_async_copy(src_ref.at[pl.ds(src_row, 1), :], dst_ref.at[pl.ds(dst_row, 1), :], sem)


def _dispatch_kernel(dest_ref, hnp_ref, xs_hbm, sem, *, chunk):
    i = pl.program_id(0)

    def body(j, carry):
        t = i * chunk + j
        _row_copy(hnp_ref, j, xs_hbm, dest_ref[2 * t], sem).start()
        _row_copy(hnp_ref, j, xs_hbm, dest_ref[2 * t + 1], sem).start()
        return carry

    lax.fori_loop(0, chunk, body, 0, unroll=8)
    for _ in range(2):
        pltpu.make_async_copy(hnp_ref, xs_hbm.at[pl.ds(0, chunk), :], sem).wait()


def _dispatch(dest, hnp, chunk=1024):
    T, W = hnp.shape
    return pl.pallas_call(
        functools.partial(_dispatch_kernel, chunk=chunk),
        grid_spec=pltpu.PrefetchScalarGridSpec(
            num_scalar_prefetch=1,
            grid=(T // chunk,),
            in_specs=[pl.BlockSpec((chunk, W), lambda i, d: (i, 0))],
            out_specs=pl.BlockSpec(memory_space=pl.ANY),
            scratch_shapes=[pltpu.SemaphoreType.DMA(())],
        ),
        out_shape=jax.ShapeDtypeStruct((2 * T, W), hnp.dtype),
        compiler_params=_params(("arbitrary",)),
        name="dispatch",
    )(dest, hnp)


def _experts_kernel(tile_ref, exp_ref, lo_ref, hi_ref, flag_ref, xs_ref, wg_ref, wu_ref, wd_ref,
                    ys_ref, acc_ref, wgb_ref, wub_ref, wdb_ref, *, tm):
    w = pl.program_id(0)
    lo = lo_ref[w]
    hi = hi_ref[w]
    flags = flag_ref[w]

    @pl.when((flags & 4) != 0)
    def _():
        wgb_ref[...] = wg_ref[0, 0].astype(BF16)
        wub_ref[...] = wu_ref[0, 0].astype(BF16)
        wdb_ref[...] = wd_ref[0, 0].astype(BF16)

    @pl.when((flags & 1) != 0)
    def _():
        acc_ref[...] = jnp.zeros_like(acc_ref)

    @pl.when(hi > lo)
    def _():
        x = _unpack_rows(xs_ref[...], BF16)
        hg = _dot(x, wgb_ref[...])
        hu = _dot(x, wub_ref[...])
        row = tile_ref[w] * tm + lax.broadcasted_iota(jnp.int32, hg.shape, 0)
        h = jnp.where((row >= lo) & (row < hi), _silu(hg) * hu, 0.0)
        acc_ref[...] += _dot(h.astype(BF16), wdb_ref[...])

    @pl.when((flags & 2) != 0)
    def _():
        ys_ref[...] = _pack_rows(acc_ref[...])


def _experts(meta, xs, wg_all, wu_all, wd_all, layer, tm):
    N, W = xs.shape
    _, E, D, F = wg_all.shape
    tile_w, exp_w, lo_w, hi_w, flag_w = meta
    n_work = tile_w.shape[0]
    wspec = lambda r, c: pl.BlockSpec((1, 1, r, c), lambda w, t, e, lo, hi, f: (layer, e[w], 0, 0))
    return pl.pallas_call(
        functools.partial(_experts_kernel, tm=tm),
        grid_spec=pltpu.PrefetchScalarGridSpec(
            num_scalar_prefetch=5,
            grid=(n_work,),
            in_specs=[
                pl.BlockSpec((tm, W), lambda w, t, e, lo, hi, f: (t[w], 0)),
                wspec(D, F), wspec(D, F), wspec(F, D),
            ],
            out_specs=pl.BlockSpec((tm, W), lambda w, t, e, lo, hi, f: (t[w], 0)),
            scratch_shapes=[pltpu.VMEM((tm, D), F32), pltpu.VMEM((D, F), BF16),
                            pltpu.VMEM((D, F), BF16), pltpu.VMEM((F, D), BF16)],
        ),
        out_shape=jax.ShapeDtypeStruct((N, W), xs.dtype),
        compiler_params=_params(("arbitrary",)),
        name="experts",
    )(tile_w, exp_w, lo_w, hi_w, flag_w, xs, wg_all, wu_all, wd_all)


def _work_items(counts, n_rows, tm):
    E = counts.shape[0]
    n_tiles = n_rows // tm
    n_work = n_tiles + E - 1
    start = jnp.cumsum(counts) - counts
    end = start + counts
    first_tile = start // tm
    last_tile = jnp.maximum(end - 1, 0) // tm
    n_e = jnp.where(counts > 0, last_tile - first_tile + 1, 0)
    wend = jnp.cumsum(n_e)
    wstart = wend - n_e
    total = wend[-1]
    w = jnp.arange(n_work, dtype=jnp.int32)
    wc = jnp.minimum(w, total - 1)
    ew = jnp.sum((wc[:, None] >= wend[None, :]).astype(jnp.int32), axis=1)
    tile_w = first_tile[ew] + (wc - wstart[ew])
    valid = w < total
    lo = jnp.where(valid, jnp.maximum(start[ew], tile_w * tm), 0)
    hi = jnp.where(valid, jnp.minimum(end[ew], (tile_w + 1) * tm), 0)
    prev_tile = jnp.concatenate([jnp.full((1,), -1, jnp.int32), tile_w[:-1]])
    next_tile = jnp.concatenate([tile_w[1:], jnp.full((1,), -1, jnp.int32)])
    prev_e = jnp.concatenate([jnp.full((1,), -1, jnp.int32), ew[:-1]])
    first = valid & (tile_w != prev_tile)
    last = valid & ((tile_w != next_tile) | (w == total - 1))
    new_e = ew != prev_e
    flags = first.astype(jnp.int32) + 2 * last.astype(jnp.int32) + 4 * new_e.astype(jnp.int32)
    i32 = lambda a: a.astype(jnp.int32)
    return i32(tile_w), i32(ew), i32(lo), i32(hi), i32(flags)


def _combine_kernel(dest_ref, x1_ref, route_ref, ys_hbm, o_ref, buf_ref, sem, *, tt):
    i = pl.program_id(0)
    n = pl.num_programs(0)

    def issue(step, slot):
        def body(j, carry):
            t = step * tt + j
            _row_copy(ys_hbm, dest_ref[2 * t], buf_ref.at[slot, 0], j, sem.at[slot]).start()
            _row_copy(ys_hbm, dest_ref[2 * t + 1], buf_ref.at[slot, 1], j, sem.at[slot]).start()
            return carry
        lax.fori_loop(0, tt, body, 0, unroll=8)

    @pl.when(i == 0)
    def _():
        issue(0, 0)

    @pl.when(i + 1 < n)
    def _():
        issue(i + 1, (i + 1) % 2)

    slot = i % 2
    for k in range(2):
        pltpu.make_async_copy(ys_hbm.at[pl.ds(0, tt), :], buf_ref.at[slot, k], sem.at[slot]).wait()

    lane = lax.broadcasted_iota(jnp.int32, route_ref.shape, 1)
    route = route_ref[...]
    w0 = jnp.sum(jnp.where(lane == ROUTE_W, route, 0.0), axis=1, keepdims=True)
    w1 = jnp.sum(jnp.where(lane == ROUTE_W + 1, route, 0.0), axis=1, keepdims=True)
    y0 = _unpack_rows(buf_ref[slot, 0], F32)
    y1 = _unpack_rows(buf_ref[slot, 1], F32)
    o_ref[...] = x1_ref[...] + (w0 * y0 + w1 * y1)


def _combine(dest, x1, route, ys, tt=256):
    T, D = x1.shape
    W = ys.shape[1]
    return pl.pallas_call(
        functools.partial(_combine_kernel, tt=tt),
        grid_spec=pltpu.PrefetchScalarGridSpec(
            num_scalar_prefetch=1,
            grid=(T // tt,),
            in_specs=[
                pl.BlockSpec((tt, D), lambda i, d: (i, 0)),
                pl.BlockSpec((tt, LANE), lambda i, d: (i, 0)),
                pl.BlockSpec(memory_space=pl.ANY),
            ],
            out_specs=pl.BlockSpec((tt, D), lambda i, d: (i, 0)),
            scratch_shapes=[pltpu.VMEM((2, 2, tt, W), ys.dtype), pltpu.SemaphoreType.DMA((2,))],
        ),
        out_shape=jax.ShapeDtypeStruct((T, D), F32),
        compiler_params=_params(("arbitrary",)),
        name="combine",
    )(dest, x1, route, ys)


def _moe(x1, hnp, route, counts8, wg_all, wu_all, wd_all, layer, tm=512):
    T = x1.shape[0]
    E = wg_all.shape[1]
    counts = counts8[0, :E].astype(jnp.int32)
    eid = route[:, ROUTE_E:ROUTE_E + 2].astype(jnp.int32)
    rank = route[:, ROUTE_RANK:ROUTE_RANK + 2].astype(jnp.int32)
    start = jnp.cumsum(counts) - counts
    onehot = eid[..., None] == jnp.arange(E, dtype=jnp.int32)
    dest = (jnp.sum(jnp.where(onehot, start, 0), axis=-1) + rank).reshape(2 * T)
    xs = _dispatch(dest, hnp)
    ys = _experts(_work_items(counts, 2 * T, tm), xs, wg_all, wu_all, wd_all, layer, tm)
    return _combine(dest, x1, route, ys)


def kernel(x, norm1_w, w_in, pool_w, pool_scale, conv_w, cmp_pe_k, cmp_w1_k, cmp_w2_k, cmp_pe_v, cmp_w1_v, cmp_w2_v, q_norm_w, k_norm_w, w_out, norm2_w, router_grp_w, router_grp_b, router_exp_w, router_exp_b, exp_w_gate, exp_w_up, exp_w_down):
    B, S, D = x.shape
    depth = w_in.shape[0]
    T = B * S
    xf = x.reshape(T, D)

    def w1_pair(w1):
        return jnp.concatenate([w1[:CMP_STRIDE], w1[CMP_STRIDE:]], axis=-1).astype(BF16)

    def pe_rows(pe):
        return jnp.broadcast_to(pe.reshape(1, CMP_LEN * HEAD_DIM), (8, CMP_LEN * HEAD_DIM)).astype(BF16)

    w_in_all = w_in.astype(BF16)
    w_gate_all = jnp.pad(w_in[:, :, COL_GATE:], ((0, 0), (0, 0), (0, LANE - (D_IN - COL_GATE)))).astype(BF16)
    w_out_all = w_out.astype(BF16)

    for l in range(depth):
        z, zgate = _inproj(xf, norm1_w[l].reshape(1, D), w_in_all, w_gate_all, l)

        yab = _mix_ab(z, pool_w[l].astype(BF16), pool_scale[l].reshape(1, POOL_WIDTH), conv_w[l], B, S)

        kc, vc = _compress(
            z, w1_pair(cmp_w1_k[l]), w1_pair(cmp_w1_v[l]), pe_rows(cmp_pe_k[l]), pe_rows(cmp_pe_v[l]),
            cmp_w1_k[l].reshape(CMP_LEN * HEAD_DIM, CMP_HIDDEN).astype(BF16),
            cmp_w1_v[l].reshape(CMP_LEN * HEAD_DIM, CMP_HIDDEN).astype(BF16),
            cmp_w2_k[l].astype(BF16), cmp_w2_v[l].astype(BF16), k_norm_w[l], B, S)
        yc = _nsa(z, zgate, kc, vc, q_norm_w[l], k_norm_w[l], B, S)

        wr = jnp.concatenate([router_exp_w[l], router_grp_w[l]], axis=1)
        wr = jnp.pad(wr, ((0, 0), (0, LANE - wr.shape[1])))
        wr_hi = wr.astype(BF16)
        wr_lo = (wr - wr_hi.astype(F32)).astype(BF16)
        br = jnp.concatenate([router_exp_b[l], router_grp_b[l]])
        br = jnp.pad(br, (0, LANE - br.shape[0])).reshape(1, LANE)
        x1, hnp, route, counts8 = _outproj_router(yab, yc, w_out_all, l, xf, norm2_w[l].reshape(1, D),
                                                  wr_hi, wr_lo, br)

        xf = _moe(x1, hnp, route, counts8, exp_w_gate, exp_w_up, exp_w_down, l)
    return xf.reshape(B, S, D)
```

```python
import functools

import numpy as np
import jax
import jax.numpy as jnp
from jax import lax
from jax.experimental import pallas as pl
from jax.experimental.pallas import tpu as pltpu

F32 = jnp.float32
BF16 = jnp.bfloat16

POOL_WINDOWS = (2, 4, 8, 16)
LANE = 128
POOL_WIDTH = 512
CONV_WIDTH = 512
CONV_K = 3
NSA_WIDTH = 1024
HEAD_DIM = 128
NSA_KV_HEADS = 2
NSA_GROUP = 4
N_BRANCH = 3
CMP_LEN = 32
CMP_STRIDE = 16
CMP_HIDDEN = 256
SEL_LEN = 64
N_SELECT = 16
SEL_FORCE = 1.0e4
WINDOW = 512
N_GROUPS_MOE = 4
EXPERTS_PER_GROUP = 8
N_EXPERTS = 32
D_EXPERT = 256
EPS = 1e-6
NEG = -1e30
LOG2_E = 1.4426950408889634

COL_U = 0
COL_B = 512
COL_C = 1024
COL_V = 1536
COL_Q = 2048
COL_KC = 3072
COL_KS = 3584
COL_VS = 3840
COL_KW = 4096
COL_VW = 4352
COL_GATE = 4608
D_IN = 4632

VMEM_LIMIT = 56 * 1024 * 1024


def _params(sem):
    return pltpu.CompilerParams(dimension_semantics=sem, vmem_limit_bytes=VMEM_LIMIT)


def _rms(x, w):
    return x * lax.rsqrt(jnp.mean(x * x, axis=-1, keepdims=True) + EPS) * w


def _silu(x):
    return x / (1.0 + jnp.exp(-x))


def _dot(a, b):
    return jnp.dot(a, b, preferred_element_type=F32)


def _split3(a):
    hi = a.astype(BF16)
    r1 = a - hi.astype(F32)
    mid = r1.astype(BF16)
    lo = (r1 - mid.astype(F32)).astype(BF16)
    return hi, mid, lo


def _inproj_kernel(x_ref, nw_ref, w_ref, wgate_ref, o_ref, gate_ref, xn_ref):
    @pl.when(pl.program_id(1) == 0)
    def _():
        xn = _rms(x_ref[...], nw_ref[...]).astype(BF16)
        xn_ref[...] = xn
        gate_ref[...] = _dot(xn, wgate_ref[0])

    o_ref[...] = _dot(xn_ref[...], w_ref[0]).astype(o_ref.dtype)


def _inproj(x2d, nw, w_all, wgate_all, layer, tm=1024, tn=1536):
    T, D = x2d.shape
    N = COL_GATE
    assert N % tn == 0 and w_all.shape[2] >= N
    return pl.pallas_call(
        _inproj_kernel,
        grid=(T // tm, N // tn),
        in_specs=[
            pl.BlockSpec((tm, D), lambda i, j: (i, 0)),
            pl.BlockSpec((1, D), lambda i, j: (0, 0)),
            pl.BlockSpec((1, D, tn), lambda i, j: (layer, 0, j)),
            pl.BlockSpec((1, D, LANE), lambda i, j: (layer, 0, 0)),
        ],
        out_specs=[pl.BlockSpec((tm, tn), lambda i, j: (i, j)),
                   pl.BlockSpec((tm, LANE), lambda i, j: (i, 0))],
        out_shape=[jax.ShapeDtypeStruct((T, N), BF16), jax.ShapeDtypeStruct((T, LANE), F32)],
        scratch_shapes=[pltpu.VMEM((tm, D), BF16)],
        compiler_params=_params(("parallel", "arbitrary")),
        name="inproj",
    )(x2d, nw, w_all, wgate_all)


def _mix_ab_kernel(u_ref, b_ref, c_ref, v_ref, pw_ref, ps_ref, cw_ref, o_ref):
    S = u_ref.shape[0]
    row = lax.broadcasted_iota(jnp.int32, (S, LANE), 0)

    def shift(a, k):
        return jnp.where(row >= k, pltpu.roll(a, k, axis=0), 0.0)

    for g, w in enumerate(POOL_WINDOWS):
        sl = slice(g * LANE, (g + 1) * LANE)
        u = u_ref[:, sl].astype(F32)
        s = u
        k = 1
        while k < w:
            s = s + shift(s, k)
            k *= 2
        cnt = jnp.minimum(row + 1, w).astype(F32)
        mixed = (s / cnt - u).astype(BF16)
        y = _dot(mixed, pw_ref[g]) * ps_ref[:, sl]
        o_ref[:, sl] = y.astype(o_ref.dtype)

    for h in range(CONV_WIDTH // LANE):
        sl = slice(h * LANE, (h + 1) * LANE)
        u2 = c_ref[:, sl].astype(F32) * v_ref[:, sl].astype(F32)
        y = cw_ref[0:1, sl] * shift(u2, 2)
        y = y + cw_ref[1:2, sl] * shift(u2, 1)
        y = y + cw_ref[2:3, sl] * u2
        y = b_ref[:, sl].astype(F32) * y
        o_ref[:, POOL_WIDTH + h * LANE:POOL_WIDTH + (h + 1) * LANE] = y.astype(o_ref.dtype)


def _mix_ab(z, pool_w_bf16, pool_scale, conv_w, B, S):
    T = z.shape[0]
    blk = lambda c: pl.BlockSpec((S, 512), lambda b, c=c: (b, c))
    return pl.pallas_call(
        _mix_ab_kernel,
        grid=(B,),
        in_specs=[
            blk(COL_U // 512), blk(COL_B // 512), blk(COL_C // 512), blk(COL_V // 512),
            pl.BlockSpec((4, LANE, LANE), lambda b: (0, 0, 0)),
            pl.BlockSpec((1, POOL_WIDTH), lambda b: (0, 0)),
            pl.BlockSpec((CONV_K, CONV_WIDTH), lambda b: (0, 0)),
        ],
        out_specs=pl.BlockSpec((S, POOL_WIDTH + CONV_WIDTH), lambda b: (b, 0)),
        out_shape=jax.ShapeDtypeStruct((T, POOL_WIDTH + CONV_WIDTH), BF16),
        compiler_params=_params(("parallel",)),
        name="mix_ab",
    )(z, z, z, z, pool_w_bf16, pool_scale, conv_w)


def _compress_kernel(z_ref, w1k_ref, w1v_ref, pek_ref, pev_ref, w1kf_ref, w1vf_ref,
                     w2k_ref, w2v_ref, knw_ref, kc_ref, vc_ref, xf_ref):
    assert CMP_LEN == 2 * CMP_STRIDE
    n16 = z_ref.shape[0] // CMP_STRIDE
    for cg in range(z_ref.shape[1] // HEAD_DIM):
        xf_ref[cg] = z_ref[:, cg * HEAD_DIM:(cg + 1) * HEAD_DIM].astype(F32)
    streams = ((w1k_ref, pek_ref, w1kf_ref, w2k_ref, kc_ref),
               (w1v_ref, pev_ref, w1vf_ref, w2v_ref, vc_ref))
    for which, (w1_ref, pe_ref, w1f_ref, w2_ref, out_ref) in enumerate(streams):
        pe_term = _dot(pe_ref[...], w1f_ref[...])[0:1, :]
        for h in range(NSA_KV_HEADS):
            cg = which * NSA_KV_HEADS + h
            acc = None
            for l in range(CMP_STRIDE):
                rows = xf_ref[cg, pl.ds(l, n16, stride=CMP_STRIDE), :]
                part = _dot(rows.astype(BF16), w1_ref[l])
                acc = part if acc is None else acc + part
            first = acc[:, :CMP_HIDDEN]
            second = acc[:, CMP_HIDDEN:]
            hid = first + pltpu.roll(second, n16 - 1, axis=0) + pe_term
            out = _dot(_silu(hid).astype(BF16), w2_ref[...])
            if which == 0:
                out = _rms(out, knw_ref[0:1, :])
            out_ref[0, h] = out.astype(out_ref.dtype)


def _compress(z, w1k, w1v, pek8, pev8, w1kf, w1vf, w2k, w2v, knw, B, S):
    n16 = S // CMP_STRIDE
    full = lambda a: pl.BlockSpec(a.shape, lambda b, nd=a.ndim: (0,) * nd)
    out_sds = jax.ShapeDtypeStruct((B, NSA_KV_HEADS, n16, HEAD_DIM), BF16)
    out_spec = pl.BlockSpec((1, NSA_KV_HEADS, n16, HEAD_DIM), lambda b: (b, 0, 0, 0))
    return pl.pallas_call(
        _compress_kernel,
        grid=(B,),
        in_specs=[pl.BlockSpec((S, 512), lambda b: (b, COL_KC // 512)),
                  full(w1k), full(w1v), full(pek8), full(pev8), full(w1kf), full(w1vf),
                  full(w2k), full(w2v), full(knw)],
        out_specs=[out_spec, out_spec],
        out_shape=[out_sds, out_sds],
        scratch_shapes=[pltpu.VMEM((512 // HEAD_DIM, S, HEAD_DIM), F32)],
        compiler_params=_params(("parallel",)),
        name="compress",
    )(z, w1k, w1v, pek8, pev8, w1kf, w1vf, w2k, w2v, knw)


V_ROWS = HEAD_DIM + 16

def _nsa_kernel(zq_ref, zg_ref, ks_ref, vs_ref, kw_ref, vw_ref, kc_ref, vc_ref,
                qnw_ref, knw_ref, ovlt_ref, negexp_ref, wband_ref, o_ref,
                kaug_ref, kwn_ref, vst_ref, vwt_ref, vct_ref, gt_ref, m_ref, acc_ref,
                ocmp_ref, owin_ref, qaug_ref, sbuf_ref, *, tq):
    G = NSA_GROUP
    S = ks_ref.shape[0]
    hkv = pl.program_id(1)
    i = pl.program_id(2)
    tk = tq
    t0 = i * tq

    def transpose_to_bf16(a):
        return a.astype(F32).T.astype(BF16)

    @pl.when(i == 0)
    def _():
        kaug_ref[:, 0:HEAD_DIM] = _rms(ks_ref[...].astype(F32), knw_ref[1:2, :]).astype(BF16)
        kaug_ref[:, HEAD_DIM:] = negexp_ref[...]
        kwn_ref[...] = _rms(kw_ref[...].astype(F32), knw_ref[2:3, :]).astype(BF16)
        ones_row = jnp.where(lax.broadcasted_iota(jnp.int32, (V_ROWS - HEAD_DIM, tk), 0) == 0, 1.0, 0.0)
        for j in range(S // tk):
            vst_ref[j, 0:HEAD_DIM, :] = transpose_to_bf16(vs_ref[j * tk:(j + 1) * tk, :])
            vwt_ref[j, 0:HEAD_DIM, :] = transpose_to_bf16(vw_ref[j * tk:(j + 1) * tk, :])
            vst_ref[j, HEAD_DIM:, :] = ones_row.astype(BF16)
            vwt_ref[j, HEAD_DIM:, :] = ones_row.astype(BF16)
        vct_ref[...] = transpose_to_bf16(vc_ref[0, 0])

    qts = []
    for g in range(G):
        qt = zq_ref[:, g * HEAD_DIM:(g + 1) * HEAD_DIM].astype(F32).T
        ms = jnp.mean(qt * qt, axis=0, keepdims=True)
        qt = qt * lax.rsqrt(ms + EPS) * qnw_ref[...] * (HEAD_DIM ** -0.5 * LOG2_E)
        qts.append(qt.astype(BF16))
    q4t = jnp.concatenate(qts, axis=1)
    per_head = tq // LANE
    n_ch = G * per_head
    cols = [slice(c * LANE, (c + 1) * LANE) for c in range(n_ch)]
    qcols = [slice((c % per_head) * LANE, (c % per_head + 1) * LANE) for c in range(n_ch)]

    wk = WINDOW + tq
    ws = pl.multiple_of(jnp.maximum(t0 - WINDOW, 0), tq)
    band = wband_ref.at[jnp.minimum(i, WINDOW // tq)]
    sw = _dot(kwn_ref[pl.ds(ws, wk), :], q4t)
    for c in range(n_ch):
        sg = sw[:, cols[c]] + band[:, qcols[c]]
        p = jnp.exp2(sg - jnp.max(sg, axis=0, keepdims=True)).astype(BF16)
        acc_w = None
        for j in range(wk // tk):
            part = _dot(vwt_ref[ws // tk + j], p[j * tk:(j + 1) * tk, :])
            acc_w = part if acc_w is None else acc_w + part
        owin_ref[:, cols[c]] = acc_w[0:HEAD_DIM] * (1.0 / acc_w[HEAD_DIM:HEAD_DIM + 1])

    sc = _dot(kc_ref[0, 0], q4t)
    n_sub = lax.broadcasted_iota(jnp.int32, (LANE, LANE), 0)
    pts = []
    psums = [None] * per_head
    for c in range(n_ch):
        t_lane = t0 + (c % per_head) * LANE + lax.broadcasted_iota(jnp.int32, (LANE, LANE), 1)
        cm = (n_sub * CMP_STRIDE + (CMP_LEN - 1)) <= t_lane
        sg = jnp.where(cm, sc[:, cols[c]], NEG)
        e = jnp.exp2(sg - jnp.max(sg, axis=0, keepdims=True))
        p = e * (1.0 / jnp.sum(e, axis=0, keepdims=True))
        p = jnp.where(cm, p, 0.0)
        pts.append(p.astype(BF16))
        k = c % per_head
        psums[k] = p if psums[k] is None else psums[k] + p
    ocmp_ref[...] = _dot(vct_ref[...], jnp.concatenate(pts, axis=1))

    hi, mid, lo = _split3(jnp.concatenate(psums, axis=1))
    imp = _dot(ovlt_ref[...], hi) + _dot(ovlt_ref[...], mid) + _dot(ovlt_ref[...], lo)
    n_sel = S // SEL_LEN
    n_top = min(N_SELECT, n_sel)
    j_sub = lax.broadcasted_iota(jnp.int32, (n_sel, tq), 0)
    t_sel = t0 + lax.broadcasted_iota(jnp.int32, (n_sel, tq), 1)
    forced = (j_sub == (t_sel // SEL_LEN)) | (j_sub == 0)
    valid = (j_sub * SEL_LEN) <= t_sel
    score = jnp.where(forced, SEL_FORCE, jnp.where(valid, imp[0:n_sel, :], -1.0))
    rank = jnp.zeros((n_sel, tq), F32)
    for c in range(n_sel):
        other = score[c:c + 1, :]
        beats = jnp.where(other > score, 1.0, jnp.where((other == score) & (j_sub > c), 1.0, 0.0))
        rank = rank + beats
    unsel = jnp.where(rank < n_top, 0.0, 1.0)
    unsel = jnp.concatenate([unsel, jnp.zeros((LANE - n_sel, tq), F32)], axis=0).astype(BF16)
    qaug_ref[0:HEAD_DIM, :] = q4t
    qaug_ref[HEAD_DIM:, :] = jnp.concatenate([unsel] * G, axis=1)

    m_ref[...] = jnp.full(m_ref.shape, NEG, F32)
    acc_ref[...] = jnp.zeros(acc_ref.shape, F32)
    k_sub = lax.broadcasted_iota(jnp.int32, (tk, LANE), 0)
    q_lane = lax.broadcasted_iota(jnp.int32, (tk, LANE), 1)

    def sel_scores(kt, slot):
        k0 = pl.multiple_of(kt * tk, tk)
        sbuf_ref[slot] = _dot(kaug_ref[pl.ds(k0, tk), :], qaug_ref[...])

    def sel_update(kt, slot, diagonal):
        vt = vst_ref[kt]
        scores = sbuf_ref.at[slot]
        for c in range(n_ch):
            sg = scores[:, cols[c]]
            if diagonal:
                sg = jnp.where(k_sub <= q_lane + (c % per_head) * LANE, sg, NEG)
            m_old = m_ref[:, cols[c]]
            m_new = jnp.maximum(m_old, jnp.max(sg, axis=0, keepdims=True))
            alpha = jnp.exp2(m_old - m_new)
            p = jnp.exp2(sg - m_new).astype(BF16)
            acc_ref[:, cols[c]] = alpha * acc_ref[:, cols[c]] + _dot(vt, p)
            m_ref[:, cols[c]] = m_new

    def sel_body(kt, carry):
        sel_update(kt, kt % 2, False)
        sel_scores(kt + 1, (kt + 1) % 2)
        return carry

    sel_scores(0, 0)
    lax.fori_loop(0, i, sel_body, 0)
    sel_update(i, i % 2, True)

    gt_ref[...] = (1.0 / (1.0 + jnp.exp(-zg_ref[...].astype(F32)))).T
    n_heads = NSA_KV_HEADS * G
    for c in range(n_ch):
        g = c // per_head
        col = hkv * G + g
        gate = lambda branch: gt_ref[pl.ds(branch * n_heads + col, 1), :][:, qcols[c]]
        o_sel = acc_ref[0:HEAD_DIM, cols[c]] * (1.0 / acc_ref[HEAD_DIM:HEAD_DIM + 1, cols[c]])
        out = gate(0) * ocmp_ref[:, cols[c]] + gate(1) * o_sel + gate(2) * owin_ref[:, cols[c]]
        o_ref[qcols[c], g * HEAD_DIM:(g + 1) * HEAD_DIM] = out.T.astype(o_ref.dtype)


def _nsa(z, zgate, kc, vc, qnw, knw, B, S, tq=256):
    T = z.shape[0]
    nq = S // tq
    G = NSA_GROUP
    n_c = (S - CMP_LEN) // CMP_STRIDE + 1
    n_sel = S // SEL_LEN
    assert S % tq == 0 and n_sel <= LANE and n_sel % 8 == 0 and n_c <= LANE and WINDOW % tq == 0
    assert S >= WINDOW + tq
    ci = np.arange(LANE)[None, :] * CMP_STRIDE
    sj = np.arange(LANE)[:, None] * SEL_LEN
    ovlt = ((ci < sj + SEL_LEN) & (ci + CMP_LEN > sj) & (np.arange(LANE)[None, :] < n_c)
            & (np.arange(LANE)[:, None] < n_sel))
    ovlt = jnp.asarray(ovlt.astype(np.float32), BF16)
    in_block = (np.arange(S)[:, None] // SEL_LEN) == np.arange(LANE)[None, :]
    negexp = jnp.asarray(in_block.astype(np.float32) * NEG, BF16)
    qnw_b = jnp.broadcast_to(qnw.reshape(HEAD_DIM, 1), (HEAD_DIM, tq))
    bands = []
    for p in range(WINDOW // tq + 1):
        t0, ws = p * tq, max(p * tq - WINDOW, 0)
        dist = (t0 + np.arange(tq)[None, :]) - (ws + np.arange(WINDOW + tq)[:, None])
        bands.append(np.where((dist >= 0) & (dist < WINDOW), 0.0, NEG))
    wband = jnp.asarray(np.stack(bands), F32)

    kvblk = lambda c: pl.BlockSpec((S, HEAD_DIM), lambda b, h, i, c=c: (b, c + h))
    cblk = pl.BlockSpec((1, 1, LANE, HEAD_DIM), lambda b, h, i: (b, h, 0, 0))
    full = lambda a: pl.BlockSpec(a.shape, lambda b, h, i, nd=a.ndim: (0,) * nd)
    return pl.pallas_call(
        functools.partial(_nsa_kernel, tq=tq),
        grid=(B, NSA_KV_HEADS, nq),
        in_specs=[
            pl.BlockSpec((tq, G * HEAD_DIM), lambda b, h, i: (b * nq + i, COL_Q // 512 + h)),
            pl.BlockSpec((tq, LANE), lambda b, h, i: (b * nq + i, 0)),
            kvblk(COL_KS // LANE), kvblk(COL_VS // LANE), kvblk(COL_KW // LANE), kvblk(COL_VW // LANE),
            cblk, cblk, full(qnw_b), full(knw), full(ovlt), full(negexp), full(wband),
        ],
        out_specs=pl.BlockSpec((tq, G * HEAD_DIM), lambda b, h, i: (b * nq + i, h)),
        out_shape=jax.ShapeDtypeStruct((T, NSA_WIDTH), BF16),
        scratch_shapes=[
            pltpu.VMEM((S, 2 * HEAD_DIM), BF16), pltpu.VMEM((S, HEAD_DIM), BF16),
            pltpu.VMEM((S // tq, V_ROWS, tq), BF16), pltpu.VMEM((S // tq, V_ROWS, tq), BF16),
            pltpu.VMEM((HEAD_DIM, LANE), BF16), pltpu.VMEM((LANE, tq), F32),
            pltpu.VMEM((1, G * tq), F32), pltpu.VMEM((V_ROWS, G * tq), F32),
            pltpu.VMEM((HEAD_DIM, G * tq), F32), pltpu.VMEM((HEAD_DIM, G * tq), F32),
            pltpu.VMEM((2 * HEAD_DIM, G * tq), BF16), pltpu.VMEM((2, tq, G * tq), F32),
        ],
        compiler_params=_params(("parallel", "parallel", "arbitrary")),
        name="nsa",
    )(z, zgate, z, z, z, z, kc, vc, qnw_b, knw, ovlt, negexp, wband)


def _pack_rows(x):
    w = x.shape[1] // 2
    return pltpu.pack_elementwise([x[:, :w], x[:, w:]], packed_dtype=BF16)


def _unpack_rows(p, dtype):
    lo = pltpu.unpack_elementwise(p, index=0, packed_dtype=BF16, unpacked_dtype=F32)
    hi = pltpu.unpack_elementwise(p, index=1, packed_dtype=BF16, unpacked_dtype=F32)
    return jnp.concatenate([lo.astype(dtype), hi.astype(dtype)], axis=1)


ROUTE_E = 0
ROUTE_W = 2
ROUTE_RANK = 4


def _outproj_router_kernel(yab_ref, yc_ref, wo_ref, x_ref, nw_ref, wrh_ref, wrl_ref, br_ref,
                           x1_ref, hnp_ref, route_ref, counts_ref, cnt_ref):
    @pl.when(pl.program_id(0) == 0)
    def _():
        cnt_ref[...] = jnp.zeros_like(cnt_ref)

    ka = yab_ref.shape[1]
    acc = _dot(yab_ref[...], wo_ref[0, 0:ka, :]) + _dot(yc_ref[...], wo_ref[0, ka:, :])
    x1 = x_ref[...] + acc
    x1_ref[...] = x1
    hn = _rms(x1, nw_ref[...])
    hnp_ref[...] = _pack_rows(hn)
    hi = hn.astype(BF16)
    lo = (hn - hi.astype(F32)).astype(BF16)
    lg = _dot(hi, wrh_ref[...]) + _dot(lo, wrh_ref[...]) + _dot(hi, wrl_ref[...]) + br_ref[...]

    tm = lg.shape[0]
    lane = lax.broadcasted_iota(jnp.int32, (tm, LANE), 1)
    lane_f = lane.astype(F32)
    big = float(LANE)

    def first_max(v):
        m = jnp.max(v, axis=1, keepdims=True)
        idx = jnp.min(jnp.where(v == m, lane_f, big), axis=1, keepdims=True)
        return m, idx

    is_grp = (lane >= N_EXPERTS) & (lane < N_EXPERTS + N_GROUPS_MOE)
    lgm = jnp.where(is_grp, lg, NEG)
    mg, grp_lane = first_max(lgm)
    p_grp = 1.0 / jnp.sum(jnp.where(is_grp, jnp.exp(lgm - mg), 0.0), axis=1, keepdims=True)
    grp = grp_lane - float(N_EXPERTS)
    in_grp = (lane < N_EXPERTS) & ((lane // EXPERTS_PER_GROUP).astype(F32) == grp)
    le = jnp.where(in_grp, lg, NEG)
    m1, i1 = first_max(le)
    le2 = jnp.where(lane_f == i1, NEG, le)
    m2, i2 = first_max(le2)
    e2 = jnp.exp(m2 - m1)
    den = 1.0 + e2
    w1 = p_grp * (1.0 / den)
    w2 = p_grp * (e2 / den)

    onehot = jnp.where((lane_f == i1) | (lane_f == i2), 1.0, 0.0)
    r_i = lax.broadcasted_iota(jnp.int32, (tm, tm), 0)
    c_i = lax.broadcasted_iota(jnp.int32, (tm, tm), 1)
    before = jnp.where(c_i < r_i, 1.0, 0.0).astype(BF16)
    base = cnt_ref[0:1, :] + _dot(before, onehot.astype(BF16))
    r1 = jnp.sum(jnp.where(lane_f == i1, base, 0.0), axis=1, keepdims=True)
    r2 = jnp.sum(jnp.where(lane_f == i2, base, 0.0), axis=1, keepdims=True)
    cnt_ref[0:1, :] = cnt_ref[0:1, :] + jnp.sum(onehot, axis=0, keepdims=True)
    counts_ref[...] = jnp.broadcast_to(cnt_ref[0:1, :], counts_ref.shape)

    route = jnp.zeros((tm, LANE), F32)
    for k, v in enumerate((i1, i2, w1, w2, r1, r2)):
        route = jnp.where(lane == k, v, route)
    route_ref[...] = route


def _outproj_router(yab, yc, wo_all, layer, x2d, nw, wr_hi, wr_lo, br, tm=512):
    T, D = x2d.shape
    full = lambda a: pl.BlockSpec(a.shape, lambda i, nd=a.ndim: (0,) * nd)
    return pl.pallas_call(
        _outproj_router_kernel,
        grid=(T // tm,),
        in_specs=[
            pl.BlockSpec((tm, yab.shape[1]), lambda i: (i, 0)),
            pl.BlockSpec((tm, yc.shape[1]), lambda i: (i, 0)),
            pl.BlockSpec((1,) + wo_all.shape[1:], lambda i: (layer, 0, 0)),
            pl.BlockSpec((tm, D), lambda i: (i, 0)),
            full(nw), full(wr_hi), full(wr_lo), full(br),
        ],
        out_specs=[pl.BlockSpec((tm, D), lambda i: (i, 0)),
                   pl.BlockSpec((tm, D // 2), lambda i: (i, 0)),
                   pl.BlockSpec((tm, LANE), lambda i: (i, 0)),
                   pl.BlockSpec((8, LANE), lambda i: (0, 0))],
        out_shape=[jax.ShapeDtypeStruct((T, D), F32),
                   jax.ShapeDtypeStruct((T, D // 2), jnp.uint32),
                   jax.ShapeDtypeStruct((T, LANE), F32),
                   jax.ShapeDtypeStruct((8, LANE), F32)],
        scratch_shapes=[pltpu.VMEM((8, LANE), F32)],
        compiler_params=_params(("arbitrary",)),
        name="outproj_router",
    )(yab, yc, wo_all, x2d, nw, wr_hi, wr_lo, br)


def _row_copy(src_ref, src_row, dst_ref, dst_row, sem):
    return pltpu.make_async_copy(src_ref.at[pl.ds(src_row, 1), :], dst_ref.at[pl.ds(dst_row, 1), :], sem)


def _dispatch_kernel(dest_ref, hnp_ref, xs_hbm, sem, *, chunk):
    i = pl.program_id(0)

    def body(j, carry):
        t = i * chunk + j
        _row_copy(hnp_ref, j, xs_hbm, dest_ref[2 * t], sem).start()
        _row_copy(hnp_ref, j, xs_hbm, dest_ref[2 * t + 1], sem).start()
        return carry

    lax.fori_loop(0, chunk, body, 0, unroll=8)
    for _ in range(2):
        pltpu.make_async_copy(hnp_ref, xs_hbm.at[pl.ds(0, chunk), :], sem).wait()


def _dispatch(dest, hnp, chunk=2048):
    T, W = hnp.shape
    return pl.pallas_call(
        functools.partial(_dispatch_kernel, chunk=chunk),
        grid_spec=pltpu.PrefetchScalarGridSpec(
            num_scalar_prefetch=1,
            grid=(T // chunk,),
            in_specs=[pl.BlockSpec((chunk, W), lambda i, d: (i, 0))],
            out_specs=pl.BlockSpec(memory_space=pl.ANY),
            scratch_shapes=[pltpu.SemaphoreType.DMA(())],
        ),
        out_shape=jax.ShapeDtypeStruct((2 * T, W), hnp.dtype),
        compiler_params=_params(("arbitrary",)),
        name="dispatch",
    )(dest, hnp)


def _experts_kernel(tile_ref, exp_ref, lo_ref, hi_ref, flag_ref, xs_ref, wg_ref, wu_ref, wd_ref,
                    ys_ref, acc_ref, wgb_ref, wub_ref, wdb_ref, *, tm):
    w = pl.program_id(0)
    lo = lo_ref[w]
    hi = hi_ref[w]
    flags = flag_ref[w]

    @pl.when((flags & 4) != 0)
    def _():
        wgb_ref[...] = wg_ref[0, 0].astype(BF16)
        wub_ref[...] = wu_ref[0, 0].astype(BF16)
        wdb_ref[...] = wd_ref[0, 0].astype(BF16)

    @pl.when((flags & 1) != 0)
    def _():
        acc_ref[...] = jnp.zeros_like(acc_ref)

    @pl.when(hi > lo)
    def _():
        x = _unpack_rows(xs_ref[...], BF16)
        hg = _dot(x, wgb_ref[...])
        hu = _dot(x, wub_ref[...])
        row = tile_ref[w] * tm + lax.broadcasted_iota(jnp.int32, hg.shape, 0)
        h = jnp.where((row >= lo) & (row < hi), _silu(hg) * hu, 0.0)
        acc_ref[...] += _dot(h.astype(BF16), wdb_ref[...])

    @pl.when((flags & 2) != 0)
    def _():
        ys_ref[...] = _pack_rows(acc_ref[...])


def _experts(meta, xs, wg_all, wu_all, wd_all, layer, tm):
    N, W = xs.shape
    _, E, D, F = wg_all.shape
    tile_w, exp_w, lo_w, hi_w, flag_w = meta
    n_work = tile_w.shape[0]
    wspec = lambda r, c: pl.BlockSpec((1, 1, r, c), lambda w, t, e, lo, hi, f: (layer, e[w], 0, 0))
    return pl.pallas_call(
        functools.partial(_experts_kernel, tm=tm),
        grid_spec=pltpu.PrefetchScalarGridSpec(
            num_scalar_prefetch=5,
            grid=(n_work,),
            in_specs=[
                pl.BlockSpec((tm, W), lambda w, t, e, lo, hi, f: (t[w], 0)),
                wspec(D, F), wspec(D, F), wspec(F, D),
            ],
            out_specs=pl.BlockSpec((tm, W), lambda w, t, e, lo, hi, f: (t[w], 0)),
            scratch_shapes=[pltpu.VMEM((tm, D), F32), pltpu.VMEM((D, F), BF16),
                            pltpu.VMEM((D, F), BF16), pltpu.VMEM((F, D), BF16)],
        ),
        out_shape=jax.ShapeDtypeStruct((N, W), xs.dtype),
        compiler_params=_params(("arbitrary",)),
        name="experts",
    )(tile_w, exp_w, lo_w, hi_w, flag_w, xs, wg_all, wu_all, wd_all)


def _work_items(counts, n_rows, tm):
    E = counts.shape[0]
    n_tiles = n_rows // tm
    n_work = n_tiles + E - 1
    start = jnp.cumsum(counts) - counts
    end = start + counts
    first_tile = start // tm
    last_tile = jnp.maximum(end - 1, 0) // tm
    n_e = jnp.where(counts > 0, last_tile - first_tile + 1, 0)
    wend = jnp.cumsum(n_e)
    wstart = wend - n_e
    total = wend[-1]
    w = jnp.arange(n_work, dtype=jnp.int32)
    wc = jnp.minimum(w, total - 1)
    ew = jnp.sum((wc[:, None] >= wend[None, :]).astype(jnp.int32), axis=1)
    tile_w = first_tile[ew] + (wc - wstart[ew])
    valid = w < total
    lo = jnp.where(valid, jnp.maximum(start[ew], tile_w * tm), 0)
    hi = jnp.where(valid, jnp.minimum(end[ew], (tile_w + 1) * tm), 0)
    prev_tile = jnp.concatenate([jnp.full((1,), -1, jnp.int32), tile_w[:-1]])
    next_tile = jnp.concatenate([tile_w[1:], jnp.full((1,), -1, jnp.int32)])
    prev_e = jnp.concatenate([jnp.full((1,), -1, jnp.int32), ew[:-1]])
    first = valid & (tile_w != prev_tile)
    last = valid & ((tile_w != next_tile) | (w == total - 1))
    new_e = ew != prev_e
    flags = first.astype(jnp.int32) + 2 * last.astype(jnp.int32) + 4 * new_e.astype(jnp.int32)
    i32 = lambda a: a.astype(jnp.int32)
    return i32(tile_w), i32(ew), i32(lo), i32(hi), i32(flags)


def _combine_kernel(dest_ref, x1_ref, route_ref, ys_hbm, o_ref, buf_ref, sem, *, tt):
    i = pl.program_id(0)
    n = pl.num_programs(0)

    def issue(step, slot):
        def body(j, carry):
            t = step * tt + j
            _row_copy(ys_hbm, dest_ref[2 * t], buf_ref.at[slot, 0], j, sem.at[slot]).start()
            _row_copy(ys_hbm, dest_ref[2 * t + 1], buf_ref.at[slot, 1], j, sem.at[slot]).start()
            return carry
        lax.fori_loop(0, tt, body, 0, unroll=8)

    @pl.when(i == 0)
    def _():
        issue(0, 0)

    @pl.when(i + 1 < n)
    def _():
        issue(i + 1, (i + 1) % 2)

    slot = i % 2
    for k in range(2):
        pltpu.make_async_copy(ys_hbm.at[pl.ds(0, tt), :], buf_ref.at[slot, k], sem.at[slot]).wait()

    lane = lax.broadcasted_iota(jnp.int32, route_ref.shape, 1)
    route = route_ref[...]
    w0 = jnp.sum(jnp.where(lane == ROUTE_W, route, 0.0), axis=1, keepdims=True)
    w1 = jnp.sum(jnp.where(lane == ROUTE_W + 1, route, 0.0), axis=1, keepdims=True)
    y0 = _unpack_rows(buf_ref[slot, 0], F32)
    y1 = _unpack_rows(buf_ref[slot, 1], F32)
    o_ref[...] = x1_ref[...] + (w0 * y0 + w1 * y1)


def _combine(dest, x1, route, ys, tt=512):
    T, D = x1.shape
    W = ys.shape[1]
    return pl.pallas_call(
        functools.partial(_combine_kernel, tt=tt),
        grid_spec=pltpu.PrefetchScalarGridSpec(
            num_scalar_prefetch=1,
            grid=(T // tt,),
            in_specs=[
                pl.BlockSpec((tt, D), lambda i, d: (i, 0)),
                pl.BlockSpec((tt, LANE), lambda i, d: (i, 0)),
                pl.BlockSpec(memory_space=pl.ANY),
            ],
            out_specs=pl.BlockSpec((tt, D), lambda i, d: (i, 0)),
            scratch_shapes=[pltpu.VMEM((2, 2, tt, W), ys.dtype), pltpu.SemaphoreType.DMA((2,))],
        ),
        out_shape=jax.ShapeDtypeStruct((T, D), F32),
        compiler_params=_params(("arbitrary",)),
        name="combine",
    )(dest, x1, route, ys)


def _moe(x1, hnp, route, counts8, wg_all, wu_all, wd_all, layer, tm=512):
    T = x1.shape[0]
    E = wg_all.shape[1]
    counts = counts8[0, :E].astype(jnp.int32)
    eid = route[:, ROUTE_E:ROUTE_E + 2].astype(jnp.int32)
    rank = route[:, ROUTE_RANK:ROUTE_RANK + 2].astype(jnp.int32)
    start = jnp.cumsum(counts) - counts
    onehot = eid[..., None] == jnp.arange(E, dtype=jnp.int32)
    dest = (jnp.sum(jnp.where(onehot, start, 0), axis=-1) + rank).reshape(2 * T)
    xs = _dispatch(dest, hnp)
    ys = _experts(_work_items(counts, 2 * T, tm), xs, wg_all, wu_all, wd_all, layer, tm)
    return _combine(dest, x1, route, ys)


def kernel(x, norm1_w, w_in, pool_w, pool_scale, conv_w, cmp_pe_k, cmp_w1_k, cmp_w2_k, cmp_pe_v, cmp_w1_v, cmp_w2_v, q_norm_w, k_norm_w, w_out, norm2_w, router_grp_w, router_grp_b, router_exp_w, router_exp_b, exp_w_gate, exp_w_up, exp_w_down):
    B, S, D = x.shape
    depth = w_in.shape[0]
    T = B * S
    xf = x.reshape(T, D)

    def w1_pair(w1):
        return jnp.concatenate([w1[:CMP_STRIDE], w1[CMP_STRIDE:]], axis=-1).astype(BF16)

    def pe_rows(pe):
        return jnp.broadcast_to(pe.reshape(1, CMP_LEN * HEAD_DIM), (8, CMP_LEN * HEAD_DIM)).astype(BF16)

    w_in_all = w_in.astype(BF16)
    w_gate_all = jnp.pad(w_in[:, :, COL_GATE:], ((0, 0), (0, 0), (0, LANE - (D_IN - COL_GATE)))).astype(BF16)
    w_out_all = w_out.astype(BF16)

    for l in range(depth):
        z, zgate = _inproj(xf, norm1_w[l].reshape(1, D), w_in_all, w_gate_all, l)

        yab = _mix_ab(z, pool_w[l].astype(BF16), pool_scale[l].reshape(1, POOL_WIDTH), conv_w[l], B, S)

        kc, vc = _compress(
            z, w1_pair(cmp_w1_k[l]), w1_pair(cmp_w1_v[l]), pe_rows(cmp_pe_k[l]), pe_rows(cmp_pe_v[l]),
            cmp_w1_k[l].reshape(CMP_LEN * HEAD_DIM, CMP_HIDDEN).astype(BF16),
            cmp_w1_v[l].reshape(CMP_LEN * HEAD_DIM, CMP_HIDDEN).astype(BF16),
            cmp_w2_k[l].astype(BF16), cmp_w2_v[l].astype(BF16), k_norm_w[l], B, S)
        yc = _nsa(z, zgate, kc, vc, q_norm_w[l], k_norm_w[l], B, S)

        wr = jnp.concatenate([router_exp_w[l], router_grp_w[l]], axis=1)
        wr = jnp.pad(wr, ((0, 0), (0, LANE - wr.shape[1])))
        wr_hi = wr.astype(BF16)
        wr_lo = (wr - wr_hi.astype(F32)).astype(BF16)
        br = jnp.concatenate([router_exp_b[l], router_grp_b[l]])
        br = jnp.pad(br, (0, LANE - br.shape[0])).reshape(1, LANE)
        x1, hnp, route, counts8 = _outproj_router(yab, yc, w_out_all, l, xf, norm2_w[l].reshape(1, D),
                                                  wr_hi, wr_lo, br)

        xf = _moe(x1, hnp, route, counts8, exp_w_gate, exp_w_up, exp_w_down, l)
    return xf.reshape(B, S, D)
```

```python
import functools

import numpy as np
import jax
import jax.numpy as jnp
from jax import lax
from jax.experimental import pallas as pl
from jax.experimental.pallas import tpu as pltpu

F32 = jnp.float32
BF16 = jnp.bfloat16

POOL_WINDOWS = (2, 4, 8, 16)
LANE = 128
POOL_WIDTH = 512
CONV_WIDTH = 512
CONV_K = 3
NSA_WIDTH = 1024
HEAD_DIM = 128
NSA_KV_HEADS = 2
NSA_GROUP = 4
N_BRANCH = 3
CMP_LEN = 32
CMP_STRIDE = 16
CMP_HIDDEN = 256
SEL_LEN = 64
N_SELECT = 16
SEL_FORCE = 1.0e4
WINDOW = 512
N_GROUPS_MOE = 4
EXPERTS_PER_GROUP = 8
N_EXPERTS = 32
D_EXPERT = 256
EPS = 1e-6
NEG = -1e30
LOG2_E = 1.4426950408889634

COL_U = 0
COL_B = 512
COL_C = 1024
COL_V = 1536
COL_Q = 2048
COL_KC = 3072
COL_KS = 3584
COL_VS = 3840
COL_KW = 4096
COL_VW = 4352
COL_GATE = 4608
D_IN = 4632

VMEM_LIMIT = 56 * 1024 * 1024


def _params(sem):
    return pltpu.CompilerParams(dimension_semantics=sem, vmem_limit_bytes=VMEM_LIMIT)


def _rms(x, w):
    return x * lax.rsqrt(jnp.mean(x * x, axis=-1, keepdims=True) + EPS) * w


def _silu(x):
    return x / (1.0 + jnp.exp(-x))


def _dot(a, b):
    return jnp.dot(a, b, preferred_element_type=F32)


def _split3(a):
    hi = a.astype(BF16)
    r1 = a - hi.astype(F32)
    mid = r1.astype(BF16)
    lo = (r1 - mid.astype(F32)).astype(BF16)
    return hi, mid, lo


def _inproj_kernel(x_ref, nw_ref, w_ref, wgate_ref, o_ref, gate_ref, xn_ref):
    @pl.when(pl.program_id(1) == 0)
    def _():
        xn = _rms(x_ref[...], nw_ref[...]).astype(BF16)
        xn_ref[...] = xn
        gate_ref[...] = _dot(xn, wgate_ref[0])

    o_ref[...] = _dot(xn_ref[...], w_ref[0]).astype(o_ref.dtype)


def _inproj(x2d, nw, w_all, wgate_all, layer, tm=1024, tn=1536):
    T, D = x2d.shape
    N = COL_GATE
    assert N % tn == 0 and w_all.shape[2] >= N
    return pl.pallas_call(
        _inproj_kernel,
        grid=(T // tm, N // tn),
        in_specs=[
            pl.BlockSpec((tm, D), lambda i, j: (i, 0)),
            pl.BlockSpec((1, D), lambda i, j: (0, 0)),
            pl.BlockSpec((1, D, tn), lambda i, j: (layer, 0, j)),
            pl.BlockSpec((1, D, LANE), lambda i, j: (layer, 0, 0)),
        ],
        out_specs=[pl.BlockSpec((tm, tn), lambda i, j: (i, j)),
                   pl.BlockSpec((tm, LANE), lambda i, j: (i, 0))],
        out_shape=[jax.ShapeDtypeStruct((T, N), BF16), jax.ShapeDtypeStruct((T, LANE), F32)],
        scratch_shapes=[pltpu.VMEM((tm, D), BF16)],
        compiler_params=_params(("parallel", "arbitrary")),
        name="inproj",
    )(x2d, nw, w_all, wgate_all)


def _mix_ab_kernel(u_ref, b_ref, c_ref, v_ref, pw_ref, ps_ref, cw_ref, o_ref):
    S = u_ref.shape[0]
    row = lax.broadcasted_iota(jnp.int32, (S, LANE), 0)

    def shift(a, k):
        return jnp.where(row >= k, pltpu.roll(a, k, axis=0), 0.0)

    for g, w in enumerate(POOL_WINDOWS):
        sl = slice(g * LANE, (g + 1) * LANE)
        u = u_ref[:, sl].astype(F32)
        s = u
        k = 1
        while k < w:
            s = s + shift(s, k)
            k *= 2
        cnt = jnp.minimum(row + 1, w).astype(F32)
        mixed = (s / cnt - u).astype(BF16)
        y = _dot(mixed, pw_ref[g]) * ps_ref[:, sl]
        o_ref[:, sl] = y.astype(o_ref.dtype)

    for h in range(CONV_WIDTH // LANE):
        sl = slice(h * LANE, (h + 1) * LANE)
        u2 = c_ref[:, sl].astype(F32) * v_ref[:, sl].astype(F32)
        y = cw_ref[0:1, sl] * shift(u2, 2)
        y = y + cw_ref[1:2, sl] * shift(u2, 1)
        y = y + cw_ref[2:3, sl] * u2
        y = b_ref[:, sl].astype(F32) * y
        o_ref[:, POOL_WIDTH + h * LANE:POOL_WIDTH + (h + 1) * LANE] = y.astype(o_ref.dtype)


def _mix_ab(z, pool_w_bf16, pool_scale, conv_w, B, S):
    T = z.shape[0]
    blk = lambda c: pl.BlockSpec((S, 512), lambda b, c=c: (b, c))
    return pl.pallas_call(
        _mix_ab_kernel,
        grid=(B,),
        in_specs=[
            blk(COL_U // 512), blk(COL_B // 512), blk(COL_C // 512), blk(COL_V // 512),
            pl.BlockSpec((4, LANE, LANE), lambda b: (0, 0, 0)),
            pl.BlockSpec((1, POOL_WIDTH), lambda b: (0, 0)),
            pl.BlockSpec((CONV_K, CONV_WIDTH), lambda b: (0, 0)),
        ],
        out_specs=pl.BlockSpec((S, POOL_WIDTH + CONV_WIDTH), lambda b: (b, 0)),
        out_shape=jax.ShapeDtypeStruct((T, POOL_WIDTH + CONV_WIDTH), BF16),
        compiler_params=_params(("parallel",)),
        name="mix_ab",
    )(z, z, z, z, pool_w_bf16, pool_scale, conv_w)


def _compress_kernel(z_ref, w1k_ref, w1v_ref, pek_ref, pev_ref, w1kf_ref, w1vf_ref,
                     w2k_ref, w2v_ref, knw_ref, kc_ref, vc_ref, xf_ref):
    assert CMP_LEN == 2 * CMP_STRIDE
    n16 = z_ref.shape[0] // CMP_STRIDE
    for cg in range(z_ref.shape[1] // HEAD_DIM):
        xf_ref[cg] = z_ref[:, cg * HEAD_DIM:(cg + 1) * HEAD_DIM].astype(F32)
    streams = ((w1k_ref, pek_ref, w1kf_ref, w2k_ref, kc_ref),
               (w1v_ref, pev_ref, w1vf_ref, w2v_ref, vc_ref))
    for which, (w1_ref, pe_ref, w1f_ref, w2_ref, out_ref) in enumerate(streams):
        pe_term = _dot(pe_ref[...], w1f_ref[...])[0:1, :]
        for h in range(NSA_KV_HEADS):
            cg = which * NSA_KV_HEADS + h
            acc = None
            for l in range(CMP_STRIDE):
                rows = xf_ref[cg, pl.ds(l, n16, stride=CMP_STRIDE), :]
                part = _dot(rows.astype(BF16), w1_ref[l])
                acc = part if acc is None else acc + part
            first = acc[:, :CMP_HIDDEN]
            second = acc[:, CMP_HIDDEN:]
            hid = first + pltpu.roll(second, n16 - 1, axis=0) + pe_term
            out = _dot(_silu(hid).astype(BF16), w2_ref[...])
            if which == 0:
                out = _rms(out, knw_ref[0:1, :])
            out_ref[0, h] = out.astype(out_ref.dtype)


def _compress(z, w1k, w1v, pek8, pev8, w1kf, w1vf, w2k, w2v, knw, B, S):
    n16 = S // CMP_STRIDE
    full = lambda a: pl.BlockSpec(a.shape, lambda b, nd=a.ndim: (0,) * nd)
    out_sds = jax.ShapeDtypeStruct((B, NSA_KV_HEADS, n16, HEAD_DIM), BF16)
    out_spec = pl.BlockSpec((1, NSA_KV_HEADS, n16, HEAD_DIM), lambda b: (b, 0, 0, 0))
    return pl.pallas_call(
        _compress_kernel,
        grid=(B,),
        in_specs=[pl.BlockSpec((S, 512), lambda b: (b, COL_KC // 512)),
                  full(w1k), full(w1v), full(pek8), full(pev8), full(w1kf), full(w1vf),
                  full(w2k), full(w2v), full(knw)],
        out_specs=[out_spec, out_spec],
        out_shape=[out_sds, out_sds],
        scratch_shapes=[pltpu.VMEM((512 // HEAD_DIM, S, HEAD_DIM), F32)],
        compiler_params=_params(("parallel",)),
        name="compress",
    )(z, w1k, w1v, pek8, pev8, w1kf, w1vf, w2k, w2v, knw)


V_ROWS = HEAD_DIM + 16

def _nsa_kernel(zq_ref, zg_ref, ks_ref, vs_ref, kw_ref, vw_ref, kc_ref, vc_ref,
                qnw_ref, knw_ref, ovlt_ref, negexp_ref, wband_ref, o_ref,
                kaug_ref, kwn_ref, vst_ref, vwt_ref, vct_ref, gt_ref, m_ref, acc_ref,
                ocmp_ref, owin_ref, qaug_ref, sbuf_ref, *, tq):
    G = NSA_GROUP
    S = ks_ref.shape[0]
    hkv = pl.program_id(1)
    i = pl.program_id(2)
    tk = tq
    t0 = i * tq

    def transpose_to_bf16(a):
        return a.astype(F32).T.astype(BF16)

    @pl.when(i == 0)
    def _():
        kaug_ref[:, 0:HEAD_DIM] = _rms(ks_ref[...].astype(F32), knw_ref[1:2, :]).astype(BF16)
        kaug_ref[:, HEAD_DIM:] = negexp_ref[...]
        kwn_ref[...] = _rms(kw_ref[...].astype(F32), knw_ref[2:3, :]).astype(BF16)
        ones_row = jnp.where(lax.broadcasted_iota(jnp.int32, (V_ROWS - HEAD_DIM, tk), 0) == 0, 1.0, 0.0)
        for j in range(S // tk):
            vst_ref[j, 0:HEAD_DIM, :] = transpose_to_bf16(vs_ref[j * tk:(j + 1) * tk, :])
            vwt_ref[j, 0:HEAD_DIM, :] = transpose_to_bf16(vw_ref[j * tk:(j + 1) * tk, :])
            vst_ref[j, HEAD_DIM:, :] = ones_row.astype(BF16)
            vwt_ref[j, HEAD_DIM:, :] = ones_row.astype(BF16)
        vct_ref[...] = transpose_to_bf16(vc_ref[0, 0])

    qts = []
    for g in range(G):
        qt = zq_ref[:, g * HEAD_DIM:(g + 1) * HEAD_DIM].astype(F32).T
        ms = jnp.mean(qt * qt, axis=0, keepdims=True)
        qt = qt * lax.rsqrt(ms + EPS) * qnw_ref[...] * (HEAD_DIM ** -0.5 * LOG2_E)
        qts.append(qt.astype(BF16))
    q4t = jnp.concatenate(qts, axis=1)
    per_head = tq // LANE
    n_ch = G * per_head
    cols = [slice(c * LANE, (c + 1) * LANE) for c in range(n_ch)]
    qcols = [slice((c % per_head) * LANE, (c % per_head + 1) * LANE) for c in range(n_ch)]

    wk = WINDOW + tq
    ws = pl.multiple_of(jnp.maximum(t0 - WINDOW, 0), tq)
    band = wband_ref.at[jnp.minimum(i, WINDOW // tq)]
    sw = _dot(kwn_ref[pl.ds(ws, wk), :], q4t)
    for c in range(n_ch):
        sg = sw[:, cols[c]] + band[:, qcols[c]]
        p = jnp.exp2(sg - jnp.max(sg, axis=0, keepdims=True)).astype(BF16)
        acc_w = None
        for j in range(wk // tk):
            part = _dot(vwt_ref[ws // tk + j], p[j * tk:(j + 1) * tk, :])
            acc_w = part if acc_w is None else acc_w + part
        owin_ref[:, cols[c]] = acc_w[0:HEAD_DIM] * (1.0 / acc_w[HEAD_DIM:HEAD_DIM + 1])

    sc = _dot(kc_ref[0, 0], q4t)
    n_sub = lax.broadcasted_iota(jnp.int32, (LANE, LANE), 0)
    pts = []
    psums = [None] * per_head
    for c in range(n_ch):
        t_lane = t0 + (c % per_head) * LANE + lax.broadcasted_iota(jnp.int32, (LANE, LANE), 1)
        cm = (n_sub * CMP_STRIDE + (CMP_LEN - 1)) <= t_lane
        sg = jnp.where(cm, sc[:, cols[c]], NEG)
        e = jnp.exp2(sg - jnp.max(sg, axis=0, keepdims=True))
        p = e * (1.0 / jnp.sum(e, axis=0, keepdims=True))
        p = jnp.where(cm, p, 0.0)
        pts.append(p.astype(BF16))
        k = c % per_head
        psums[k] = p if psums[k] is None else psums[k] + p
    ocmp_ref[...] = _dot(vct_ref[...], jnp.concatenate(pts, axis=1))

    hi, mid, lo = _split3(jnp.concatenate(psums, axis=1))
    imp = _dot(ovlt_ref[...], hi) + _dot(ovlt_ref[...], mid) + _dot(ovlt_ref[...], lo)
    n_sel = S // SEL_LEN
    n_top = min(N_SELECT, n_sel)
    j_sub = lax.broadcasted_iota(jnp.int32, (n_sel, tq), 0)
    t_sel = t0 + lax.broadcasted_iota(jnp.int32, (n_sel, tq), 1)
    forced = (j_sub == (t_sel // SEL_LEN)) | (j_sub == 0)
    valid = (j_sub * SEL_LEN) <= t_sel
    score = jnp.where(forced, SEL_FORCE, jnp.where(valid, imp[0:n_sel, :], -1.0))
    rank = jnp.zeros((n_sel, tq), F32)
    for c in range(n_sel):
        other = score[c:c + 1, :]
        beats = jnp.where(other > score, 1.0, jnp.where((other == score) & (j_sub > c), 1.0, 0.0))
        rank = rank + beats
    unsel = jnp.where(rank < n_top, 0.0, 1.0)
    unsel = jnp.concatenate([unsel, jnp.zeros((LANE - n_sel, tq), F32)], axis=0).astype(BF16)
    qaug_ref[0:HEAD_DIM, :] = q4t
    qaug_ref[HEAD_DIM:, :] = jnp.concatenate([unsel] * G, axis=1)

    m_ref[...] = jnp.full(m_ref.shape, NEG, F32)
    acc_ref[...] = jnp.zeros(acc_ref.shape, F32)
    k_sub = lax.broadcasted_iota(jnp.int32, (tk, LANE), 0)
    q_lane = lax.broadcasted_iota(jnp.int32, (tk, LANE), 1)

    def sel_scores(kt, slot):
        k0 = pl.multiple_of(kt * tk, tk)
        sbuf_ref[slot] = _dot(kaug_ref[pl.ds(k0, tk), :], qaug_ref[...])

    def sel_update(kt, slot, diagonal):
        vt = vst_ref[kt]
        scores = sbuf_ref.at[slot]
        for c in range(n_ch):
            sg = scores[:, cols[c]]
            if diagonal:
                sg = jnp.where(k_sub <= q_lane + (c % per_head) * LANE, sg, NEG)
            m_old = m_ref[:, cols[c]]
            m_new = jnp.maximum(m_old, jnp.max(sg, axis=0, keepdims=True))
            alpha = jnp.exp2(m_old - m_new)
            p = jnp.exp2(sg - m_new).astype(BF16)
            acc_ref[:, cols[c]] = alpha * acc_ref[:, cols[c]] + _dot(vt, p)
            m_ref[:, cols[c]] = m_new

    def sel_body(kt, carry):
        sel_update(kt, kt % 2, False)
        sel_scores(kt + 1, (kt + 1) % 2)
        return carry

    sel_scores(0, 0)
    lax.fori_loop(0, i, sel_body, 0)
    sel_update(i, i % 2, True)

    gt_ref[...] = (1.0 / (1.0 + jnp.exp(-zg_ref[...].astype(F32)))).T
    n_heads = NSA_KV_HEADS * G
    for c in range(n_ch):
        g = c // per_head
        col = hkv * G + g
        gate = lambda branch: gt_ref[pl.ds(branch * n_heads + col, 1), :][:, qcols[c]]
        o_sel = acc_ref[0:HEAD_DIM, cols[c]] * (1.0 / acc_ref[HEAD_DIM:HEAD_DIM + 1, cols[c]])
        out = gate(0) * ocmp_ref[:, cols[c]] + gate(1) * o_sel + gate(2) * owin_ref[:, cols[c]]
        o_ref[qcols[c], g * HEAD_DIM:(g + 1) * HEAD_DIM] = out.T.astype(o_ref.dtype)


def _nsa(z, zgate, kc, vc, qnw, knw, B, S, tq=256):
    T = z.shape[0]
    nq = S // tq
    G = NSA_GROUP
    n_c = (S - CMP_LEN) // CMP_STRIDE + 1
    n_sel = S // SEL_LEN
    assert S % tq == 0 and n_sel <= LANE and n_sel % 8 == 0 and n_c <= LANE and WINDOW % tq == 0
    assert S >= WINDOW + tq
    ci = np.arange(LANE)[None, :] * CMP_STRIDE
    sj = np.arange(LANE)[:, None] * SEL_LEN
    ovlt = ((ci < sj + SEL_LEN) & (ci + CMP_LEN > sj) & (np.arange(LANE)[None, :] < n_c)
            & (np.arange(LANE)[:, None] < n_sel))
    ovlt = jnp.asarray(ovlt.astype(np.float32), BF16)
    in_block = (np.arange(S)[:, None] // SEL_LEN) == np.arange(LANE)[None, :]
    negexp = jnp.asarray(in_block.astype(np.float32) * NEG, BF16)
    qnw_b = jnp.broadcast_to(qnw.reshape(HEAD_DIM, 1), (HEAD_DIM, tq))
    bands = []
    for p in range(WINDOW // tq + 1):
        t0, ws = p * tq, max(p * tq - WINDOW, 0)
        dist = (t0 + np.arange(tq)[None, :]) - (ws + np.arange(WINDOW + tq)[:, None])
        bands.append(np.where((dist >= 0) & (dist < WINDOW), 0.0, NEG))
    wband = jnp.asarray(np.stack(bands), F32)

    kvblk = lambda c: pl.BlockSpec((S, HEAD_DIM), lambda b, h, i, c=c: (b, c + h))
    cblk = pl.BlockSpec((1, 1, LANE, HEAD_DIM), lambda b, h, i: (b, h, 0, 0))
    full = lambda a: pl.BlockSpec(a.shape, lambda b, h, i, nd=a.ndim: (0,) * nd)
    return pl.pallas_call(
        functools.partial(_nsa_kernel, tq=tq),
        grid=(B, NSA_KV_HEADS, nq),
        in_specs=[
            pl.BlockSpec((tq, G * HEAD_DIM), lambda b, h, i: (b * nq + i, COL_Q // 512 + h)),
            pl.BlockSpec((tq, LANE), lambda b, h, i: (b * nq + i, 0)),
            kvblk(COL_KS // LANE), kvblk(COL_VS // LANE), kvblk(COL_KW // LANE), kvblk(COL_VW // LANE),
            cblk, cblk, full(qnw_b), full(knw), full(ovlt), full(negexp), full(wband),
        ],
        out_specs=pl.BlockSpec((tq, G * HEAD_DIM), lambda b, h, i: (b * nq + i, h)),
        out_shape=jax.ShapeDtypeStruct((T, NSA_WIDTH), BF16),
        scratch_shapes=[
            pltpu.VMEM((S, 2 * HEAD_DIM), BF16), pltpu.VMEM((S, HEAD_DIM), BF16),
            pltpu.VMEM((S // tq, V_ROWS, tq), BF16), pltpu.VMEM((S // tq, V_ROWS, tq), BF16),
            pltpu.VMEM((HEAD_DIM, LANE), BF16), pltpu.VMEM((LANE, tq), F32),
            pltpu.VMEM((1, G * tq), F32), pltpu.VMEM((V_ROWS, G * tq), F32),
            pltpu.VMEM((HEAD_DIM, G * tq), F32), pltpu.VMEM((HEAD_DIM, G * tq), F32),
            pltpu.VMEM((2 * HEAD_DIM, G * tq), BF16), pltpu.VMEM((2, tq, G * tq), F32),
        ],
        compiler_params=_params(("parallel", "parallel", "arbitrary")),
        name="nsa",
    )(z, zgate, z, z, z, z, kc, vc, qnw_b, knw, ovlt, negexp, wband)


def _pack_rows(x):
    w = x.shape[1] // 2
    return pltpu.pack_elementwise([x[:, :w], x[:, w:]], packed_dtype=BF16)


def _unpack_rows(p, dtype):
    lo = pltpu.unpack_elementwise(p, index=0, packed_dtype=BF16, unpacked_dtype=F32)
    hi = pltpu.unpack_elementwise(p, index=1, packed_dtype=BF16, unpacked_dtype=F32)
    return jnp.concatenate([lo.astype(dtype), hi.astype(dtype)], axis=1)


ROUTE_E = 0
ROUTE_W = 2
ROUTE_RANK = 4


def _outproj_router_kernel(yab_ref, yc_ref, wo_ref, x_ref, nw_ref, wrh_ref, wrl_ref, br_ref,
                           x1_ref, hnp_ref, route_ref, counts_ref, cnt_ref):
    @pl.when(pl.program_id(0) == 0)
    def _():
        cnt_ref[...] = jnp.zeros_like(cnt_ref)

    ka = yab_ref.shape[1]
    acc = _dot(yab_ref[...], wo_ref[0, 0:ka, :]) + _dot(yc_ref[...], wo_ref[0, ka:, :])
    x1 = x_ref[...] + acc
    x1_ref[...] = x1
    hn = _rms(x1, nw_ref[...])
    hnp_ref[...] = _pack_rows(hn)
    hi = hn.astype(BF16)
    lo = (hn - hi.astype(F32)).astype(BF16)
    lg = _dot(hi, wrh_ref[...]) + _dot(lo, wrh_ref[...]) + _dot(hi, wrl_ref[...]) + br_ref[...]

    tm = lg.shape[0]
    lane = lax.broadcasted_iota(jnp.int32, (tm, LANE), 1)
    lane_f = lane.astype(F32)
    big = float(LANE)

    def first_max(v):
        m = jnp.max(v, axis=1, keepdims=True)
        idx = jnp.min(jnp.where(v == m, lane_f, big), axis=1, keepdims=True)
        return m, idx

    is_grp = (lane >= N_EXPERTS) & (lane < N_EXPERTS + N_GROUPS_MOE)
    lgm = jnp.where(is_grp, lg, NEG)
    mg, grp_lane = first_max(lgm)
    p_grp = 1.0 / jnp.sum(jnp.where(is_grp, jnp.exp(lgm - mg), 0.0), axis=1, keepdims=True)
    grp = grp_lane - float(N_EXPERTS)
    in_grp = (lane < N_EXPERTS) & ((lane // EXPERTS_PER_GROUP).astype(F32) == grp)
    le = jnp.where(in_grp, lg, NEG)
    m1, i1 = first_max(le)
    le2 = jnp.where(lane_f == i1, NEG, le)
    m2, i2 = first_max(le2)
    e2 = jnp.exp(m2 - m1)
    den = 1.0 + e2
    w1 = p_grp * (1.0 / den)
    w2 = p_grp * (e2 / den)

    onehot = jnp.where((lane_f == i1) | (lane_f == i2), 1.0, 0.0)
    r_i = lax.broadcasted_iota(jnp.int32, (tm, tm), 0)
    c_i = lax.broadcasted_iota(jnp.int32, (tm, tm), 1)
    before = jnp.where(c_i < r_i, 1.0, 0.0).astype(BF16)
    base = cnt_ref[0:1, :] + _dot(before, onehot.astype(BF16))
    r1 = jnp.sum(jnp.where(lane_f == i1, base, 0.0), axis=1, keepdims=True)
    r2 = jnp.sum(jnp.where(lane_f == i2, base, 0.0), axis=1, keepdims=True)
    cnt_ref[0:1, :] = cnt_ref[0:1, :] + jnp.sum(onehot, axis=0, keepdims=True)
    counts_ref[...] = jnp.broadcast_to(cnt_ref[0:1, :], counts_ref.shape)

    route = jnp.zeros((tm, LANE), F32)
    for k, v in enumerate((i1, i2, w1, w2, r1, r2)):
        route = jnp.where(lane == k, v, route)
    route_ref[...] = route


def _outproj_router(yab, yc, wo_all, layer, x2d, nw, wr_hi, wr_lo, br, tm=512):
    T, D = x2d.shape
    full = lambda a: pl.BlockSpec(a.shape, lambda i, nd=a.ndim: (0,) * nd)
    return pl.pallas_call(
        _outproj_router_kernel,
        grid=(T // tm,),
        in_specs=[
            pl.BlockSpec((tm, yab.shape[1]), lambda i: (i, 0)),
            pl.BlockSpec((tm, yc.shape[1]), lambda i: (i, 0)),
            pl.BlockSpec((1,) + wo_all.shape[1:], lambda i: (layer, 0, 0)),
            pl.BlockSpec((tm, D), lambda i: (i, 0)),
            full(nw), full(wr_hi), full(wr_lo), full(br),
        ],
        out_specs=[pl.BlockSpec((tm, D), lambda i: (i, 0)),
                   pl.BlockSpec((tm, D // 2), lambda i: (i, 0)),
                   pl.BlockSpec((tm, LANE), lambda i: (i, 0)),
                   pl.BlockSpec((8, LANE), lambda i: (0, 0))],
        out_shape=[jax.ShapeDtypeStruct((T, D), F32),
                   jax.ShapeDtypeStruct((T, D // 2), jnp.uint32),
                   jax.ShapeDtypeStruct((T, LANE), F32),
                   jax.ShapeDtypeStruct((8, LANE), F32)],
        scratch_shapes=[pltpu.VMEM((8, LANE), F32)],
        compiler_params=_params(("arbitrary",)),
        name="outproj_router",
    )(yab, yc, wo_all, x2d, nw, wr_hi, wr_lo, br)


def _row_copy(src_ref, src_row, dst_ref, dst_row, sem):
    return pltpu.make_async_copy(src_ref.at[pl.ds(src_row, 1), :], dst_ref.at[pl.ds(dst_row, 1), :], sem)


def _dispatch_kernel(dest_ref, hnp_ref, xs_hbm, sem, *, chunk):
    i = pl.program_id(0)

    def body(j, carry):
        t = i * chunk + j
        _row_copy(hnp_ref, j, xs_hbm, dest_ref[2 * t], sem).start()
        _row_copy(hnp_ref, j, xs_hbm, dest_ref[2 * t + 1], sem).start()
        return carry

    lax.fori_loop(0, chunk, body, 0, unroll=8)
    for _ in range(2):
        pltpu.make_async_copy(hnp_ref, xs_hbm.at[pl.ds(0, chunk), :], sem).wait()


def _dispatch(dest, hnp, chunk=2048):
    T, W = hnp.shape
    return pl.pallas_call(
        functools.partial(_dispatch_kernel, chunk=chunk),
        grid_spec=pltpu.PrefetchScalarGridSpec(
            num_scalar_prefetch=1,
            grid=(T // chunk,),
            in_specs=[pl.BlockSpec((chunk, W), lambda i, d: (i, 0))],
            out_specs=pl.BlockSpec(memory_space=pl.ANY),
            scratch_shapes=[pltpu.SemaphoreType.DMA(())],
        ),
        out_shape=jax.ShapeDtypeStruct((2 * T, W), hnp.dtype),
        compiler_params=_params(("arbitrary",)),
        name="dispatch",
    )(dest, hnp)


FLAG_FIRST = 1
FLAG_LAST = 2
FLAG_NEW_EXPERT = 4
FLAG_SLOT = 8
FLAG_NEXT = 16


def _experts_kernel(tile_ref, exp_ref, lo_ref, hi_ref, flag_ref, xs_ref, wg_hbm, wu_hbm, wd_hbm,
                    ys_ref, acc_ref, wgb_ref, wub_ref, wdb_ref, wgf_ref, wuf_ref, wdf_ref, sem, *, tm, layer):
    w = pl.program_id(0)
    lo = lo_ref[w]
    hi = hi_ref[w]
    flags = flag_ref[w]
    slot = (flags // FLAG_SLOT) & 1
    next_e = flags // FLAG_NEXT - 1

    def weight_copies(expert, s):
        return [pltpu.make_async_copy(src.at[layer, expert], dst.at[s], sem.at[s])
                for src, dst in ((wg_hbm, wgf_ref), (wu_hbm, wuf_ref), (wd_hbm, wdf_ref))]

    @pl.when(w == 0)
    def _():
        for c in weight_copies(exp_ref[0], slot):
            c.start()

    @pl.when((flags & FLAG_NEW_EXPERT) != 0)
    def _():
        for c in weight_copies(exp_ref[w], slot):
            c.wait()
        wgb_ref[...] = wgf_ref[slot].astype(BF16)
        wub_ref[...] = wuf_ref[slot].astype(BF16)
        wdb_ref[...] = wdf_ref[slot].astype(BF16)

        @pl.when(next_e >= 0)
        def _():
            for c in weight_copies(next_e, 1 - slot):
                c.start()

    first = (flags & FLAG_FIRST) != 0

    @pl.when(hi > lo)
    def _():
        x = _unpack_rows(xs_ref[...], BF16)
        hg = _dot(x, wgb_ref[...])
        hu = _dot(x, wub_ref[...])
        row = tile_ref[w] * tm + lax.broadcasted_iota(jnp.int32, hg.shape, 0)
        h = jnp.where((row >= lo) & (row < hi), _silu(hg) * hu, 0.0).astype(BF16)

        @pl.when(first)
        def _():
            acc_ref[...] = _dot(h, wdb_ref[...])

        @pl.when(jnp.logical_not(first))
        def _():
            acc_ref[...] += _dot(h, wdb_ref[...])

    @pl.when((flags & FLAG_LAST) != 0)
    def _():
        ys_ref[...] = _pack_rows(acc_ref[...])


def _experts(meta, xs, wg_all, wu_all, wd_all, layer, tm):
    N, W = xs.shape
    _, E, D, F = wg_all.shape
    tile_w, exp_w, lo_w, hi_w, flag_w = meta
    n_work = tile_w.shape[0]
    hbm = pl.BlockSpec(memory_space=pl.ANY)
    return pl.pallas_call(
        functools.partial(_experts_kernel, tm=tm, layer=layer),
        grid_spec=pltpu.PrefetchScalarGridSpec(
            num_scalar_prefetch=5,
            grid=(n_work,),
            in_specs=[
                pl.BlockSpec((tm, W), lambda w, t, e, lo, hi, f: (t[w], 0)),
                hbm, hbm, hbm,
            ],
            out_specs=pl.BlockSpec((tm, W), lambda w, t, e, lo, hi, f: (t[w], 0)),
            scratch_shapes=[pltpu.VMEM((tm, D), F32), pltpu.VMEM((D, F), BF16),
                            pltpu.VMEM((D, F), BF16), pltpu.VMEM((F, D), BF16),
                            pltpu.VMEM((2, D, F), F32), pltpu.VMEM((2, D, F), F32),
                            pltpu.VMEM((2, F, D), F32), pltpu.SemaphoreType.DMA((2,))],
        ),
        out_shape=jax.ShapeDtypeStruct((N, W), xs.dtype),
        compiler_params=_params(("arbitrary",)),
        name="experts",
    )(tile_w, exp_w, lo_w, hi_w, flag_w, xs, wg_all, wu_all, wd_all)


def _work_items(counts, n_rows, tm):
    E = counts.shape[0]
    n_tiles = n_rows // tm
    n_work = n_tiles + E - 1
    start = jnp.cumsum(counts) - counts
    end = start + counts
    first_tile = start // tm
    last_tile = jnp.maximum(end - 1, 0) // tm
    n_e = jnp.where(counts > 0, last_tile - first_tile + 1, 0)
    wend = jnp.cumsum(n_e)
    wstart = wend - n_e
    total = wend[-1]
    w = jnp.arange(n_work, dtype=jnp.int32)
    wc = jnp.minimum(w, total - 1)
    ew = jnp.sum((wc[:, None] >= wend[None, :]).astype(jnp.int32), axis=1)
    tile_w = first_tile[ew] + (wc - wstart[ew])
    valid = w < total
    lo = jnp.where(valid, jnp.maximum(start[ew], tile_w * tm), 0)
    hi = jnp.where(valid, jnp.minimum(end[ew], (tile_w + 1) * tm), 0)
    prev_tile = jnp.concatenate([jnp.full((1,), -1, jnp.int32), tile_w[:-1]])
    next_tile = jnp.concatenate([tile_w[1:], jnp.full((1,), -1, jnp.int32)])
    prev_e = jnp.concatenate([jnp.full((1,), -1, jnp.int32), ew[:-1]])
    first = valid & (tile_w != prev_tile)
    last = valid & ((tile_w != next_tile) | (w == total - 1))
    new_e = ew != prev_e
    present = n_e > 0
    slot_e = (jnp.cumsum(present.astype(jnp.int32)) - 1) % 2
    ids = jnp.where(present, jnp.arange(E, dtype=jnp.int32), E)
    at_or_after = jnp.flip(lax.cummin(jnp.flip(ids)))
    next_e = jnp.concatenate([at_or_after[1:], jnp.full((1,), E, jnp.int32)])
    next_e = jnp.where(next_e < E, next_e, -1)
    i32 = lambda a: a.astype(jnp.int32)
    flags = (FLAG_FIRST * i32(first) + FLAG_LAST * i32(last) + FLAG_NEW_EXPERT * i32(new_e)
             + FLAG_SLOT * slot_e[ew] + FLAG_NEXT * (next_e[ew] + 1))
    return i32(tile_w), i32(ew), i32(lo), i32(hi), i32(flags)


def _combine_kernel(dest_ref, x1_ref, route_ref, ys_hbm, o_ref, buf_ref, sem, *, tt):
    i = pl.program_id(0)
    n = pl.num_programs(0)

    def issue(step, slot):
        def body(j, carry):
            t = step * tt + j
            _row_copy(ys_hbm, dest_ref[2 * t], buf_ref.at[slot, 0], j, sem.at[slot]).start()
            _row_copy(ys_hbm, dest_ref[2 * t + 1], buf_ref.at[slot, 1], j, sem.at[slot]).start()
            return carry
        lax.fori_loop(0, tt, body, 0, unroll=8)

    @pl.when(i == 0)
    def _():
        issue(0, 0)

    @pl.when(i + 1 < n)
    def _():
        issue(i + 1, (i + 1) % 2)

    slot = i % 2
    for k in range(2):
        pltpu.make_async_copy(ys_hbm.at[pl.ds(0, tt), :], buf_ref.at[slot, k], sem.at[slot]).wait()

    lane = lax.broadcasted_iota(jnp.int32, route_ref.shape, 1)
    route = route_ref[...]
    w0 = jnp.sum(jnp.where(lane == ROUTE_W, route, 0.0), axis=1, keepdims=True)
    w1 = jnp.sum(jnp.where(lane == ROUTE_W + 1, route, 0.0), axis=1, keepdims=True)
    y0 = _unpack_rows(buf_ref[slot, 0], F32)
    y1 = _unpack_rows(buf_ref[slot, 1], F32)
    o_ref[...] = x1_ref[...] + (w0 * y0 + w1 * y1)


def _combine(dest, x1, route, ys, tt=256):
    T, D = x1.shape
    W = ys.shape[1]
    return pl.pallas_call(
        functools.partial(_combine_kernel, tt=tt),
        grid_spec=pltpu.PrefetchScalarGridSpec(
            num_scalar_prefetch=1,
            grid=(T // tt,),
            in_specs=[
                pl.BlockSpec((tt, D), lambda i, d: (i, 0)),
                pl.BlockSpec((tt, LANE), lambda i, d: (i, 0)),
                pl.BlockSpec(memory_space=pl.ANY),
            ],
            out_specs=pl.BlockSpec((tt, D), lambda i, d: (i, 0)),
            scratch_shapes=[pltpu.VMEM((2, 2, tt, W), ys.dtype), pltpu.SemaphoreType.DMA((2,))],
        ),
        out_shape=jax.ShapeDtypeStruct((T, D), F32),
        compiler_params=_params(("arbitrary",)),
        name="combine",
    )(dest, x1, route, ys)


def _moe(x1, hnp, route, counts8, wg_all, wu_all, wd_all, layer, tm=512):
    T = x1.shape[0]
    E = wg_all.shape[1]
    counts = counts8[0, :E].astype(jnp.int32)
    eid = route[:, ROUTE_E:ROUTE_E + 2].astype(jnp.int32)
    rank = route[:, ROUTE_RANK:ROUTE_RANK + 2].astype(jnp.int32)
    start = jnp.cumsum(counts) - counts
    onehot = eid[..., None] == jnp.arange(E, dtype=jnp.int32)
    dest = (jnp.sum(jnp.where(onehot, start, 0), axis=-1) + rank).reshape(2 * T)
    xs = _dispatch(dest, hnp)
    ys = _experts(_work_items(counts, 2 * T, tm), xs, wg_all, wu_all, wd_all, layer, tm)
    return _combine(dest, x1, route, ys)


def kernel(x, norm1_w, w_in, pool_w, pool_scale, conv_w, cmp_pe_k, cmp_w1_k, cmp_w2_k, cmp_pe_v, cmp_w1_v, cmp_w2_v, q_norm_w, k_norm_w, w_out, norm2_w, router_grp_w, router_grp_b, router_exp_w, router_exp_b, exp_w_gate, exp_w_up, exp_w_down):
    B, S, D = x.shape
    depth = w_in.shape[0]
    T = B * S
    xf = x.reshape(T, D)

    def w1_pair(w1):
        return jnp.concatenate([w1[:CMP_STRIDE], w1[CMP_STRIDE:]], axis=-1).astype(BF16)

    def pe_rows(pe):
        return jnp.broadcast_to(pe.reshape(1, CMP_LEN * HEAD_DIM), (8, CMP_LEN * HEAD_DIM)).astype(BF16)

    w_in_all = w_in.astype(BF16)
    w_gate_all = jnp.pad(w_in[:, :, COL_GATE:], ((0, 0), (0, 0), (0, LANE - (D_IN - COL_GATE)))).astype(BF16)
    w_out_all = w_out.astype(BF16)

    for l in range(depth):
        z, zgate = _inproj(xf, norm1_w[l].reshape(1, D), w_in_all, w_gate_all, l)

        yab = _mix_ab(z, pool_w[l].astype(BF16), pool_scale[l].reshape(1, POOL_WIDTH), conv_w[l], B, S)

        kc, vc = _compress(
            z, w1_pair(cmp_w1_k[l]), w1_pair(cmp_w1_v[l]), pe_rows(cmp_pe_k[l]), pe_rows(cmp_pe_v[l]),
            cmp_w1_k[l].reshape(CMP_LEN * HEAD_DIM, CMP_HIDDEN).astype(BF16),
            cmp_w1_v[l].reshape(CMP_LEN * HEAD_DIM, CMP_HIDDEN).astype(BF16),
            cmp_w2_k[l].astype(BF16), cmp_w2_v[l].astype(BF16), k_norm_w[l], B, S)
        yc = _nsa(z, zgate, kc, vc, q_norm_w[l], k_norm_w[l], B, S)

        wr = jnp.concatenate([router_exp_w[l], router_grp_w[l]], axis=1)
        wr = jnp.pad(wr, ((0, 0), (0, LANE - wr.shape[1])))
        wr_hi = wr.astype(BF16)
        wr_lo = (wr - wr_hi.astype(F32)).astype(BF16)
        br = jnp.concatenate([router_exp_b[l], router_grp_b[l]])
        br = jnp.pad(br, (0, LANE - br.shape[0])).reshape(1, LANE)
        x1, hnp, route, counts8 = _outproj_router(yab, yc, w_out_all, l, xf, norm2_w[l].reshape(1, D),
                                                  wr_hi, wr_lo, br)

        xf = _moe(x1, hnp, route, counts8, exp_w_gate, exp_w_up, exp_w_down, l)
    return xf.reshape(B, S, D)
```

```python
import functools

import numpy as np
import jax
import jax.numpy as jnp
from jax import lax
from jax.experimental import pallas as pl
from jax.experimental.pallas import tpu as pltpu

F32 = jnp.float32
BF16 = jnp.bfloat16

POOL_WINDOWS = (2, 4, 8, 16)
LANE = 128
POOL_WIDTH = 512
CONV_WIDTH = 512
CONV_K = 3
NSA_WIDTH = 1024
HEAD_DIM = 128
NSA_KV_HEADS = 2
NSA_GROUP = 4
N_BRANCH = 3
CMP_LEN = 32
CMP_STRIDE = 16
CMP_HIDDEN = 256
SEL_LEN = 64
N_SELECT = 16
SEL_FORCE = 1.0e4
WINDOW = 512
N_GROUPS_MOE = 4
EXPERTS_PER_GROUP = 8
N_EXPERTS = 32
D_EXPERT = 256
EPS = 1e-6
NEG = -1e30
LOG2_E = 1.4426950408889634

COL_U = 0
COL_B = 512
COL_C = 1024
COL_V = 1536
COL_Q = 2048
COL_KC = 3072
COL_KS = 3584
COL_VS = 3840
COL_KW = 4096
COL_VW = 4352
COL_GATE = 4608
D_IN = 4632

VMEM_LIMIT = 56 * 1024 * 1024


def _params(sem):
    return pltpu.CompilerParams(dimension_semantics=sem, vmem_limit_bytes=VMEM_LIMIT)


def _rms(x, w):
    return x * lax.rsqrt(jnp.mean(x * x, axis=-1, keepdims=True) + EPS) * w


def _silu(x):
    return x / (1.0 + jnp.exp(-x))


def _dot(a, b):
    return jnp.dot(a, b, preferred_element_type=F32)


def _split3(a):
    hi = a.astype(BF16)
    r1 = a - hi.astype(F32)
    mid = r1.astype(BF16)
    lo = (r1 - mid.astype(F32)).astype(BF16)
    return hi, mid, lo


def _inproj_kernel(x_ref, nw_ref, w_ref, wgate_ref, o_ref, gate_ref, xn_ref):
    @pl.when(pl.program_id(1) == 0)
    def _():
        xn = _rms(x_ref[...], nw_ref[...]).astype(BF16)
        xn_ref[...] = xn
        gate_ref[...] = _dot(xn, wgate_ref[0])

    o_ref[...] = _dot(xn_ref[...], w_ref[0]).astype(o_ref.dtype)


def _inproj(x2d, nw, w_all, wgate_all, layer, tm=1024, tn=1536):
    T, D = x2d.shape
    N = COL_GATE
    assert N % tn == 0 and w_all.shape[2] >= N
    return pl.pallas_call(
        _inproj_kernel,
        grid=(T // tm, N // tn),
        in_specs=[
            pl.BlockSpec((tm, D), lambda i, j: (i, 0)),
            pl.BlockSpec((1, D), lambda i, j: (0, 0)),
            pl.BlockSpec((1, D, tn), lambda i, j: (layer, 0, j)),
            pl.BlockSpec((1, D, LANE), lambda i, j: (layer, 0, 0)),
        ],
        out_specs=[pl.BlockSpec((tm, tn), lambda i, j: (i, j)),
                   pl.BlockSpec((tm, LANE), lambda i, j: (i, 0))],
        out_shape=[jax.ShapeDtypeStruct((T, N), BF16), jax.ShapeDtypeStruct((T, LANE), F32)],
        scratch_shapes=[pltpu.VMEM((tm, D), BF16)],
        compiler_params=_params(("parallel", "arbitrary")),
        name="inproj",
    )(x2d, nw, w_all, wgate_all)


def _mix_ab_kernel(u_ref, b_ref, c_ref, v_ref, pw_ref, ps_ref, cw_ref, o_ref):
    S = u_ref.shape[0]
    row = lax.broadcasted_iota(jnp.int32, (S, LANE), 0)

    def shift(a, k):
        return jnp.where(row >= k, pltpu.roll(a, k, axis=0), 0.0)

    for g, w in enumerate(POOL_WINDOWS):
        sl = slice(g * LANE, (g + 1) * LANE)
        u = u_ref[:, sl].astype(F32)
        s = u
        k = 1
        while k < w:
            s = s + shift(s, k)
            k *= 2
        cnt = jnp.minimum(row + 1, w).astype(F32)
        mixed = (s / cnt - u).astype(BF16)
        y = _dot(mixed, pw_ref[g]) * ps_ref[:, sl]
        o_ref[:, sl] = y.astype(o_ref.dtype)

    for h in range(CONV_WIDTH // LANE):
        sl = slice(h * LANE, (h + 1) * LANE)
        u2 = c_ref[:, sl].astype(F32) * v_ref[:, sl].astype(F32)
        y = cw_ref[0:1, sl] * shift(u2, 2)
        y = y + cw_ref[1:2, sl] * shift(u2, 1)
        y = y + cw_ref[2:3, sl] * u2
        y = b_ref[:, sl].astype(F32) * y
        o_ref[:, POOL_WIDTH + h * LANE:POOL_WIDTH + (h + 1) * LANE] = y.astype(o_ref.dtype)


def _mix_ab(z, pool_w_bf16, pool_scale, conv_w, B, S):
    T = z.shape[0]
    blk = lambda c: pl.BlockSpec((S, 512), lambda b, c=c: (b, c))
    return pl.pallas_call(
        _mix_ab_kernel,
        grid=(B,),
        in_specs=[
            blk(COL_U // 512), blk(COL_B // 512), blk(COL_C // 512), blk(COL_V // 512),
            pl.BlockSpec((4, LANE, LANE), lambda b: (0, 0, 0)),
            pl.BlockSpec((1, POOL_WIDTH), lambda b: (0, 0)),
            pl.BlockSpec((CONV_K, CONV_WIDTH), lambda b: (0, 0)),
        ],
        out_specs=pl.BlockSpec((S, POOL_WIDTH + CONV_WIDTH), lambda b: (b, 0)),
        out_shape=jax.ShapeDtypeStruct((T, POOL_WIDTH + CONV_WIDTH), BF16),
        compiler_params=_params(("parallel",)),
        name="mix_ab",
    )(z, z, z, z, pool_w_bf16, pool_scale, conv_w)


def _compress_kernel(z_ref, w1k_ref, w1v_ref, pek_ref, pev_ref, w1kf_ref, w1vf_ref,
                     w2k_ref, w2v_ref, knw_ref, kc_ref, vc_ref, xf_ref):
    assert CMP_LEN == 2 * CMP_STRIDE
    n16 = z_ref.shape[0] // CMP_STRIDE
    for cg in range(z_ref.shape[1] // HEAD_DIM):
        xf_ref[cg] = z_ref[:, cg * HEAD_DIM:(cg + 1) * HEAD_DIM].astype(F32)
    streams = ((w1k_ref, pek_ref, w1kf_ref, w2k_ref, kc_ref),
               (w1v_ref, pev_ref, w1vf_ref, w2v_ref, vc_ref))
    for which, (w1_ref, pe_ref, w1f_ref, w2_ref, out_ref) in enumerate(streams):
        pe_term = _dot(pe_ref[...], w1f_ref[...])[0:1, :]
        for h in range(NSA_KV_HEADS):
            cg = which * NSA_KV_HEADS + h
            acc = None
            for l in range(CMP_STRIDE):
                rows = xf_ref[cg, pl.ds(l, n16, stride=CMP_STRIDE), :]
                part = _dot(rows.astype(BF16), w1_ref[l])
                acc = part if acc is None else acc + part
            first = acc[:, :CMP_HIDDEN]
            second = acc[:, CMP_HIDDEN:]
            hid = first + pltpu.roll(second, n16 - 1, axis=0) + pe_term
            out = _dot(_silu(hid).astype(BF16), w2_ref[...])
            if which == 0:
                out = _rms(out, knw_ref[0:1, :])
            out_ref[0, h] = out.astype(out_ref.dtype)


def _compress(z, w1k, w1v, pek8, pev8, w1kf, w1vf, w2k, w2v, knw, B, S):
    n16 = S // CMP_STRIDE
    full = lambda a: pl.BlockSpec(a.shape, lambda b, nd=a.ndim: (0,) * nd)
    out_sds = jax.ShapeDtypeStruct((B, NSA_KV_HEADS, n16, HEAD_DIM), BF16)
    out_spec = pl.BlockSpec((1, NSA_KV_HEADS, n16, HEAD_DIM), lambda b: (b, 0, 0, 0))
    return pl.pallas_call(
        _compress_kernel,
        grid=(B,),
        in_specs=[pl.BlockSpec((S, 512), lambda b: (b, COL_KC // 512)),
                  full(w1k), full(w1v), full(pek8), full(pev8), full(w1kf), full(w1vf),
                  full(w2k), full(w2v), full(knw)],
        out_specs=[out_spec, out_spec],
        out_shape=[out_sds, out_sds],
        scratch_shapes=[pltpu.VMEM((512 // HEAD_DIM, S, HEAD_DIM), F32)],
        compiler_params=_params(("parallel",)),
        name="compress",
    )(z, w1k, w1v, pek8, pev8, w1kf, w1vf, w2k, w2v, knw)


V_ROWS = HEAD_DIM + 16

def _nsa_kernel(zq_ref, zg_ref, ks_ref, vs_ref, kw_ref, vw_ref, kc_ref, vc_ref,
                qnw_ref, knw_ref, ovlt_ref, negexp_ref, wband_ref, o_ref,
                kaug_ref, kwn_ref, vst_ref, vwt_ref, vct_ref, gt_ref, m_ref, acc_ref,
                ocmp_ref, owin_ref, qaug_ref, sbuf_ref, *, tq):
    G = NSA_GROUP
    S = ks_ref.shape[0]
    hkv = pl.program_id(1)
    i = pl.program_id(2)
    tk = tq
    t0 = i * tq

    def transpose_to_bf16(a):
        return a.astype(F32).T.astype(BF16)

    @pl.when(i == 0)
    def _():
        kaug_ref[:, 0:HEAD_DIM] = _rms(ks_ref[...].astype(F32), knw_ref[1:2, :]).astype(BF16)
        kaug_ref[:, HEAD_DIM:] = negexp_ref[...]
        kwn_ref[...] = _rms(kw_ref[...].astype(F32), knw_ref[2:3, :]).astype(BF16)
        ones_row = jnp.where(lax.broadcasted_iota(jnp.int32, (V_ROWS - HEAD_DIM, tk), 0) == 0, 1.0, 0.0)
        for j in range(S // tk):
            vst_ref[j, 0:HEAD_DIM, :] = transpose_to_bf16(vs_ref[j * tk:(j + 1) * tk, :])
            vwt_ref[j, 0:HEAD_DIM, :] = transpose_to_bf16(vw_ref[j * tk:(j + 1) * tk, :])
            vst_ref[j, HEAD_DIM:, :] = ones_row.astype(BF16)
            vwt_ref[j, HEAD_DIM:, :] = ones_row.astype(BF16)
        vct_ref[...] = transpose_to_bf16(vc_ref[0, 0])

    qts = []
    for g in range(G):
        qt = zq_ref[:, g * HEAD_DIM:(g + 1) * HEAD_DIM].astype(F32).T
        ms = jnp.mean(qt * qt, axis=0, keepdims=True)
        qt = qt * lax.rsqrt(ms + EPS) * qnw_ref[...] * (HEAD_DIM ** -0.5 * LOG2_E)
        qts.append(qt.astype(BF16))
    q4t = jnp.concatenate(qts, axis=1)
    per_head = tq // LANE
    n_ch = G * per_head
    cols = [slice(c * LANE, (c + 1) * LANE) for c in range(n_ch)]
    qcols = [slice((c % per_head) * LANE, (c % per_head + 1) * LANE) for c in range(n_ch)]

    wk = WINDOW + tq
    ws = pl.multiple_of(jnp.maximum(t0 - WINDOW, 0), tq)
    band = wband_ref.at[jnp.minimum(i, WINDOW // tq)]
    sw = _dot(kwn_ref[pl.ds(ws, wk), :], q4t)
    for c in range(n_ch):
        sg = sw[:, cols[c]] + band[:, qcols[c]]
        p = jnp.exp2(sg - jnp.max(sg, axis=0, keepdims=True)).astype(BF16)
        acc_w = None
        for j in range(wk // tk):
            part = _dot(vwt_ref[ws // tk + j], p[j * tk:(j + 1) * tk, :])
            acc_w = part if acc_w is None else acc_w + part
        owin_ref[:, cols[c]] = acc_w[0:HEAD_DIM] * (1.0 / acc_w[HEAD_DIM:HEAD_DIM + 1])

    sc = _dot(kc_ref[0, 0], q4t)
    n_sub = lax.broadcasted_iota(jnp.int32, (LANE, LANE), 0)
    pts = []
    psums = [None] * per_head
    for c in range(n_ch):
        t_lane = t0 + (c % per_head) * LANE + lax.broadcasted_iota(jnp.int32, (LANE, LANE), 1)
        cm = (n_sub * CMP_STRIDE + (CMP_LEN - 1)) <= t_lane
        sg = jnp.where(cm, sc[:, cols[c]], NEG)
        e = jnp.exp2(sg - jnp.max(sg, axis=0, keepdims=True))
        p = e * (1.0 / jnp.sum(e, axis=0, keepdims=True))
        p = jnp.where(cm, p, 0.0)
        pts.append(p.astype(BF16))
        k = c % per_head
        psums[k] = p if psums[k] is None else psums[k] + p
    ocmp_ref[...] = _dot(vct_ref[...], jnp.concatenate(pts, axis=1))

    hi, mid, lo = _split3(jnp.concatenate(psums, axis=1))
    imp = _dot(ovlt_ref[...], hi) + _dot(ovlt_ref[...], mid) + _dot(ovlt_ref[...], lo)
    n_sel = S // SEL_LEN
    n_top = min(N_SELECT, n_sel)
    j_sub = lax.broadcasted_iota(jnp.int32, (n_sel, tq), 0)
    t_sel = t0 + lax.broadcasted_iota(jnp.int32, (n_sel, tq), 1)
    forced = (j_sub == (t_sel // SEL_LEN)) | (j_sub == 0)
    valid = (j_sub * SEL_LEN) <= t_sel
    score = jnp.where(forced, SEL_FORCE, jnp.where(valid, imp[0:n_sel, :], -1.0))
    rank = jnp.zeros((n_sel, tq), F32)
    for c in range(n_sel):
        other = score[c:c + 1, :]
        beats = jnp.where(other > score, 1.0, jnp.where((other == score) & (j_sub > c), 1.0, 0.0))
        rank = rank + beats
    unsel = jnp.where(rank < n_top, 0.0, 1.0)
    unsel = jnp.concatenate([unsel, jnp.zeros((LANE - n_sel, tq), F32)], axis=0).astype(BF16)
    qaug_ref[0:HEAD_DIM, :] = q4t
    qaug_ref[HEAD_DIM:, :] = jnp.concatenate([unsel] * G, axis=1)

    m_ref[...] = jnp.full(m_ref.shape, NEG, F32)
    acc_ref[...] = jnp.zeros(acc_ref.shape, F32)
    k_sub = lax.broadcasted_iota(jnp.int32, (tk, LANE), 0)
    q_lane = lax.broadcasted_iota(jnp.int32, (tk, LANE), 1)

    def sel_scores(kt, slot):
        k0 = pl.multiple_of(kt * tk, tk)
        sbuf_ref[slot] = _dot(kaug_ref[pl.ds(k0, tk), :], qaug_ref[...])

    def sel_update(kt, slot, diagonal):
        vt = vst_ref[kt]
        scores = sbuf_ref.at[slot]
        for c in range(n_ch):
            sg = scores[:, cols[c]]
            if diagonal:
                sg = jnp.where(k_sub <= q_lane + (c % per_head) * LANE, sg, NEG)
            m_old = m_ref[:, cols[c]]
            m_new = jnp.maximum(m_old, jnp.max(sg, axis=0, keepdims=True))
            alpha = jnp.exp2(m_old - m_new)
            p = jnp.exp2(sg - m_new).astype(BF16)
            acc_ref[:, cols[c]] = alpha * acc_ref[:, cols[c]] + _dot(vt, p)
            m_ref[:, cols[c]] = m_new

    def sel_body(kt, carry):
        sel_update(kt, kt % 2, False)
        sel_scores(kt + 1, (kt + 1) % 2)
        return carry

    sel_scores(0, 0)
    lax.fori_loop(0, i, sel_body, 0)
    sel_update(i, i % 2, True)

    gt_ref[...] = (1.0 / (1.0 + jnp.exp(-zg_ref[...].astype(F32)))).T
    n_heads = NSA_KV_HEADS * G
    for c in range(n_ch):
        g = c // per_head
        col = hkv * G + g
        gate = lambda branch: gt_ref[pl.ds(branch * n_heads + col, 1), :][:, qcols[c]]
        o_sel = acc_ref[0:HEAD_DIM, cols[c]] * (1.0 / acc_ref[HEAD_DIM:HEAD_DIM + 1, cols[c]])
        out = gate(0) * ocmp_ref[:, cols[c]] + gate(1) * o_sel + gate(2) * owin_ref[:, cols[c]]
        o_ref[qcols[c], g * HEAD_DIM:(g + 1) * HEAD_DIM] = out.T.astype(o_ref.dtype)


def _nsa(z, zgate, kc, vc, qnw, knw, B, S, tq=512):
    T = z.shape[0]
    nq = S // tq
    G = NSA_GROUP
    n_c = (S - CMP_LEN) // CMP_STRIDE + 1
    n_sel = S // SEL_LEN
    assert S % tq == 0 and n_sel <= LANE and n_sel % 8 == 0 and n_c <= LANE and WINDOW % tq == 0
    assert S >= WINDOW + tq
    ci = np.arange(LANE)[None, :] * CMP_STRIDE
    sj = np.arange(LANE)[:, None] * SEL_LEN
    ovlt = ((ci < sj + SEL_LEN) & (ci + CMP_LEN > sj) & (np.arange(LANE)[None, :] < n_c)
            & (np.arange(LANE)[:, None] < n_sel))
    ovlt = jnp.asarray(ovlt.astype(np.float32), BF16)
    in_block = (np.arange(S)[:, None] // SEL_LEN) == np.arange(LANE)[None, :]
    negexp = jnp.asarray(in_block.astype(np.float32) * NEG, BF16)
    qnw_b = jnp.broadcast_to(qnw.reshape(HEAD_DIM, 1), (HEAD_DIM, tq))
    bands = []
    for p in range(WINDOW // tq + 1):
        t0, ws = p * tq, max(p * tq - WINDOW, 0)
        dist = (t0 + np.arange(tq)[None, :]) - (ws + np.arange(WINDOW + tq)[:, None])
        bands.append(np.where((dist >= 0) & (dist < WINDOW), 0.0, NEG))
    wband = jnp.asarray(np.stack(bands), F32)

    kvblk = lambda c: pl.BlockSpec((S, HEAD_DIM), lambda b, h, i, c=c: (b, c + h))
    cblk = pl.BlockSpec((1, 1, LANE, HEAD_DIM), lambda b, h, i: (b, h, 0, 0))
    full = lambda a: pl.BlockSpec(a.shape, lambda b, h, i, nd=a.ndim: (0,) * nd)
    return pl.pallas_call(
        functools.partial(_nsa_kernel, tq=tq),
        grid=(B, NSA_KV_HEADS, nq),
        in_specs=[
            pl.BlockSpec((tq, G * HEAD_DIM), lambda b, h, i: (b * nq + i, COL_Q // 512 + h)),
            pl.BlockSpec((tq, LANE), lambda b, h, i: (b * nq + i, 0)),
            kvblk(COL_KS // LANE), kvblk(COL_VS // LANE), kvblk(COL_KW // LANE), kvblk(COL_VW // LANE),
            cblk, cblk, full(qnw_b), full(knw), full(ovlt), full(negexp), full(wband),
        ],
        out_specs=pl.BlockSpec((tq, G * HEAD_DIM), lambda b, h, i: (b * nq + i, h)),
        out_shape=jax.ShapeDtypeStruct((T, NSA_WIDTH), BF16),
        scratch_shapes=[
            pltpu.VMEM((S, 2 * HEAD_DIM), BF16), pltpu.VMEM((S, HEAD_DIM), BF16),
            pltpu.VMEM((S // tq, V_ROWS, tq), BF16), pltpu.VMEM((S // tq, V_ROWS, tq), BF16),
            pltpu.VMEM((HEAD_DIM, LANE), BF16), pltpu.VMEM((LANE, tq), F32),
            pltpu.VMEM((1, G * tq), F32), pltpu.VMEM((V_ROWS, G * tq), F32),
            pltpu.VMEM((HEAD_DIM, G * tq), F32), pltpu.VMEM((HEAD_DIM, G * tq), F32),
            pltpu.VMEM((2 * HEAD_DIM, G * tq), BF16), pltpu.VMEM((2, tq, G * tq), F32),
        ],
        compiler_params=_params(("parallel", "parallel", "arbitrary")),
        name="nsa",
    )(z, zgate, z, z, z, z, kc, vc, qnw_b, knw, ovlt, negexp, wband)


def _pack_rows(x):
    w = x.shape[1] // 2
    return pltpu.pack_elementwise([x[:, :w], x[:, w:]], packed_dtype=BF16)


def _unpack_rows(p, dtype):
    lo = pltpu.unpack_elementwise(p, index=0, packed_dtype=BF16, unpacked_dtype=F32)
    hi = pltpu.unpack_elementwise(p, index=1, packed_dtype=BF16, unpacked_dtype=F32)
    return jnp.concatenate([lo.astype(dtype), hi.astype(dtype)], axis=1)


ROUTE_E = 0
ROUTE_W = 2
ROUTE_RANK = 4


def _outproj_router_kernel(yab_ref, yc_ref, wo_ref, x_ref, nw_ref, wrh_ref, wrl_ref, br_ref,
                           x1_ref, hnp_ref, route_ref, counts_ref, cnt_ref):
    @pl.when(pl.program_id(0) == 0)
    def _():
        cnt_ref[...] = jnp.zeros_like(cnt_ref)

    ka = yab_ref.shape[1]
    acc = _dot(yab_ref[...], wo_ref[0, 0:ka, :]) + _dot(yc_ref[...], wo_ref[0, ka:, :])
    x1 = x_ref[...] + acc
    x1_ref[...] = x1
    hn = _rms(x1, nw_ref[...])
    hnp_ref[...] = _pack_rows(hn)
    hi = hn.astype(BF16)
    lo = (hn - hi.astype(F32)).astype(BF16)
    lg = _dot(hi, wrh_ref[...]) + _dot(lo, wrh_ref[...]) + _dot(hi, wrl_ref[...]) + br_ref[...]

    tm = lg.shape[0]
    lane = lax.broadcasted_iota(jnp.int32, (tm, LANE), 1)
    lane_f = lane.astype(F32)
    big = float(LANE)

    def first_max(v):
        m = jnp.max(v, axis=1, keepdims=True)
        idx = jnp.min(jnp.where(v == m, lane_f, big), axis=1, keepdims=True)
        return m, idx

    is_grp = (lane >= N_EXPERTS) & (lane < N_EXPERTS + N_GROUPS_MOE)
    lgm = jnp.where(is_grp, lg, NEG)
    mg, grp_lane = first_max(lgm)
    p_grp = 1.0 / jnp.sum(jnp.where(is_grp, jnp.exp(lgm - mg), 0.0), axis=1, keepdims=True)
    grp = grp_lane - float(N_EXPERTS)
    in_grp = (lane < N_EXPERTS) & ((lane // EXPERTS_PER_GROUP).astype(F32) == grp)
    le = jnp.where(in_grp, lg, NEG)
    m1, i1 = first_max(le)
    le2 = jnp.where(lane_f == i1, NEG, le)
    m2, i2 = first_max(le2)
    e2 = jnp.exp(m2 - m1)
    den = 1.0 + e2
    w1 = p_grp * (1.0 / den)
    w2 = p_grp * (e2 / den)

    onehot = jnp.where((lane_f == i1) | (lane_f == i2), 1.0, 0.0)
    r_i = lax.broadcasted_iota(jnp.int32, (tm, tm), 0)
    c_i = lax.broadcasted_iota(jnp.int32, (tm, tm), 1)
    before = jnp.where(c_i < r_i, 1.0, 0.0).astype(BF16)
    base = cnt_ref[0:1, :] + _dot(before, onehot.astype(BF16))
    r1 = jnp.sum(jnp.where(lane_f == i1, base, 0.0), axis=1, keepdims=True)
    r2 = jnp.sum(jnp.where(lane_f == i2, base, 0.0), axis=1, keepdims=True)
    cnt_ref[0:1, :] = cnt_ref[0:1, :] + jnp.sum(onehot, axis=0, keepdims=True)
    counts_ref[...] = jnp.broadcast_to(cnt_ref[0:1, :], counts_ref.shape)

    route = jnp.zeros((tm, LANE), F32)
    for k, v in enumerate((i1, i2, w1, w2, r1, r2)):
        route = jnp.where(lane == k, v, route)
    route_ref[...] = route


def _outproj_router(yab, yc, wo_all, layer, x2d, nw, wr_hi, wr_lo, br, tm=512):
    T, D = x2d.shape
    full = lambda a: pl.BlockSpec(a.shape, lambda i, nd=a.ndim: (0,) * nd)
    return pl.pallas_call(
        _outproj_router_kernel,
        grid=(T // tm,),
        in_specs=[
            pl.BlockSpec((tm, yab.shape[1]), lambda i: (i, 0)),
            pl.BlockSpec((tm, yc.shape[1]), lambda i: (i, 0)),
            pl.BlockSpec((1,) + wo_all.shape[1:], lambda i: (layer, 0, 0)),
            pl.BlockSpec((tm, D), lambda i: (i, 0)),
            full(nw), full(wr_hi), full(wr_lo), full(br),
        ],
        out_specs=[pl.BlockSpec((tm, D), lambda i: (i, 0)),
                   pl.BlockSpec((tm, D // 2), lambda i: (i, 0)),
                   pl.BlockSpec((tm, LANE), lambda i: (i, 0)),
                   pl.BlockSpec((8, LANE), lambda i: (0, 0))],
        out_shape=[jax.ShapeDtypeStruct((T, D), F32),
                   jax.ShapeDtypeStruct((T, D // 2), jnp.uint32),
                   jax.ShapeDtypeStruct((T, LANE), F32),
                   jax.ShapeDtypeStruct((8, LANE), F32)],
        scratch_shapes=[pltpu.VMEM((8, LANE), F32)],
        compiler_params=_params(("arbitrary",)),
        name="outproj_router",
    )(yab, yc, wo_all, x2d, nw, wr_hi, wr_lo, br)


def _row_copy(src_ref, src_row, dst_ref, dst_row, sem):
    return pltpu.make_async_copy(src_ref.at[pl.ds(src_row, 1), :], dst_ref.at[pl.ds(dst_row, 1), :], sem)


def _dispatch_kernel(dest_ref, hnp_ref, xs_hbm, sem, *, chunk):
    i = pl.program_id(0)

    def body(j, carry):
        t = i * chunk + j
        _row_copy(hnp_ref, j, xs_hbm, dest_ref[2 * t], sem).start()
        _row_copy(hnp_ref, j, xs_hbm, dest_ref[2 * t + 1], sem).start()
        return carry

    lax.fori_loop(0, chunk, body, 0, unroll=8)
    for _ in range(2):
        pltpu.make_async_copy(hnp_ref, xs_hbm.at[pl.ds(0, chunk), :], sem).wait()


def _dispatch(dest, hnp, chunk=2048):
    T, W = hnp.shape
    return pl.pallas_call(
        functools.partial(_dispatch_kernel, chunk=chunk),
        grid_spec=pltpu.PrefetchScalarGridSpec(
            num_scalar_prefetch=1,
            grid=(T // chunk,),
            in_specs=[pl.BlockSpec((chunk, W), lambda i, d: (i, 0))],
            out_specs=pl.BlockSpec(memory_space=pl.ANY),
            scratch_shapes=[pltpu.SemaphoreType.DMA(())],
        ),
        out_shape=jax.ShapeDtypeStruct((2 * T, W), hnp.dtype),
        compiler_params=_params(("arbitrary",)),
        name="dispatch",
    )(dest, hnp)


FLAG_FIRST = 1
FLAG_LAST = 2
FLAG_NEW_EXPERT = 4
FLAG_SLOT = 8
FLAG_NEXT = 16


def _experts_kernel(tile_ref, exp_ref, lo_ref, hi_ref, flag_ref, xs_ref, wg_hbm, wu_hbm, wd_hbm,
                    ys_ref, acc_ref, wgb_ref, wub_ref, wdb_ref, wgf_ref, wuf_ref, wdf_ref, sem, *, tm, layer):
    w = pl.program_id(0)
    lo = lo_ref[w]
    hi = hi_ref[w]
    flags = flag_ref[w]
    slot = (flags // FLAG_SLOT) & 1
    next_e = flags // FLAG_NEXT - 1

    def weight_copies(expert, s):
        return [pltpu.make_async_copy(src.at[layer, expert], dst.at[s], sem.at[s])
                for src, dst in ((wg_hbm, wgf_ref), (wu_hbm, wuf_ref), (wd_hbm, wdf_ref))]

    @pl.when(w == 0)
    def _():
        for c in weight_copies(exp_ref[0], slot):
            c.start()

    @pl.when((flags & FLAG_NEW_EXPERT) != 0)
    def _():
        for c in weight_copies(exp_ref[w], slot):
            c.wait()
        wgb_ref[...] = wgf_ref[slot].astype(BF16)
        wub_ref[...] = wuf_ref[slot].astype(BF16)
        wdb_ref[...] = wdf_ref[slot].astype(BF16)

        @pl.when(next_e >= 0)
        def _():
            for c in weight_copies(next_e, 1 - slot):
                c.start()

    first = (flags & FLAG_FIRST) != 0

    @pl.when(hi > lo)
    def _():
        x = _unpack_rows(xs_ref[...], BF16)
        hg = _dot(x, wgb_ref[...])
        hu = _dot(x, wub_ref[...])
        row = tile_ref[w] * tm + lax.broadcasted_iota(jnp.int32, hg.shape, 0)
        h = jnp.where((row >= lo) & (row < hi), _silu(hg) * hu, 0.0).astype(BF16)

        @pl.when(first)
        def _():
            acc_ref[...] = _dot(h, wdb_ref[...])

        @pl.when(jnp.logical_not(first))
        def _():
            acc_ref[...] += _dot(h, wdb_ref[...])

    @pl.when((flags & FLAG_LAST) != 0)
    def _():
        ys_ref[...] = _pack_rows(acc_ref[...])


def _experts(meta, xs, wg_all, wu_all, wd_all, layer, tm):
    N, W = xs.shape
    _, E, D, F = wg_all.shape
    tile_w, exp_w, lo_w, hi_w, flag_w = meta
    n_work = tile_w.shape[0]
    hbm = pl.BlockSpec(memory_space=pl.ANY)
    return pl.pallas_call(
        functools.partial(_experts_kernel, tm=tm, layer=layer),
        grid_spec=pltpu.PrefetchScalarGridSpec(
            num_scalar_prefetch=5,
            grid=(n_work,),
            in_specs=[
                pl.BlockSpec((tm, W), lambda w, t, e, lo, hi, f: (t[w], 0)),
                hbm, hbm, hbm,
            ],
            out_specs=pl.BlockSpec((tm, W), lambda w, t, e, lo, hi, f: (t[w], 0)),
            scratch_shapes=[pltpu.VMEM((tm, D), F32), pltpu.VMEM((D, F), BF16),
                            pltpu.VMEM((D, F), BF16), pltpu.VMEM((F, D), BF16),
                            pltpu.VMEM((2, D, F), F32), pltpu.VMEM((2, D, F), F32),
                            pltpu.VMEM((2, F, D), F32), pltpu.SemaphoreType.DMA((2,))],
        ),
        out_shape=jax.ShapeDtypeStruct((N, W), xs.dtype),
        compiler_params=_params(("arbitrary",)),
        name="experts",
    )(tile_w, exp_w, lo_w, hi_w, flag_w, xs, wg_all, wu_all, wd_all)


def _work_items(counts, n_rows, tm):
    E = counts.shape[0]
    n_tiles = n_rows // tm
    n_work = n_tiles + E - 1
    start = jnp.cumsum(counts) - counts
    end = start + counts
    first_tile = start // tm
    last_tile = jnp.maximum(end - 1, 0) // tm
    n_e = jnp.where(counts > 0, last_tile - first_tile + 1, 0)
    wend = jnp.cumsum(n_e)
    wstart = wend - n_e
    total = wend[-1]
    w = jnp.arange(n_work, dtype=jnp.int32)
    wc = jnp.minimum(w, total - 1)
    ew = jnp.sum((wc[:, None] >= wend[None, :]).astype(jnp.int32), axis=1)
    tile_w = first_tile[ew] + (wc - wstart[ew])
    valid = w < total
    lo = jnp.where(valid, jnp.maximum(start[ew], tile_w * tm), 0)
    hi = jnp.where(valid, jnp.minimum(end[ew], (tile_w + 1) * tm), 0)
    prev_tile = jnp.concatenate([jnp.full((1,), -1, jnp.int32), tile_w[:-1]])
    next_tile = jnp.concatenate([tile_w[1:], jnp.full((1,), -1, jnp.int32)])
    prev_e = jnp.concatenate([jnp.full((1,), -1, jnp.int32), ew[:-1]])
    first = valid & (tile_w != prev_tile)
    last = valid & ((tile_w != next_tile) | (w == total - 1))
    new_e = ew != prev_e
    present = n_e > 0
    slot_e = (jnp.cumsum(present.astype(jnp.int32)) - 1) % 2
    ids = jnp.where(present, jnp.arange(E, dtype=jnp.int32), E)
    at_or_after = jnp.flip(lax.cummin(jnp.flip(ids)))
    next_e = jnp.concatenate([at_or_after[1:], jnp.full((1,), E, jnp.int32)])
    next_e = jnp.where(next_e < E, next_e, -1)
    i32 = lambda a: a.astype(jnp.int32)
    flags = (FLAG_FIRST * i32(first) + FLAG_LAST * i32(last) + FLAG_NEW_EXPERT * i32(new_e)
             + FLAG_SLOT * slot_e[ew] + FLAG_NEXT * (next_e[ew] + 1))
    return i32(tile_w), i32(ew), i32(lo), i32(hi), i32(flags)


def _combine_kernel(dest_ref, x1_ref, route_ref, ys_hbm, o_ref, buf_ref, sem, *, tt):
    i = pl.program_id(0)
    n = pl.num_programs(0)

    def issue(step, slot):
        def body(j, carry):
            t = step * tt + j
            _row_copy(ys_hbm, dest_ref[2 * t], buf_ref.at[slot, 0], j, sem.at[slot]).start()
            _row_copy(ys_hbm, dest_ref[2 * t + 1], buf_ref.at[slot, 1], j, sem.at[slot]).start()
            return carry
        lax.fori_loop(0, tt, body, 0, unroll=8)

    @pl.when(i == 0)
    def _():
        issue(0, 0)

    @pl.when(i + 1 < n)
    def _():
        issue(i + 1, (i + 1) % 2)

    slot = i % 2
    for k in range(2):
        pltpu.make_async_copy(ys_hbm.at[pl.ds(0, tt), :], buf_ref.at[slot, k], sem.at[slot]).wait()

    lane = lax.broadcasted_iota(jnp.int32, route_ref.shape, 1)
    route = route_ref[...]
    w0 = jnp.sum(jnp.where(lane == ROUTE_W, route, 0.0), axis=1, keepdims=True)
    w1 = jnp.sum(jnp.where(lane == ROUTE_W + 1, route, 0.0), axis=1, keepdims=True)
    y0 = _unpack_rows(buf_ref[slot, 0], F32)
    y1 = _unpack_rows(buf_ref[slot, 1], F32)
    o_ref[...] = x1_ref[...] + (w0 * y0 + w1 * y1)


def _combine(dest, x1, route, ys, tt=256):
    T, D = x1.shape
    W = ys.shape[1]
    return pl.pallas_call(
        functools.partial(_combine_kernel, tt=tt),
        grid_spec=pltpu.PrefetchScalarGridSpec(
            num_scalar_prefetch=1,
            grid=(T // tt,),
            in_specs=[
                pl.BlockSpec((tt, D), lambda i, d: (i, 0)),
                pl.BlockSpec((tt, LANE), lambda i, d: (i, 0)),
                pl.BlockSpec(memory_space=pl.ANY),
            ],
            out_specs=pl.BlockSpec((tt, D), lambda i, d: (i, 0)),
            scratch_shapes=[pltpu.VMEM((2, 2, tt, W), ys.dtype), pltpu.SemaphoreType.DMA((2,))],
        ),
        out_shape=jax.ShapeDtypeStruct((T, D), F32),
        compiler_params=_params(("arbitrary",)),
        name="combine",
    )(dest, x1, route, ys)


def _moe(x1, hnp, route, counts8, wg_all, wu_all, wd_all, layer, tm=512):
    T = x1.shape[0]
    E = wg_all.shape[1]
    counts = counts8[0, :E].astype(jnp.int32)
    eid = route[:, ROUTE_E:ROUTE_E + 2].astype(jnp.int32)
    rank = route[:, ROUTE_RANK:ROUTE_RANK + 2].astype(jnp.int32)
    start = jnp.cumsum(counts) - counts
    onehot = eid[..., None] == jnp.arange(E, dtype=jnp.int32)
    dest = (jnp.sum(jnp.where(onehot, start, 0), axis=-1) + rank).reshape(2 * T)
    xs = _dispatch(dest, hnp)
    ys = _experts(_work_items(counts, 2 * T, tm), xs, wg_all, wu_all, wd_all, layer, tm)
    return _combine(dest, x1, route, ys)


def kernel(x, norm1_w, w_in, pool_w, pool_scale, conv_w, cmp_pe_k, cmp_w1_k, cmp_w2_k, cmp_pe_v, cmp_w1_v, cmp_w2_v, q_norm_w, k_norm_w, w_out, norm2_w, router_grp_w, router_grp_b, router_exp_w, router_exp_b, exp_w_gate, exp_w_up, exp_w_down):
    B, S, D = x.shape
    depth = w_in.shape[0]
    T = B * S
    xf = x.reshape(T, D)

    def w1_pair(w1):
        return jnp.concatenate([w1[:CMP_STRIDE], w1[CMP_STRIDE:]], axis=-1).astype(BF16)

    def pe_rows(pe):
        return jnp.broadcast_to(pe.reshape(1, CMP_LEN * HEAD_DIM), (8, CMP_LEN * HEAD_DIM)).astype(BF16)

    w_in_all = w_in.astype(BF16)
    w_gate_all = jnp.pad(w_in[:, :, COL_GATE:], ((0, 0), (0, 0), (0, LANE - (D_IN - COL_GATE)))).astype(BF16)
    w_out_all = w_out.astype(BF16)

    for l in range(depth):
        z, zgate = _inproj(xf, norm1_w[l].reshape(1, D), w_in_all, w_gate_all, l)

        yab = _mix_ab(z, pool_w[l].astype(BF16), pool_scale[l].reshape(1, POOL_WIDTH), conv_w[l], B, S)

        kc, vc = _compress(
            z, w1_pair(cmp_w1_k[l]), w1_pair(cmp_w1_v[l]), pe_rows(cmp_pe_k[l]), pe_rows(cmp_pe_v[l]),
            cmp_w1_k[l].reshape(CMP_LEN * HEAD_DIM, CMP_HIDDEN).astype(BF16),
            cmp_w1_v[l].reshape(CMP_LEN * HEAD_DIM, CMP_HIDDEN).astype(BF16),
            cmp_w2_k[l].astype(BF16), cmp_w2_v[l].astype(BF16), k_norm_w[l], B, S)
        yc = _nsa(z, zgate, kc, vc, q_norm_w[l], k_norm_w[l], B, S)

        wr = jnp.concatenate([router_exp_w[l], router_grp_w[l]], axis=1)
        wr = jnp.pad(wr, ((0, 0), (0, LANE - wr.shape[1])))
        wr_hi = wr.astype(BF16)
        wr_lo = (wr - wr_hi.astype(F32)).astype(BF16)
        br = jnp.concatenate([router_exp_b[l], router_grp_b[l]])
        br = jnp.pad(br, (0, LANE - br.shape[0])).reshape(1, LANE)
        x1, hnp, route, counts8 = _outproj_router(yab, yc, w_out_all, l, xf, norm2_w[l].reshape(1, D),
                                                  wr_hi, wr_lo, br)

        xf = _moe(x1, hnp, route, counts8, exp_w_gate, exp_w_up, exp_w_down, l)
    return xf.reshape(B, S, D)
```

```python
import functools

import numpy as np
import jax
import jax.numpy as jnp
from jax import lax
from jax.experimental import pallas as pl
from jax.experimental.pallas import tpu as pltpu

F32 = jnp.float32
BF16 = jnp.bfloat16

POOL_WINDOWS = (2, 4, 8, 16)
LANE = 128
POOL_WIDTH = 512
CONV_WIDTH = 512
CONV_K = 3
NSA_WIDTH = 1024
HEAD_DIM = 128
NSA_KV_HEADS = 2
NSA_GROUP = 4
N_BRANCH = 3
CMP_LEN = 32
CMP_STRIDE = 16
CMP_HIDDEN = 256
SEL_LEN = 64
N_SELECT = 16
SEL_FORCE = 1.0e4
WINDOW = 512
N_GROUPS_MOE = 4
EXPERTS_PER_GROUP = 8
N_EXPERTS = 32
D_EXPERT = 256
EPS = 1e-6
NEG = -1e30
LOG2_E = 1.4426950408889634

COL_U = 0
COL_B = 512
COL_C = 1024
COL_V = 1536
COL_Q = 2048
COL_KC = 3072
COL_KS = 3584
COL_VS = 3840
COL_KW = 4096
COL_VW = 4352
COL_GATE = 4608
D_IN = 4632

VMEM_LIMIT = 56 * 1024 * 1024


def _params(sem):
    return pltpu.CompilerParams(dimension_semantics=sem, vmem_limit_bytes=VMEM_LIMIT)


def _rms(x, w):
    return x * lax.rsqrt(jnp.mean(x * x, axis=-1, keepdims=True) + EPS) * w


def _silu(x):
    return x / (1.0 + jnp.exp(-x))


def _dot(a, b):
    return jnp.dot(a, b, preferred_element_type=F32)


def _split3(a):
    hi = a.astype(BF16)
    r1 = a - hi.astype(F32)
    mid = r1.astype(BF16)
    lo = (r1 - mid.astype(F32)).astype(BF16)
    return hi, mid, lo


def _inproj_kernel(x_ref, nw_ref, w_ref, wgate_ref, o_ref, gate_ref, xn_ref):
    @pl.when(pl.program_id(1) == 0)
    def _():
        xn = _rms(x_ref[...], nw_ref[...]).astype(BF16)
        xn_ref[...] = xn
        gate_ref[...] = _dot(xn, wgate_ref[0])

    o_ref[...] = _dot(xn_ref[...], w_ref[0]).astype(o_ref.dtype)


def _inproj(x2d, nw, w_all, wgate_all, layer, tm=1024, tn=1536):
    T, D = x2d.shape
    N = COL_GATE
    assert N % tn == 0 and w_all.shape[2] >= N
    return pl.pallas_call(
        _inproj_kernel,
        grid=(T // tm, N // tn),
        in_specs=[
            pl.BlockSpec((tm, D), lambda i, j: (i, 0)),
            pl.BlockSpec((1, D), lambda i, j: (0, 0)),
            pl.BlockSpec((1, D, tn), lambda i, j: (layer, 0, j)),
            pl.BlockSpec((1, D, LANE), lambda i, j: (layer, 0, 0)),
        ],
        out_specs=[pl.BlockSpec((tm, tn), lambda i, j: (i, j)),
                   pl.BlockSpec((tm, LANE), lambda i, j: (i, 0))],
        out_shape=[jax.ShapeDtypeStruct((T, N), BF16), jax.ShapeDtypeStruct((T, LANE), F32)],
        scratch_shapes=[pltpu.VMEM((tm, D), BF16)],
        compiler_params=_params(("parallel", "arbitrary")),
        name="inproj",
    )(x2d, nw, w_all, wgate_all)


def _mix_ab_kernel(u_ref, b_ref, c_ref, v_ref, pw_ref, ps_ref, cw_ref, o_ref):
    S = u_ref.shape[0]
    row = lax.broadcasted_iota(jnp.int32, (S, LANE), 0)

    def shift(a, k):
        return jnp.where(row >= k, pltpu.roll(a, k, axis=0), 0.0)

    for g, w in enumerate(POOL_WINDOWS):
        sl = slice(g * LANE, (g + 1) * LANE)
        u = u_ref[:, sl].astype(F32)
        s = u
        k = 1
        while k < w:
            s = s + shift(s, k)
            k *= 2
        cnt = jnp.minimum(row + 1, w).astype(F32)
        mixed = (s / cnt - u).astype(BF16)
        y = _dot(mixed, pw_ref[g]) * ps_ref[:, sl]
        o_ref[:, sl] = y.astype(o_ref.dtype)

    for h in range(CONV_WIDTH // LANE):
        sl = slice(h * LANE, (h + 1) * LANE)
        u2 = c_ref[:, sl].astype(F32) * v_ref[:, sl].astype(F32)
        y = cw_ref[0:1, sl] * shift(u2, 2)
        y = y + cw_ref[1:2, sl] * shift(u2, 1)
        y = y + cw_ref[2:3, sl] * u2
        y = b_ref[:, sl].astype(F32) * y
        o_ref[:, POOL_WIDTH + h * LANE:POOL_WIDTH + (h + 1) * LANE] = y.astype(o_ref.dtype)


def _mix_ab(z, pool_w_bf16, pool_scale, conv_w, B, S):
    T = z.shape[0]
    blk = lambda c: pl.BlockSpec((S, 512), lambda b, c=c: (b, c))
    return pl.pallas_call(
        _mix_ab_kernel,
        grid=(B,),
        in_specs=[
            blk(COL_U // 512), blk(COL_B // 512), blk(COL_C // 512), blk(COL_V // 512),
            pl.BlockSpec((4, LANE, LANE), lambda b: (0, 0, 0)),
            pl.BlockSpec((1, POOL_WIDTH), lambda b: (0, 0)),
            pl.BlockSpec((CONV_K, CONV_WIDTH), lambda b: (0, 0)),
        ],
        out_specs=pl.BlockSpec((S, POOL_WIDTH + CONV_WIDTH), lambda b: (b, 0)),
        out_shape=jax.ShapeDtypeStruct((T, POOL_WIDTH + CONV_WIDTH), BF16),
        compiler_params=_params(("parallel",)),
        name="mix_ab",
    )(z, z, z, z, pool_w_bf16, pool_scale, conv_w)


def _compress_kernel(z_ref, w1k_ref, w1v_ref, pek_ref, pev_ref, w1kf_ref, w1vf_ref,
                     w2k_ref, w2v_ref, knw_ref, kc_ref, vc_ref, xf_ref):
    assert CMP_LEN == 2 * CMP_STRIDE
    n16 = z_ref.shape[0] // CMP_STRIDE
    for cg in range(z_ref.shape[1] // HEAD_DIM):
        xf_ref[cg] = z_ref[:, cg * HEAD_DIM:(cg + 1) * HEAD_DIM].astype(F32)
    streams = ((w1k_ref, pek_ref, w1kf_ref, w2k_ref, kc_ref),
               (w1v_ref, pev_ref, w1vf_ref, w2v_ref, vc_ref))
    for which, (w1_ref, pe_ref, w1f_ref, w2_ref, out_ref) in enumerate(streams):
        pe_term = _dot(pe_ref[...], w1f_ref[...])[0:1, :]
        for h in range(NSA_KV_HEADS):
            cg = which * NSA_KV_HEADS + h
            acc = None
            for l in range(CMP_STRIDE):
                rows = xf_ref[cg, pl.ds(l, n16, stride=CMP_STRIDE), :]
                part = _dot(rows.astype(BF16), w1_ref[l])
                acc = part if acc is None else acc + part
            first = acc[:, :CMP_HIDDEN]
            second = acc[:, CMP_HIDDEN:]
            hid = first + pltpu.roll(second, n16 - 1, axis=0) + pe_term
            out = _dot(_silu(hid).astype(BF16), w2_ref[...])
            if which == 0:
                out = _rms(out, knw_ref[0:1, :])
            out_ref[0, h] = out.astype(out_ref.dtype)


def _compress(z, w1k, w1v, pek8, pev8, w1kf, w1vf, w2k, w2v, knw, B, S):
    n16 = S // CMP_STRIDE
    full = lambda a: pl.BlockSpec(a.shape, lambda b, nd=a.ndim: (0,) * nd)
    out_sds = jax.ShapeDtypeStruct((B, NSA_KV_HEADS, n16, HEAD_DIM), BF16)
    out_spec = pl.BlockSpec((1, NSA_KV_HEADS, n16, HEAD_DIM), lambda b: (b, 0, 0, 0))
    return pl.pallas_call(
        _compress_kernel,
        grid=(B,),
        in_specs=[pl.BlockSpec((S, 512), lambda b: (b, COL_KC // 512)),
                  full(w1k), full(w1v), full(pek8), full(pev8), full(w1kf), full(w1vf),
                  full(w2k), full(w2v), full(knw)],
        out_specs=[out_spec, out_spec],
        out_shape=[out_sds, out_sds],
        scratch_shapes=[pltpu.VMEM((512 // HEAD_DIM, S, HEAD_DIM), F32)],
        compiler_params=_params(("parallel",)),
        name="compress",
    )(z, w1k, w1v, pek8, pev8, w1kf, w1vf, w2k, w2v, knw)


V_ROWS = HEAD_DIM + 16

def _nsa_kernel(zq_ref, zg_ref, ks_ref, vs_ref, kw_ref, vw_ref, kc_ref, vc_ref,
                qnw_ref, knw_ref, ovlt_ref, negexp_ref, wband_ref, o_ref,
                kaug_ref, kwn_ref, vst_ref, vwt_ref, vct_ref, gt_ref, m_ref, acc_ref,
                ocmp_ref, owin_ref, qaug_ref, sbuf_ref, *, tq):
    G = NSA_GROUP
    S = ks_ref.shape[0]
    hkv = pl.program_id(1)
    i = pl.program_id(2)
    tk = tq
    t0 = i * tq

    def transpose_to_bf16(a):
        return a.astype(F32).T.astype(BF16)

    @pl.when(i == 0)
    def _():
        kaug_ref[:, 0:HEAD_DIM] = _rms(ks_ref[...].astype(F32), knw_ref[1:2, :]).astype(BF16)
        kaug_ref[:, HEAD_DIM:] = negexp_ref[...]
        kwn_ref[0:WINDOW, :] = jnp.zeros((WINDOW, HEAD_DIM), BF16)
        kwn_ref[WINDOW:, :] = _rms(kw_ref[...].astype(F32), knw_ref[2:3, :]).astype(BF16)
        ones_row = jnp.where(lax.broadcasted_iota(jnp.int32, (V_ROWS - HEAD_DIM, tk), 0) == 0, 1.0, 0.0)
        for j in range(S // tk):
            vst_ref[j, 0:HEAD_DIM, :] = transpose_to_bf16(vs_ref[j * tk:(j + 1) * tk, :])
            vst_ref[j, HEAD_DIM:, :] = ones_row.astype(BF16)
        vwt_ref[0:HEAD_DIM, 0:WINDOW] = jnp.zeros((HEAD_DIM, WINDOW), BF16)
        for j in range(S // tk):
            vwt_ref[0:HEAD_DIM, WINDOW + j * tk:WINDOW + (j + 1) * tk] = transpose_to_bf16(
                vw_ref[j * tk:(j + 1) * tk, :])
        vwt_ref[HEAD_DIM:, :] = jnp.where(
            lax.broadcasted_iota(jnp.int32, (V_ROWS - HEAD_DIM, S + WINDOW), 0) == 0, 1.0, 0.0).astype(BF16)
        vct_ref[...] = transpose_to_bf16(vc_ref[0, 0])

    qts = []
    for g in range(G):
        qt = zq_ref[:, g * HEAD_DIM:(g + 1) * HEAD_DIM].astype(F32).T
        ms = jnp.mean(qt * qt, axis=0, keepdims=True)
        qt = qt * lax.rsqrt(ms + EPS) * qnw_ref[...] * (HEAD_DIM ** -0.5 * LOG2_E)
        qts.append(qt.astype(BF16))
    q4t = jnp.concatenate(qts, axis=1)
    per_head = tq // LANE
    n_ch = G * per_head
    cols = [slice(c * LANE, (c + 1) * LANE) for c in range(n_ch)]
    qcols = [slice((c % per_head) * LANE, (c % per_head + 1) * LANE) for c in range(n_ch)]

    wk = WINDOW + tq
    band = wband_ref.at[jnp.minimum(i, 1)]
    sw = _dot(kwn_ref[pl.ds(pl.multiple_of(t0, tq), wk), :], q4t)
    span = WINDOW + LANE
    for c in range(n_ch):
        q0 = (c % per_head) * LANE
        sg = sw[q0:q0 + span, cols[c]] + band[q0:q0 + span, qcols[c]]
        p = jnp.exp2(sg - jnp.max(sg, axis=0, keepdims=True)).astype(BF16)
        acc_w = _dot(vwt_ref[:, pl.ds(pl.multiple_of(t0 + q0, LANE), span)], p)
        owin_ref[:, cols[c]] = acc_w[0:HEAD_DIM] * (1.0 / acc_w[HEAD_DIM:HEAD_DIM + 1])

    sc = _dot(kc_ref[0, 0], q4t)
    n_sub = lax.broadcasted_iota(jnp.int32, (LANE, LANE), 0)
    pts = []
    psums = [None] * per_head
    for c in range(n_ch):
        t_lane = t0 + (c % per_head) * LANE + lax.broadcasted_iota(jnp.int32, (LANE, LANE), 1)
        cm = (n_sub * CMP_STRIDE + (CMP_LEN - 1)) <= t_lane
        sg = jnp.where(cm, sc[:, cols[c]], NEG)
        e = jnp.exp2(sg - jnp.max(sg, axis=0, keepdims=True))
        p = e * (1.0 / jnp.sum(e, axis=0, keepdims=True))
        p = jnp.where(cm, p, 0.0)
        pts.append(p.astype(BF16))
        k = c % per_head
        psums[k] = p if psums[k] is None else psums[k] + p
    ocmp_ref[...] = _dot(vct_ref[...], jnp.concatenate(pts, axis=1))

    hi, mid, lo = _split3(jnp.concatenate(psums, axis=1))
    imp = _dot(ovlt_ref[...], hi) + _dot(ovlt_ref[...], mid) + _dot(ovlt_ref[...], lo)
    n_sel = S // SEL_LEN
    n_top = min(N_SELECT, n_sel)
    j_sub = lax.broadcasted_iota(jnp.int32, (n_sel, tq), 0)
    t_sel = t0 + lax.broadcasted_iota(jnp.int32, (n_sel, tq), 1)
    forced = (j_sub == (t_sel // SEL_LEN)) | (j_sub == 0)
    valid = (j_sub * SEL_LEN) <= t_sel
    score = jnp.where(forced, SEL_FORCE, jnp.where(valid, imp[0:n_sel, :], -1.0))
    rank = jnp.zeros((n_sel, tq), F32)
    for c in range(n_sel):
        other = score[c:c + 1, :]
        beats = jnp.where(other > score, 1.0, jnp.where((other == score) & (j_sub > c), 1.0, 0.0))
        rank = rank + beats
    unsel = jnp.where(rank < n_top, 0.0, 1.0)
    unsel = jnp.concatenate([unsel, jnp.zeros((LANE - n_sel, tq), F32)], axis=0).astype(BF16)
    qaug_ref[0:HEAD_DIM, :] = q4t
    qaug_ref[HEAD_DIM:, :] = jnp.concatenate([unsel] * G, axis=1)

    m_ref[...] = jnp.full(m_ref.shape, NEG, F32)
    acc_ref[...] = jnp.zeros(acc_ref.shape, F32)

    def sel_scores(kt, slot):
        k0 = pl.multiple_of(kt * tk, tk)
        sbuf_ref[slot] = _dot(kaug_ref[pl.ds(k0, tk), :], qaug_ref[...])

    def sel_update(kt, slot, diagonal):
        vt = vst_ref[kt]
        scores = sbuf_ref.at[slot]
        for c in range(n_ch):
            if diagonal:
                q0 = (c % per_head) * LANE
                n_k = q0 + LANE
                k_sub = lax.broadcasted_iota(jnp.int32, (n_k, LANE), 0)
                q_lane = lax.broadcasted_iota(jnp.int32, (n_k, LANE), 1)
                sg = jnp.where(k_sub <= q_lane + q0, scores[0:n_k, cols[c]], NEG)
            else:
                n_k = tk
                sg = scores[:, cols[c]]
            m_old = m_ref[:, cols[c]]
            m_new = jnp.maximum(m_old, jnp.max(sg, axis=0, keepdims=True))
            alpha = jnp.exp2(m_old - m_new)
            p = jnp.exp2(sg - m_new).astype(BF16)
            acc_ref[:, cols[c]] = alpha * acc_ref[:, cols[c]] + _dot(vt[:, 0:n_k], p)
            m_ref[:, cols[c]] = m_new

    def sel_body(kt, carry):
        sel_update(kt, kt % 2, False)
        sel_scores(kt + 1, (kt + 1) % 2)
        return carry

    sel_scores(0, 0)
    lax.fori_loop(0, i, sel_body, 0)
    sel_update(i, i % 2, True)

    gt_ref[...] = (1.0 / (1.0 + jnp.exp(-zg_ref[...].astype(F32)))).T
    n_heads = NSA_KV_HEADS * G
    for c in range(n_ch):
        g = c // per_head
        col = hkv * G + g
        gate = lambda branch: gt_ref[pl.ds(branch * n_heads + col, 1), :][:, qcols[c]]
        o_sel = acc_ref[0:HEAD_DIM, cols[c]] * (1.0 / acc_ref[HEAD_DIM:HEAD_DIM + 1, cols[c]])
        out = gate(0) * ocmp_ref[:, cols[c]] + gate(1) * o_sel + gate(2) * owin_ref[:, cols[c]]
        o_ref[qcols[c], g * HEAD_DIM:(g + 1) * HEAD_DIM] = out.T.astype(o_ref.dtype)


def _nsa(z, zgate, kc, vc, qnw, knw, B, S, tq=512):
    T = z.shape[0]
    nq = S // tq
    G = NSA_GROUP
    n_c = (S - CMP_LEN) // CMP_STRIDE + 1
    n_sel = S // SEL_LEN
    assert S % tq == 0 and n_sel <= LANE and n_sel % 8 == 0 and n_c <= LANE and WINDOW % tq == 0
    assert S >= WINDOW + tq
    ci = np.arange(LANE)[None, :] * CMP_STRIDE
    sj = np.arange(LANE)[:, None] * SEL_LEN
    ovlt = ((ci < sj + SEL_LEN) & (ci + CMP_LEN > sj) & (np.arange(LANE)[None, :] < n_c)
            & (np.arange(LANE)[:, None] < n_sel))
    ovlt = jnp.asarray(ovlt.astype(np.float32), BF16)
    in_block = (np.arange(S)[:, None] // SEL_LEN) == np.arange(LANE)[None, :]
    negexp = jnp.asarray(in_block.astype(np.float32) * NEG, BF16)
    qnw_b = jnp.broadcast_to(qnw.reshape(HEAD_DIM, 1), (HEAD_DIM, tq))
    r_idx = np.arange(WINDOW + tq)[:, None]
    q_idx = np.arange(tq)[None, :]
    in_band = (r_idx > q_idx) & (r_idx <= q_idx + WINDOW)
    wband = jnp.asarray(np.stack([np.where(in_band & (r_idx >= WINDOW), 0.0, NEG),
                                  np.where(in_band, 0.0, NEG)]), F32)

    kvblk = lambda c: pl.BlockSpec((S, HEAD_DIM), lambda b, h, i, c=c: (b, c + h))
    cblk = pl.BlockSpec((1, 1, LANE, HEAD_DIM), lambda b, h, i: (b, h, 0, 0))
    full = lambda a: pl.BlockSpec(a.shape, lambda b, h, i, nd=a.ndim: (0,) * nd)
    return pl.pallas_call(
        functools.partial(_nsa_kernel, tq=tq),
        grid=(B, NSA_KV_HEADS, nq),
        in_specs=[
            pl.BlockSpec((tq, G * HEAD_DIM), lambda b, h, i: (b * nq + i, COL_Q // 512 + h)),
            pl.BlockSpec((tq, LANE), lambda b, h, i: (b * nq + i, 0)),
            kvblk(COL_KS // LANE), kvblk(COL_VS // LANE), kvblk(COL_KW // LANE), kvblk(COL_VW // LANE),
            cblk, cblk, full(qnw_b), full(knw), full(ovlt), full(negexp), full(wband),
        ],
        out_specs=pl.BlockSpec((tq, G * HEAD_DIM), lambda b, h, i: (b * nq + i, h)),
        out_shape=jax.ShapeDtypeStruct((T, NSA_WIDTH), BF16),
        scratch_shapes=[
            pltpu.VMEM((S, 2 * HEAD_DIM), BF16), pltpu.VMEM((S + WINDOW, HEAD_DIM), BF16),
            pltpu.VMEM((S // tq, V_ROWS, tq), BF16), pltpu.VMEM((V_ROWS, S + WINDOW), BF16),
            pltpu.VMEM((HEAD_DIM, LANE), BF16), pltpu.VMEM((LANE, tq), F32),
            pltpu.VMEM((1, G * tq), F32), pltpu.VMEM((V_ROWS, G * tq), F32),
            pltpu.VMEM((HEAD_DIM, G * tq), F32), pltpu.VMEM((HEAD_DIM, G * tq), F32),
            pltpu.VMEM((2 * HEAD_DIM, G * tq), BF16), pltpu.VMEM((2, tq, G * tq), F32),
        ],
        compiler_params=_params(("parallel", "parallel", "arbitrary")),
        name="nsa",
    )(z, zgate, z, z, z, z, kc, vc, qnw_b, knw, ovlt, negexp, wband)


def _pack_rows(x):
    w = x.shape[1] // 2
    return pltpu.pack_elementwise([x[:, :w], x[:, w:]], packed_dtype=BF16)


def _unpack_rows(p, dtype):
    lo = pltpu.unpack_elementwise(p, index=0, packed_dtype=BF16, unpacked_dtype=F32)
    hi = pltpu.unpack_elementwise(p, index=1, packed_dtype=BF16, unpacked_dtype=F32)
    return jnp.concatenate([lo.astype(dtype), hi.astype(dtype)], axis=1)


ROUTE_E = 0
ROUTE_W = 2
ROUTE_RANK = 4


def _outproj_router_kernel(yab_ref, yc_ref, wo_ref, x_ref, nw_ref, wrh_ref, wrl_ref, br_ref,
                           x1_ref, hnp_ref, route_ref, counts_ref, cnt_ref):
    @pl.when(pl.program_id(0) == 0)
    def _():
        cnt_ref[...] = jnp.zeros_like(cnt_ref)

    ka = yab_ref.shape[1]
    acc = _dot(yab_ref[...], wo_ref[0, 0:ka, :]) + _dot(yc_ref[...], wo_ref[0, ka:, :])
    x1 = x_ref[...] + acc
    x1_ref[...] = x1
    hn = _rms(x1, nw_ref[...])
    hnp_ref[...] = _pack_rows(hn)
    hi = hn.astype(BF16)
    lo = (hn - hi.astype(F32)).astype(BF16)
    lg = _dot(hi, wrh_ref[...]) + _dot(lo, wrh_ref[...]) + _dot(hi, wrl_ref[...]) + br_ref[...]

    tm = lg.shape[0]
    lane = lax.broadcasted_iota(jnp.int32, (tm, LANE), 1)
    lane_f = lane.astype(F32)
    big = float(LANE)

    def first_max(v):
        m = jnp.max(v, axis=1, keepdims=True)
        idx = jnp.min(jnp.where(v == m, lane_f, big), axis=1, keepdims=True)
        return m, idx

    is_grp = (lane >= N_EXPERTS) & (lane < N_EXPERTS + N_GROUPS_MOE)
    lgm = jnp.where(is_grp, lg, NEG)
    mg, grp_lane = first_max(lgm)
    p_grp = 1.0 / jnp.sum(jnp.where(is_grp, jnp.exp(lgm - mg), 0.0), axis=1, keepdims=True)
    grp = grp_lane - float(N_EXPERTS)
    in_grp = (lane < N_EXPERTS) & ((lane // EXPERTS_PER_GROUP).astype(F32) == grp)
    le = jnp.where(in_grp, lg, NEG)
    m1, i1 = first_max(le)
    le2 = jnp.where(lane_f == i1, NEG, le)
    m2, i2 = first_max(le2)
    e2 = jnp.exp(m2 - m1)
    den = 1.0 + e2
    w1 = p_grp * (1.0 / den)
    w2 = p_grp * (e2 / den)

    onehot = jnp.where((lane_f == i1) | (lane_f == i2), 1.0, 0.0)
    r_i = lax.broadcasted_iota(jnp.int32, (tm, tm), 0)
    c_i = lax.broadcasted_iota(jnp.int32, (tm, tm), 1)
    before = jnp.where(c_i < r_i, 1.0, 0.0).astype(BF16)
    base = cnt_ref[0:1, :] + _dot(before, onehot.astype(BF16))
    r1 = jnp.sum(jnp.where(lane_f == i1, base, 0.0), axis=1, keepdims=True)
    r2 = jnp.sum(jnp.where(lane_f == i2, base, 0.0), axis=1, keepdims=True)
    cnt_ref[0:1, :] = cnt_ref[0:1, :] + jnp.sum(onehot, axis=0, keepdims=True)
    counts_ref[...] = jnp.broadcast_to(cnt_ref[0:1, :], counts_ref.shape)

    route = jnp.zeros((tm, LANE), F32)
    for k, v in enumerate((i1, i2, w1, w2, r1, r2)):
        route = jnp.where(lane == k, v, route)
    route_ref[...] = route


def _outproj_router(yab, yc, wo_all, layer, x2d, nw, wr_hi, wr_lo, br, tm=512):
    T, D = x2d.shape
    full = lambda a: pl.BlockSpec(a.shape, lambda i, nd=a.ndim: (0,) * nd)
    return pl.pallas_call(
        _outproj_router_kernel,
        grid=(T // tm,),
        in_specs=[
            pl.BlockSpec((tm, yab.shape[1]), lambda i: (i, 0)),
            pl.BlockSpec((tm, yc.shape[1]), lambda i: (i, 0)),
            pl.BlockSpec((1,) + wo_all.shape[1:], lambda i: (layer, 0, 0)),
            pl.BlockSpec((tm, D), lambda i: (i, 0)),
            full(nw), full(wr_hi), full(wr_lo), full(br),
        ],
        out_specs=[pl.BlockSpec((tm, D), lambda i: (i, 0)),
                   pl.BlockSpec((tm, D // 2), lambda i: (i, 0)),
                   pl.BlockSpec((tm, LANE), lambda i: (i, 0)),
                   pl.BlockSpec((8, LANE), lambda i: (0, 0))],
        out_shape=[jax.ShapeDtypeStruct((T, D), F32),
                   jax.ShapeDtypeStruct((T, D // 2), jnp.uint32),
                   jax.ShapeDtypeStruct((T, LANE), F32),
                   jax.ShapeDtypeStruct((8, LANE), F32)],
        scratch_shapes=[pltpu.VMEM((8, LANE), F32)],
        compiler_params=_params(("arbitrary",)),
        name="outproj_router",
    )(yab, yc, wo_all, x2d, nw, wr_hi, wr_lo, br)


def _row_copy(src_ref, src_row, dst_ref, dst_row, sem):
    return pltpu.make_async_copy(src_ref.at[pl.ds(src_row, 1), :], dst_ref.at[pl.ds(dst_row, 1), :], sem)


def _dispatch_kernel(dest_ref, hnp_ref, xs_hbm, sem, *, chunk):
    i = pl.program_id(0)

    def body(j, carry):
        t = i * chunk + j
        _row_copy(hnp_ref, j, xs_hbm, dest_ref[2 * t], sem).start()
        _row_copy(hnp_ref, j, xs_hbm, dest_ref[2 * t + 1], sem).start()
        return carry

    lax.fori_loop(0, chunk, body, 0, unroll=8)
    for _ in range(2):
        pltpu.make_async_copy(hnp_ref, xs_hbm.at[pl.ds(0, chunk), :], sem).wait()


def _dispatch(dest, hnp, chunk=2048):
    T, W = hnp.shape
    return pl.pallas_call(
        functools.partial(_dispatch_kernel, chunk=chunk),
        grid_spec=pltpu.PrefetchScalarGridSpec(
            num_scalar_prefetch=1,
            grid=(T // chunk,),
            in_specs=[pl.BlockSpec((chunk, W), lambda i, d: (i, 0))],
            out_specs=pl.BlockSpec(memory_space=pl.ANY),
            scratch_shapes=[pltpu.SemaphoreType.DMA(())],
        ),
        out_shape=jax.ShapeDtypeStruct((2 * T, W), hnp.dtype),
        compiler_params=_params(("arbitrary",)),
        name="dispatch",
    )(dest, hnp)


FLAG_FIRST = 1
FLAG_LAST = 2
FLAG_NEW_EXPERT = 4
FLAG_SLOT = 8
FLAG_NEXT = 16


def _experts_kernel(tile_ref, exp_ref, lo_ref, hi_ref, flag_ref, xs_ref, wg_hbm, wu_hbm, wd_hbm,
                    ys_ref, acc_ref, wgb_ref, wub_ref, wdb_ref, wgf_ref, wuf_ref, wdf_ref, sem, *, tm, layer):
    w = pl.program_id(0)
    lo = lo_ref[w]
    hi = hi_ref[w]
    flags = flag_ref[w]
    slot = (flags // FLAG_SLOT) & 1
    next_e = flags // FLAG_NEXT - 1

    def weight_copies(expert, s):
        return [pltpu.make_async_copy(src.at[layer, expert], dst.at[s], sem.at[s])
                for src, dst in ((wg_hbm, wgf_ref), (wu_hbm, wuf_ref), (wd_hbm, wdf_ref))]

    @pl.when(w == 0)
    def _():
        for c in weight_copies(exp_ref[0], slot):
            c.start()

    @pl.when((flags & FLAG_NEW_EXPERT) != 0)
    def _():
        for c in weight_copies(exp_ref[w], slot):
            c.wait()
        wgb_ref[...] = wgf_ref[slot].astype(BF16)
        wub_ref[...] = wuf_ref[slot].astype(BF16)
        wdb_ref[...] = wdf_ref[slot].astype(BF16)

        @pl.when(next_e >= 0)
        def _():
            for c in weight_copies(next_e, 1 - slot):
                c.start()

    first = (flags & FLAG_FIRST) != 0

    @pl.when(hi > lo)
    def _():
        x = _unpack_rows(xs_ref[...], BF16)
        hg = _dot(x, wgb_ref[...])
        hu = _dot(x, wub_ref[...])
        row = tile_ref[w] * tm + lax.broadcasted_iota(jnp.int32, hg.shape, 0)
        h = jnp.where((row >= lo) & (row < hi), _silu(hg) * hu, 0.0).astype(BF16)

        @pl.when(first)
        def _():
            acc_ref[...] = _dot(h, wdb_ref[...])

        @pl.when(jnp.logical_not(first))
        def _():
            acc_ref[...] += _dot(h, wdb_ref[...])

    @pl.when((flags & FLAG_LAST) != 0)
    def _():
        ys_ref[...] = _pack_rows(acc_ref[...])


def _experts(meta, xs, wg_all, wu_all, wd_all, layer, tm):
    N, W = xs.shape
    _, E, D, F = wg_all.shape
    tile_w, exp_w, lo_w, hi_w, flag_w = meta
    n_work = tile_w.shape[0]
    hbm = pl.BlockSpec(memory_space=pl.ANY)
    return pl.pallas_call(
        functools.partial(_experts_kernel, tm=tm, layer=layer),
        grid_spec=pltpu.PrefetchScalarGridSpec(
            num_scalar_prefetch=5,
            grid=(n_work,),
            in_specs=[
                pl.BlockSpec((tm, W), lambda w, t, e, lo, hi, f: (t[w], 0)),
                hbm, hbm, hbm,
            ],
            out_specs=pl.BlockSpec((tm, W), lambda w, t, e, lo, hi, f: (t[w], 0)),
            scratch_shapes=[pltpu.VMEM((tm, D), F32), pltpu.VMEM((D, F), BF16),
                            pltpu.VMEM((D, F), BF16), pltpu.VMEM((F, D), BF16),
                            pltpu.VMEM((2, D, F), F32), pltpu.VMEM((2, D, F), F32),
                            pltpu.VMEM((2, F, D), F32), pltpu.SemaphoreType.DMA((2,))],
        ),
        out_shape=jax.ShapeDtypeStruct((N, W), xs.dtype),
        compiler_params=_params(("arbitrary",)),
        name="experts",
    )(tile_w, exp_w, lo_w, hi_w, flag_w, xs, wg_all, wu_all, wd_all)


def _work_items(counts, n_rows, tm):
    E = counts.shape[0]
    n_tiles = n_rows // tm
    n_work = n_tiles + E - 1
    start = jnp.cumsum(counts) - counts
    end = start + counts
    first_tile = start // tm
    last_tile = jnp.maximum(end - 1, 0) // tm
    n_e = jnp.where(counts > 0, last_tile - first_tile + 1, 0)
    wend = jnp.cumsum(n_e)
    wstart = wend - n_e
    total = wend[-1]
    w = jnp.arange(n_work, dtype=jnp.int32)
    wc = jnp.minimum(w, total - 1)
    ew = jnp.sum((wc[:, None] >= wend[None, :]).astype(jnp.int32), axis=1)
    tile_w = first_tile[ew] + (wc - wstart[ew])
    valid = w < total
    lo = jnp.where(valid, jnp.maximum(start[ew], tile_w * tm), 0)
    hi = jnp.where(valid, jnp.minimum(end[ew], (tile_w + 1) * tm), 0)
    prev_tile = jnp.concatenate([jnp.full((1,), -1, jnp.int32), tile_w[:-1]])
    next_tile = jnp.concatenate([tile_w[1:], jnp.full((1,), -1, jnp.int32)])
    prev_e = jnp.concatenate([jnp.full((1,), -1, jnp.int32), ew[:-1]])
    first = valid & (tile_w != prev_tile)
    last = valid & ((tile_w != next_tile) | (w == total - 1))
    new_e = ew != prev_e
    present = n_e > 0
    slot_e = (jnp.cumsum(present.astype(jnp.int32)) - 1) % 2
    ids = jnp.where(present, jnp.arange(E, dtype=jnp.int32), E)
    at_or_after = jnp.flip(lax.cummin(jnp.flip(ids)))
    next_e = jnp.concatenate([at_or_after[1:], jnp.full((1,), E, jnp.int32)])
    next_e = jnp.where(next_e < E, next_e, -1)
    i32 = lambda a: a.astype(jnp.int32)
    flags = (FLAG_FIRST * i32(first) + FLAG_LAST * i32(last) + FLAG_NEW_EXPERT * i32(new_e)
             + FLAG_SLOT * slot_e[ew] + FLAG_NEXT * (next_e[ew] + 1))
    return i32(tile_w), i32(ew), i32(lo), i32(hi), i32(flags)


def _combine_kernel(dest_ref, x1_ref, route_ref, ys_hbm, o_ref, buf_ref, sem, *, tt):
    i = pl.program_id(0)
    n = pl.num_programs(0)

    def issue(step, slot):
        def body(j, carry):
            t = step * tt + j
            _row_copy(ys_hbm, dest_ref[2 * t], buf_ref.at[slot, 0], j, sem.at[slot]).start()
            _row_copy(ys_hbm, dest_ref[2 * t + 1], buf_ref.at[slot, 1], j, sem.at[slot]).start()
            return carry
        lax.fori_loop(0, tt, body, 0, unroll=8)

    @pl.when(i == 0)
    def _():
        issue(0, 0)

    @pl.when(i + 1 < n)
    def _():
        issue(i + 1, (i + 1) % 2)

    slot = i % 2
    for k in range(2):
        pltpu.make_async_copy(ys_hbm.at[pl.ds(0, tt), :], buf_ref.at[slot, k], sem.at[slot]).wait()

    lane = lax.broadcasted_iota(jnp.int32, route_ref.shape, 1)
    route = route_ref[...]
    w0 = jnp.sum(jnp.where(lane == ROUTE_W, route, 0.0), axis=1, keepdims=True)
    w1 = jnp.sum(jnp.where(lane == ROUTE_W + 1, route, 0.0), axis=1, keepdims=True)
    y0 = _unpack_rows(buf_ref[slot, 0], F32)
    y1 = _unpack_rows(buf_ref[slot, 1], F32)
    o_ref[...] = x1_ref[...] + (w0 * y0 + w1 * y1)


def _combine(dest, x1, route, ys, tt=256):
    T, D = x1.shape
    W = ys.shape[1]
    return pl.pallas_call(
        functools.partial(_combine_kernel, tt=tt),
        grid_spec=pltpu.PrefetchScalarGridSpec(
            num_scalar_prefetch=1,
            grid=(T // tt,),
            in_specs=[
                pl.BlockSpec((tt, D), lambda i, d: (i, 0)),
                pl.BlockSpec((tt, LANE), lambda i, d: (i, 0)),
                pl.BlockSpec(memory_space=pl.ANY),
            ],
            out_specs=pl.BlockSpec((tt, D), lambda i, d: (i, 0)),
            scratch_shapes=[pltpu.VMEM((2, 2, tt, W), ys.dtype), pltpu.SemaphoreType.DMA((2,))],
        ),
        out_shape=jax.ShapeDtypeStruct((T, D), F32),
        compiler_params=_params(("arbitrary",)),
        name="combine",
    )(dest, x1, route, ys)


def _moe(x1, hnp, route, counts8, wg_all, wu_all, wd_all, layer, tm=512):
    T = x1.shape[0]
    E = wg_all.shape[1]
    counts = counts8[0, :E].astype(jnp.int32)
    eid = route[:, ROUTE_E:ROUTE_E + 2].astype(jnp.int32)
    rank = route[:, ROUTE_RANK:ROUTE_RANK + 2].astype(jnp.int32)
    start = jnp.cumsum(counts) - counts
    onehot = eid[..., None] == jnp.arange(E, dtype=jnp.int32)
    dest = (jnp.sum(jnp.where(onehot, start, 0), axis=-1) + rank).reshape(2 * T)
    xs = _dispatch(dest, hnp)
    ys = _experts(_work_items(counts, 2 * T, tm), xs, wg_all, wu_all, wd_all, layer, tm)
    return _combine(dest, x1, route, ys)


def kernel(x, norm1_w, w_in, pool_w, pool_scale, conv_w, cmp_pe_k, cmp_w1_k, cmp_w2_k, cmp_pe_v, cmp_w1_v, cmp_w2_v, q_norm_w, k_norm_w, w_out, norm2_w, router_grp_w, router_grp_b, router_exp_w, router_exp_b, exp_w_gate, exp_w_up, exp_w_down):
    B, S, D = x.shape
    depth = w_in.shape[0]
    T = B * S
    xf = x.reshape(T, D)

    def w1_pair(w1):
        return jnp.concatenate([w1[:CMP_STRIDE], w1[CMP_STRIDE:]], axis=-1).astype(BF16)

    def pe_rows(pe):
        return jnp.broadcast_to(pe.reshape(1, CMP_LEN * HEAD_DIM), (8, CMP_LEN * HEAD_DIM)).astype(BF16)

    w_in_all = w_in.astype(BF16)
    w_gate_all = jnp.pad(w_in[:, :, COL_GATE:], ((0, 0), (0, 0), (0, LANE - (D_IN - COL_GATE)))).astype(BF16)
    w_out_all = w_out.astype(BF16)

    for l in range(depth):
        z, zgate = _inproj(xf, norm1_w[l].reshape(1, D), w_in_all, w_gate_all, l)

        yab = _mix_ab(z, pool_w[l].astype(BF16), pool_scale[l].reshape(1, POOL_WIDTH), conv_w[l], B, S)

        kc, vc = _compress(
            z, w1_pair(cmp_w1_k[l]), w1_pair(cmp_w1_v[l]), pe_rows(cmp_pe_k[l]), pe_rows(cmp_pe_v[l]),
            cmp_w1_k[l].reshape(CMP_LEN * HEAD_DIM, CMP_HIDDEN).astype(BF16),
            cmp_w1_v[l].reshape(CMP_LEN * HEAD_DIM, CMP_HIDDEN).astype(BF16),
            cmp_w2_k[l].astype(BF16), cmp_w2_v[l].astype(BF16), k_norm_w[l], B, S)
        yc = _nsa(z, zgate, kc, vc, q_norm_w[l], k_norm_w[l], B, S)

        wr = jnp.concatenate([router_exp_w[l], router_grp_w[l]], axis=1)
        wr = jnp.pad(wr, ((0, 0), (0, LANE - wr.shape[1])))
        wr_hi = wr.astype(BF16)
        wr_lo = (wr - wr_hi.astype(F32)).astype(BF16)
        br = jnp.concatenate([router_exp_b[l], router_grp_b[l]])
        br = jnp.pad(br, (0, LANE - br.shape[0])).reshape(1, LANE)
        x1, hnp, route, counts8 = _outproj_router(yab, yc, w_out_all, l, xf, norm2_w[l].reshape(1, D),
                                                  wr_hi, wr_lo, br)

        xf = _moe(x1, hnp, route, counts8, exp_w_gate, exp_w_up, exp_w_down, l)
    return xf.reshape(B, S, D)
```

```python
import functools

import numpy as np
import jax
import jax.numpy as jnp
from jax import lax
from jax.experimental import pallas as pl
from jax.experimental.pallas import tpu as pltpu

F32 = jnp.float32
BF16 = jnp.bfloat16

POOL_WINDOWS = (2, 4, 8, 16)
LANE = 128
POOL_WIDTH = 512
CONV_WIDTH = 512
CONV_K = 3
NSA_WIDTH = 1024
HEAD_DIM = 128
NSA_KV_HEADS = 2
NSA_GROUP = 4
N_BRANCH = 3
CMP_LEN = 32
CMP_STRIDE = 16
CMP_HIDDEN = 256
SEL_LEN = 64
N_SELECT = 16
SEL_FORCE = 1.0e4
WINDOW = 512
N_GROUPS_MOE = 4
EXPERTS_PER_GROUP = 8
N_EXPERTS = 32
D_EXPERT = 256
EPS = 1e-6
NEG = -1e30
LOG2_E = 1.4426950408889634

COL_U = 0
COL_B = 512
COL_C = 1024
COL_V = 1536
COL_Q = 2048
COL_KC = 3072
COL_KS = 3584
COL_VS = 3840
COL_KW = 4096
COL_VW = 4352
COL_GATE = 4608
D_IN = 4632

VMEM_LIMIT = 56 * 1024 * 1024


def _params(sem):
    return pltpu.CompilerParams(dimension_semantics=sem, vmem_limit_bytes=VMEM_LIMIT)


def _rms(x, w):
    return x * lax.rsqrt(jnp.mean(x * x, axis=-1, keepdims=True) + EPS) * w


def _silu(x):
    return x / (1.0 + jnp.exp(-x))


def _dot(a, b):
    return jnp.dot(a, b, preferred_element_type=F32)


def _split3(a):
    hi = a.astype(BF16)
    r1 = a - hi.astype(F32)
    mid = r1.astype(BF16)
    lo = (r1 - mid.astype(F32)).astype(BF16)
    return hi, mid, lo


def _inproj_kernel(x_ref, nw_ref, w_ref, wgate_ref, o_ref, gate_ref, xn_ref):
    @pl.when(pl.program_id(1) == 0)
    def _():
        xn = _rms(x_ref[...], nw_ref[...]).astype(BF16)
        xn_ref[...] = xn
        gate_ref[...] = _dot(xn, wgate_ref[0])

    o_ref[...] = _dot(xn_ref[...], w_ref[0]).astype(o_ref.dtype)


def _inproj(x2d, nw, w_all, wgate_all, layer, tm=1024, tn=1536):
    T, D = x2d.shape
    N = COL_GATE
    assert N % tn == 0 and w_all.shape[2] >= N
    return pl.pallas_call(
        _inproj_kernel,
        grid=(T // tm, N // tn),
        in_specs=[
            pl.BlockSpec((tm, D), lambda i, j: (i, 0)),
            pl.BlockSpec((1, D), lambda i, j: (0, 0)),
            pl.BlockSpec((1, D, tn), lambda i, j: (layer, 0, j)),
            pl.BlockSpec((1, D, LANE), lambda i, j: (layer, 0, 0)),
        ],
        out_specs=[pl.BlockSpec((tm, tn), lambda i, j: (i, j)),
                   pl.BlockSpec((tm, LANE), lambda i, j: (i, 0))],
        out_shape=[jax.ShapeDtypeStruct((T, N), BF16), jax.ShapeDtypeStruct((T, LANE), F32)],
        scratch_shapes=[pltpu.VMEM((tm, D), BF16)],
        compiler_params=_params(("parallel", "arbitrary")),
        name="inproj",
    )(x2d, nw, w_all, wgate_all)


def _mix_ab_kernel(u_ref, b_ref, c_ref, v_ref, pw_ref, ps_ref, cw_ref, o_ref):
    S = u_ref.shape[0]
    row = lax.broadcasted_iota(jnp.int32, (S, LANE), 0)

    def shift(a, k):
        return jnp.where(row >= k, pltpu.roll(a, k, axis=0), 0.0)

    for g, w in enumerate(POOL_WINDOWS):
        sl = slice(g * LANE, (g + 1) * LANE)
        u = u_ref[:, sl].astype(F32)
        s = u
        k = 1
        while k < w:
            s = s + shift(s, k)
            k *= 2
        cnt = jnp.minimum(row + 1, w).astype(F32)
        mixed = (s / cnt - u).astype(BF16)
        y = _dot(mixed, pw_ref[g]) * ps_ref[:, sl]
        o_ref[:, sl] = y.astype(o_ref.dtype)

    for h in range(CONV_WIDTH // LANE):
        sl = slice(h * LANE, (h + 1) * LANE)
        u2 = c_ref[:, sl].astype(F32) * v_ref[:, sl].astype(F32)
        y = cw_ref[0:1, sl] * shift(u2, 2)
        y = y + cw_ref[1:2, sl] * shift(u2, 1)
        y = y + cw_ref[2:3, sl] * u2
        y = b_ref[:, sl].astype(F32) * y
        o_ref[:, POOL_WIDTH + h * LANE:POOL_WIDTH + (h + 1) * LANE] = y.astype(o_ref.dtype)


def _mix_ab(z, pool_w_bf16, pool_scale, conv_w, B, S):
    T = z.shape[0]
    blk = lambda c: pl.BlockSpec((S, 512), lambda b, c=c: (b, c))
    return pl.pallas_call(
        _mix_ab_kernel,
        grid=(B,),
        in_specs=[
            blk(COL_U // 512), blk(COL_B // 512), blk(COL_C // 512), blk(COL_V // 512),
            pl.BlockSpec((4, LANE, LANE), lambda b: (0, 0, 0)),
            pl.BlockSpec((1, POOL_WIDTH), lambda b: (0, 0)),
            pl.BlockSpec((CONV_K, CONV_WIDTH), lambda b: (0, 0)),
        ],
        out_specs=pl.BlockSpec((S, POOL_WIDTH + CONV_WIDTH), lambda b: (b, 0)),
        out_shape=jax.ShapeDtypeStruct((T, POOL_WIDTH + CONV_WIDTH), BF16),
        compiler_params=_params(("parallel",)),
        name="mix_ab",
    )(z, z, z, z, pool_w_bf16, pool_scale, conv_w)


def _compress_kernel(z_ref, w1k_ref, w1v_ref, pek_ref, pev_ref, w1kf_ref, w1vf_ref,
                     w2k_ref, w2v_ref, knw_ref, kc_ref, vc_ref, xf_ref):
    assert CMP_LEN == 2 * CMP_STRIDE
    n16 = z_ref.shape[0] // CMP_STRIDE
    for cg in range(z_ref.shape[1] // HEAD_DIM):
        xf_ref[cg] = z_ref[:, cg * HEAD_DIM:(cg + 1) * HEAD_DIM].astype(F32)
    streams = ((w1k_ref, pek_ref, w1kf_ref, w2k_ref, kc_ref),
               (w1v_ref, pev_ref, w1vf_ref, w2v_ref, vc_ref))
    for which, (w1_ref, pe_ref, w1f_ref, w2_ref, out_ref) in enumerate(streams):
        pe_term = _dot(pe_ref[...], w1f_ref[...])[0:1, :]
        for h in range(NSA_KV_HEADS):
            cg = which * NSA_KV_HEADS + h
            acc = None
            for l in range(CMP_STRIDE):
                rows = xf_ref[cg, pl.ds(l, n16, stride=CMP_STRIDE), :]
                part = _dot(rows.astype(BF16), w1_ref[l])
                acc = part if acc is None else acc + part
            first = acc[:, :CMP_HIDDEN]
            second = acc[:, CMP_HIDDEN:]
            hid = first + pltpu.roll(second, n16 - 1, axis=0) + pe_term
            out = _dot(_silu(hid).astype(BF16), w2_ref[...])
            if which == 0:
                out = _rms(out, knw_ref[0:1, :])
            out_ref[0, h] = out.astype(out_ref.dtype)


def _compress(z, w1k, w1v, pek8, pev8, w1kf, w1vf, w2k, w2v, knw, B, S):
    n16 = S // CMP_STRIDE
    full = lambda a: pl.BlockSpec(a.shape, lambda b, nd=a.ndim: (0,) * nd)
    out_sds = jax.ShapeDtypeStruct((B, NSA_KV_HEADS, n16, HEAD_DIM), BF16)
    out_spec = pl.BlockSpec((1, NSA_KV_HEADS, n16, HEAD_DIM), lambda b: (b, 0, 0, 0))
    return pl.pallas_call(
        _compress_kernel,
        grid=(B,),
        in_specs=[pl.BlockSpec((S, 512), lambda b: (b, COL_KC // 512)),
                  full(w1k), full(w1v), full(pek8), full(pev8), full(w1kf), full(w1vf),
                  full(w2k), full(w2v), full(knw)],
        out_specs=[out_spec, out_spec],
        out_shape=[out_sds, out_sds],
        scratch_shapes=[pltpu.VMEM((512 // HEAD_DIM, S, HEAD_DIM), F32)],
        compiler_params=_params(("parallel",)),
        name="compress",
    )(z, w1k, w1v, pek8, pev8, w1kf, w1vf, w2k, w2v, knw)


V_ROWS = HEAD_DIM + 16

def _nsa_kernel(zq_ref, zg_ref, ks_ref, vs_ref, kw_ref, vw_ref, kc_ref, vc_ref,
                qnw_ref, knw_ref, ovlt_ref, negexp_ref, wband_ref, o_ref,
                kaug_ref, kwn_ref, vst_ref, vwt_ref, vct_ref, gt_ref, m_ref, acc_ref,
                ocmp_ref, owin_ref, qaug_ref, sbuf_ref, *, tq):
    G = NSA_GROUP
    S = ks_ref.shape[0]
    hkv = pl.program_id(1)
    i = pl.program_id(2)
    tk = tq
    t0 = i * tq

    def transpose_to_bf16(a):
        return a.astype(F32).T.astype(BF16)

    @pl.when(i == 0)
    def _():
        kaug_ref[:, 0:HEAD_DIM] = _rms(ks_ref[...].astype(F32), knw_ref[1:2, :]).astype(BF16)
        kaug_ref[:, HEAD_DIM:] = negexp_ref[...]
        kwn_ref[0:WINDOW, :] = jnp.zeros((WINDOW, HEAD_DIM), BF16)
        kwn_ref[WINDOW:, :] = _rms(kw_ref[...].astype(F32), knw_ref[2:3, :]).astype(BF16)
        ones_row = jnp.where(lax.broadcasted_iota(jnp.int32, (V_ROWS - HEAD_DIM, tk), 0) == 0, 1.0, 0.0)
        for j in range(S // tk):
            vst_ref[j, 0:HEAD_DIM, :] = transpose_to_bf16(vs_ref[j * tk:(j + 1) * tk, :])
            vst_ref[j, HEAD_DIM:, :] = ones_row.astype(BF16)
        vwt_ref[0:HEAD_DIM, 0:WINDOW] = jnp.zeros((HEAD_DIM, WINDOW), BF16)
        for j in range(S // tk):
            vwt_ref[0:HEAD_DIM, WINDOW + j * tk:WINDOW + (j + 1) * tk] = transpose_to_bf16(
                vw_ref[j * tk:(j + 1) * tk, :])
        vwt_ref[HEAD_DIM:, :] = jnp.where(
            lax.broadcasted_iota(jnp.int32, (V_ROWS - HEAD_DIM, S + WINDOW), 0) == 0, 1.0, 0.0).astype(BF16)
        vct_ref[...] = transpose_to_bf16(vc_ref[0, 0])

    qts = []
    for g in range(G):
        qt = zq_ref[:, g * HEAD_DIM:(g + 1) * HEAD_DIM].astype(F32).T
        ms = jnp.mean(qt * qt, axis=0, keepdims=True)
        qt = qt * lax.rsqrt(ms + EPS) * qnw_ref[...] * (HEAD_DIM ** -0.5 * LOG2_E)
        qts.append(qt.astype(BF16))
    q4t = jnp.concatenate(qts, axis=1)
    per_head = tq // LANE
    n_ch = G * per_head
    cols = [slice(c * LANE, (c + 1) * LANE) for c in range(n_ch)]
    qcols = [slice((c % per_head) * LANE, (c % per_head + 1) * LANE) for c in range(n_ch)]

    wk = WINDOW + tq
    band = wband_ref.at[jnp.minimum(i, 1)]
    sw = _dot(kwn_ref[pl.ds(pl.multiple_of(t0, tq), wk), :], q4t)
    span = WINDOW + 2 * LANE
    for c in range(0, n_ch, 2):
        q0 = (c % per_head) * LANE
        ps = []
        for cc in (c, c + 1):
            sg = sw[q0:q0 + span, cols[cc]] + band[q0:q0 + span, qcols[cc]]
            ps.append(jnp.exp2(sg - jnp.max(sg, axis=0, keepdims=True)).astype(BF16))
        acc_w = _dot(vwt_ref[:, pl.ds(pl.multiple_of(t0 + q0, LANE), span)],
                     jnp.concatenate(ps, axis=1))
        both = slice(c * LANE, (c + 2) * LANE)
        owin_ref[:, both] = acc_w[0:HEAD_DIM] * (1.0 / acc_w[HEAD_DIM:HEAD_DIM + 1])

    sc = _dot(kc_ref[0, 0], q4t)
    n_sub = lax.broadcasted_iota(jnp.int32, (LANE, LANE), 0)
    pts = []
    psums = [None] * per_head
    for c in range(n_ch):
        t_lane = t0 + (c % per_head) * LANE + lax.broadcasted_iota(jnp.int32, (LANE, LANE), 1)
        cm = (n_sub * CMP_STRIDE + (CMP_LEN - 1)) <= t_lane
        sg = jnp.where(cm, sc[:, cols[c]], NEG)
        e = jnp.exp2(sg - jnp.max(sg, axis=0, keepdims=True))
        p = e * (1.0 / jnp.sum(e, axis=0, keepdims=True))
        p = jnp.where(cm, p, 0.0)
        pts.append(p.astype(BF16))
        k = c % per_head
        psums[k] = p if psums[k] is None else psums[k] + p
    ocmp_ref[...] = _dot(vct_ref[...], jnp.concatenate(pts, axis=1))

    hi, mid, lo = _split3(jnp.concatenate(psums, axis=1))
    imp = _dot(ovlt_ref[...], hi) + _dot(ovlt_ref[...], mid) + _dot(ovlt_ref[...], lo)
    n_sel = S // SEL_LEN
    n_top = min(N_SELECT, n_sel)
    j_sub = lax.broadcasted_iota(jnp.int32, (n_sel, tq), 0)
    t_sel = t0 + lax.broadcasted_iota(jnp.int32, (n_sel, tq), 1)
    forced = (j_sub == (t_sel // SEL_LEN)) | (j_sub == 0)
    valid = (j_sub * SEL_LEN) <= t_sel
    score = jnp.where(forced, SEL_FORCE, jnp.where(valid, imp[0:n_sel, :], -1.0))
    rank = jnp.zeros((n_sel, tq), F32)
    for c in range(n_sel):
        other = score[c:c + 1, :]
        beats = jnp.where(other > score, 1.0, jnp.where((other == score) & (j_sub > c), 1.0, 0.0))
        rank = rank + beats
    unsel = jnp.where(rank < n_top, 0.0, 1.0)
    unsel = jnp.concatenate([unsel, jnp.zeros((LANE - n_sel, tq), F32)], axis=0).astype(BF16)
    qaug_ref[0:HEAD_DIM, :] = q4t
    qaug_ref[HEAD_DIM:, :] = jnp.concatenate([unsel] * G, axis=1)

    m_ref[...] = jnp.full(m_ref.shape, NEG, F32)
    acc_ref[...] = jnp.zeros(acc_ref.shape, F32)

    def sel_scores(kt, slot):
        k0 = pl.multiple_of(kt * tk, tk)
        sbuf_ref[slot] = _dot(kaug_ref[pl.ds(k0, tk), :], qaug_ref[...])

    def sel_update(kt, slot, diagonal):
        vt = vst_ref[kt]
        scores = sbuf_ref.at[slot]
        for c in range(0, n_ch, 2):
            pair = (c, c + 1)
            n_k = (c % per_head + 2) * LANE if diagonal else tk
            ps, alphas = [], []
            for cc in pair:
                sg = scores[0:n_k, cols[cc]]
                if diagonal:
                    k_sub = lax.broadcasted_iota(jnp.int32, (n_k, LANE), 0)
                    q_lane = lax.broadcasted_iota(jnp.int32, (n_k, LANE), 1)
                    sg = jnp.where(k_sub <= q_lane + (cc % per_head) * LANE, sg, NEG)
                m_old = m_ref[:, cols[cc]]
                m_new = jnp.maximum(m_old, jnp.max(sg, axis=0, keepdims=True))
                alphas.append(jnp.exp2(m_old - m_new))
                ps.append(jnp.exp2(sg - m_new).astype(BF16))
                m_ref[:, cols[cc]] = m_new
            both = slice(c * LANE, (c + 2) * LANE)
            acc_ref[:, both] = (jnp.concatenate(alphas, axis=1) * acc_ref[:, both]
                                + _dot(vt[:, 0:n_k], jnp.concatenate(ps, axis=1)))

    def sel_body(kt, carry):
        sel_update(kt, kt % 2, False)
        sel_scores(kt + 1, (kt + 1) % 2)
        return carry

    sel_scores(0, 0)
    lax.fori_loop(0, i, sel_body, 0)
    sel_update(i, i % 2, True)

    gt_ref[...] = (1.0 / (1.0 + jnp.exp(-zg_ref[...].astype(F32)))).T
    n_heads = NSA_KV_HEADS * G
    for c in range(n_ch):
        g = c // per_head
        col = hkv * G + g
        gate = lambda branch: gt_ref[pl.ds(branch * n_heads + col, 1), :][:, qcols[c]]
        o_sel = acc_ref[0:HEAD_DIM, cols[c]] * (1.0 / acc_ref[HEAD_DIM:HEAD_DIM + 1, cols[c]])
        out = gate(0) * ocmp_ref[:, cols[c]] + gate(1) * o_sel + gate(2) * owin_ref[:, cols[c]]
        o_ref[qcols[c], g * HEAD_DIM:(g + 1) * HEAD_DIM] = out.T.astype(o_ref.dtype)


def _nsa(z, zgate, kc, vc, qnw, knw, B, S, tq=512):
    T = z.shape[0]
    nq = S // tq
    G = NSA_GROUP
    n_c = (S - CMP_LEN) // CMP_STRIDE + 1
    n_sel = S // SEL_LEN
    assert S % tq == 0 and n_sel <= LANE and n_sel % 8 == 0 and n_c <= LANE and WINDOW % tq == 0
    assert S >= WINDOW + tq
    ci = np.arange(LANE)[None, :] * CMP_STRIDE
    sj = np.arange(LANE)[:, None] * SEL_LEN
    ovlt = ((ci < sj + SEL_LEN) & (ci + CMP_LEN > sj) & (np.arange(LANE)[None, :] < n_c)
            & (np.arange(LANE)[:, None] < n_sel))
    ovlt = jnp.asarray(ovlt.astype(np.float32), BF16)
    in_block = (np.arange(S)[:, None] // SEL_LEN) == np.arange(LANE)[None, :]
    negexp = jnp.asarray(in_block.astype(np.float32) * NEG, BF16)
    qnw_b = jnp.broadcast_to(qnw.reshape(HEAD_DIM, 1), (HEAD_DIM, tq))
    r_idx = np.arange(WINDOW + tq)[:, None]
    q_idx = np.arange(tq)[None, :]
    in_band = (r_idx > q_idx) & (r_idx <= q_idx + WINDOW)
    wband = jnp.asarray(np.stack([np.where(in_band & (r_idx >= WINDOW), 0.0, NEG),
                                  np.where(in_band, 0.0, NEG)]), F32)

    kvblk = lambda c: pl.BlockSpec((S, HEAD_DIM), lambda b, h, i, c=c: (b, c + h))
    cblk = pl.BlockSpec((1, 1, LANE, HEAD_DIM), lambda b, h, i: (b, h, 0, 0))
    full = lambda a: pl.BlockSpec(a.shape, lambda b, h, i, nd=a.ndim: (0,) * nd)
    return pl.pallas_call(
        functools.partial(_nsa_kernel, tq=tq),
        grid=(B, NSA_KV_HEADS, nq),
        in_specs=[
            pl.BlockSpec((tq, G * HEAD_DIM), lambda b, h, i: (b * nq + i, COL_Q // 512 + h)),
            pl.BlockSpec((tq, LANE), lambda b, h, i: (b * nq + i, 0)),
            kvblk(COL_KS // LANE), kvblk(COL_VS // LANE), kvblk(COL_KW // LANE), kvblk(COL_VW // LANE),
            cblk, cblk, full(qnw_b), full(knw), full(ovlt), full(negexp), full(wband),
        ],
        out_specs=pl.BlockSpec((tq, G * HEAD_DIM), lambda b, h, i: (b * nq + i, h)),
        out_shape=jax.ShapeDtypeStruct((T, NSA_WIDTH), BF16),
        scratch_shapes=[
            pltpu.VMEM((S, 2 * HEAD_DIM), BF16), pltpu.VMEM((S + WINDOW, HEAD_DIM), BF16),
            pltpu.VMEM((S // tq, V_ROWS, tq), BF16), pltpu.VMEM((V_ROWS, S + WINDOW), BF16),
            pltpu.VMEM((HEAD_DIM, LANE), BF16), pltpu.VMEM((LANE, tq), F32),
            pltpu.VMEM((1, G * tq), F32), pltpu.VMEM((V_ROWS, G * tq), F32),
            pltpu.VMEM((HEAD_DIM, G * tq), F32), pltpu.VMEM((HEAD_DIM, G * tq), F32),
            pltpu.VMEM((2 * HEAD_DIM, G * tq), BF16), pltpu.VMEM((2, tq, G * tq), F32),
        ],
        compiler_params=_params(("parallel", "parallel", "arbitrary")),
        name="nsa",
    )(z, zgate, z, z, z, z, kc, vc, qnw_b, knw, ovlt, negexp, wband)


def _pack_rows(x):
    w = x.shape[1] // 2
    return pltpu.pack_elementwise([x[:, :w], x[:, w:]], packed_dtype=BF16)


def _unpack_rows(p, dtype):
    lo = pltpu.unpack_elementwise(p, index=0, packed_dtype=BF16, unpacked_dtype=F32)
    hi = pltpu.unpack_elementwise(p, index=1, packed_dtype=BF16, unpacked_dtype=F32)
    return jnp.concatenate([lo.astype(dtype), hi.astype(dtype)], axis=1)


ROUTE_E = 0
ROUTE_W = 2
ROUTE_RANK = 4


def _outproj_router_kernel(yab_ref, yc_ref, wo_ref, x_ref, nw_ref, wrh_ref, wrl_ref, br_ref,
                           x1_ref, hnp_ref, route_ref, counts_ref, cnt_ref):
    @pl.when(pl.program_id(0) == 0)
    def _():
        cnt_ref[...] = jnp.zeros_like(cnt_ref)

    ka = yab_ref.shape[1]
    acc = _dot(yab_ref[...], wo_ref[0, 0:ka, :]) + _dot(yc_ref[...], wo_ref[0, ka:, :])
    x1 = x_ref[...] + acc
    x1_ref[...] = x1
    hn = _rms(x1, nw_ref[...])
    hnp_ref[...] = _pack_rows(hn)
    hi = hn.astype(BF16)
    lo = (hn - hi.astype(F32)).astype(BF16)
    lg = _dot(hi, wrh_ref[...]) + _dot(lo, wrh_ref[...]) + _dot(hi, wrl_ref[...]) + br_ref[...]

    tm = lg.shape[0]
    lane = lax.broadcasted_iota(jnp.int32, (tm, LANE), 1)
    lane_f = lane.astype(F32)
    big = float(LANE)

    def first_max(v):
        m = jnp.max(v, axis=1, keepdims=True)
        idx = jnp.min(jnp.where(v == m, lane_f, big), axis=1, keepdims=True)
        return m, idx

    is_grp = (lane >= N_EXPERTS) & (lane < N_EXPERTS + N_GROUPS_MOE)
    lgm = jnp.where(is_grp, lg, NEG)
    mg, grp_lane = first_max(lgm)
    p_grp = 1.0 / jnp.sum(jnp.where(is_grp, jnp.exp(lgm - mg), 0.0), axis=1, keepdims=True)
    grp = grp_lane - float(N_EXPERTS)
    in_grp = (lane < N_EXPERTS) & ((lane // EXPERTS_PER_GROUP).astype(F32) == grp)
    le = jnp.where(in_grp, lg, NEG)
    m1, i1 = first_max(le)
    le2 = jnp.where(lane_f == i1, NEG, le)
    m2, i2 = first_max(le2)
    e2 = jnp.exp(m2 - m1)
    den = 1.0 + e2
    w1 = p_grp * (1.0 / den)
    w2 = p_grp * (e2 / den)

    onehot = jnp.where((lane_f == i1) | (lane_f == i2), 1.0, 0.0)
    r_i = lax.broadcasted_iota(jnp.int32, (tm, tm), 0)
    c_i = lax.broadcasted_iota(jnp.int32, (tm, tm), 1)
    before = jnp.where(c_i < r_i, 1.0, 0.0).astype(BF16)
    base = cnt_ref[0:1, :] + _dot(before, onehot.astype(BF16))
    r1 = jnp.sum(jnp.where(lane_f == i1, base, 0.0), axis=1, keepdims=True)
    r2 = jnp.sum(jnp.where(lane_f == i2, base, 0.0), axis=1, keepdims=True)
    cnt_ref[0:1, :] = cnt_ref[0:1, :] + jnp.sum(onehot, axis=0, keepdims=True)
    counts_ref[...] = jnp.broadcast_to(cnt_ref[0:1, :], counts_ref.shape)

    route = jnp.zeros((tm, LANE), F32)
    for k, v in enumerate((i1, i2, w1, w2, r1, r2)):
        route = jnp.where(lane == k, v, route)
    route_ref[...] = route


def _outproj_router(yab, yc, wo_all, layer, x2d, nw, wr_hi, wr_lo, br, tm=512):
    T, D = x2d.shape
    full = lambda a: pl.BlockSpec(a.shape, lambda i, nd=a.ndim: (0,) * nd)
    return pl.pallas_call(
        _outproj_router_kernel,
        grid=(T // tm,),
        in_specs=[
            pl.BlockSpec((tm, yab.shape[1]), lambda i: (i, 0)),
            pl.BlockSpec((tm, yc.shape[1]), lambda i: (i, 0)),
            pl.BlockSpec((1,) + wo_all.shape[1:], lambda i: (layer, 0, 0)),
            pl.BlockSpec((tm, D), lambda i: (i, 0)),
            full(nw), full(wr_hi), full(wr_lo), full(br),
        ],
        out_specs=[pl.BlockSpec((tm, D), lambda i: (i, 0)),
                   pl.BlockSpec((tm, D // 2), lambda i: (i, 0)),
                   pl.BlockSpec((tm, LANE), lambda i: (i, 0)),
                   pl.BlockSpec((8, LANE), lambda i: (0, 0))],
        out_shape=[jax.ShapeDtypeStruct((T, D), F32),
                   jax.ShapeDtypeStruct((T, D // 2), jnp.uint32),
                   jax.ShapeDtypeStruct((T, LANE), F32),
                   jax.ShapeDtypeStruct((8, LANE), F32)],
        scratch_shapes=[pltpu.VMEM((8, LANE), F32)],
        compiler_params=_params(("arbitrary",)),
        name="outproj_router",
    )(yab, yc, wo_all, x2d, nw, wr_hi, wr_lo, br)


def _row_copy(src_ref, src_row, dst_ref, dst_row, sem):
    return pltpu.make_async_copy(src_ref.at[pl.ds(src_row, 1), :], dst_ref.at[pl.ds(dst_row, 1), :], sem)


def _dispatch_kernel(dest_ref, hnp_ref, xs_hbm, sem, *, chunk):
    i = pl.program_id(0)

    def body(j, carry):
        t = i * chunk + j
        _row_copy(hnp_ref, j, xs_hbm, dest_ref[2 * t], sem).start()
        _row_copy(hnp_ref, j, xs_hbm, dest_ref[2 * t + 1], sem).start()
        return carry

    lax.fori_loop(0, chunk, body, 0, unroll=8)
    for _ in range(2):
        pltpu.make_async_copy(hnp_ref, xs_hbm.at[pl.ds(0, chunk), :], sem).wait()


def _dispatch(dest, hnp, chunk=2048):
    T, W = hnp.shape
    return pl.pallas_call(
        functools.partial(_dispatch_kernel, chunk=chunk),
        grid_spec=pltpu.PrefetchScalarGridSpec(
            num_scalar_prefetch=1,
            grid=(T // chunk,),
            in_specs=[pl.BlockSpec((chunk, W), lambda i, d: (i, 0))],
            out_specs=pl.BlockSpec(memory_space=pl.ANY),
            scratch_shapes=[pltpu.SemaphoreType.DMA(())],
        ),
        out_shape=jax.ShapeDtypeStruct((2 * T, W), hnp.dtype),
        compiler_params=_params(("arbitrary",)),
        name="dispatch",
    )(dest, hnp)


FLAG_FIRST = 1
FLAG_LAST = 2
FLAG_NEW_EXPERT = 4
FLAG_SLOT = 8
FLAG_NEXT = 16


def _experts_kernel(tile_ref, exp_ref, lo_ref, hi_ref, flag_ref, xs_ref, wg_hbm, wu_hbm, wd_hbm,
                    ys_ref, acc_ref, wgb_ref, wub_ref, wdb_ref, wgf_ref, wuf_ref, wdf_ref, sem, *, tm, layer):
    w = pl.program_id(0)
    lo = lo_ref[w]
    hi = hi_ref[w]
    flags = flag_ref[w]
    slot = (flags // FLAG_SLOT) & 1
    next_e = flags // FLAG_NEXT - 1

    def weight_copies(expert, s):
        return [pltpu.make_async_copy(src.at[layer, expert], dst.at[s], sem.at[s])
                for src, dst in ((wg_hbm, wgf_ref), (wu_hbm, wuf_ref), (wd_hbm, wdf_ref))]

    @pl.when(w == 0)
    def _():
        for c in weight_copies(exp_ref[0], slot):
            c.start()

    @pl.when((flags & FLAG_NEW_EXPERT) != 0)
    def _():
        for c in weight_copies(exp_ref[w], slot):
            c.wait()
        wgb_ref[...] = wgf_ref[slot].astype(BF16)
        wub_ref[...] = wuf_ref[slot].astype(BF16)
        wdb_ref[...] = wdf_ref[slot].astype(BF16)

        @pl.when(next_e >= 0)
        def _():
            for c in weight_copies(next_e, 1 - slot):
                c.start()

    first = (flags & FLAG_FIRST) != 0

    @pl.when(hi > lo)
    def _():
        x = _unpack_rows(xs_ref[...], BF16)
        hg = _dot(x, wgb_ref[...])
        hu = _dot(x, wub_ref[...])
        row = tile_ref[w] * tm + lax.broadcasted_iota(jnp.int32, hg.shape, 0)
        h = jnp.where((row >= lo) & (row < hi), _silu(hg) * hu, 0.0).astype(BF16)

        @pl.when(first)
        def _():
            acc_ref[...] = _dot(h, wdb_ref[...])

        @pl.when(jnp.logical_not(first))
        def _():
            acc_ref[...] += _dot(h, wdb_ref[...])

    @pl.when((flags & FLAG_LAST) != 0)
    def _():
        ys_ref[...] = _pack_rows(acc_ref[...])


def _experts(meta, xs, wg_all, wu_all, wd_all, layer, tm):
    N, W = xs.shape
    _, E, D, F = wg_all.shape
    tile_w, exp_w, lo_w, hi_w, flag_w = meta
    n_work = tile_w.shape[0]
    hbm = pl.BlockSpec(memory_space=pl.ANY)
    return pl.pallas_call(
        functools.partial(_experts_kernel, tm=tm, layer=layer),
        grid_spec=pltpu.PrefetchScalarGridSpec(
            num_scalar_prefetch=5,
            grid=(n_work,),
            in_specs=[
                pl.BlockSpec((tm, W), lambda w, t, e, lo, hi, f: (t[w], 0)),
                hbm, hbm, hbm,
            ],
            out_specs=pl.BlockSpec((tm, W), lambda w, t, e, lo, hi, f: (t[w], 0)),
            scratch_shapes=[pltpu.VMEM((tm, D), F32), pltpu.VMEM((D, F), BF16),
                            pltpu.VMEM((D, F), BF16), pltpu.VMEM((F, D), BF16),
                            pltpu.VMEM((2, D, F), F32), pltpu.VMEM((2, D, F), F32),
                            pltpu.VMEM((2, F, D), F32), pltpu.SemaphoreType.DMA((2,))],
        ),
        out_shape=jax.ShapeDtypeStruct((N, W), xs.dtype),
        compiler_params=_params(("arbitrary",)),
        name="experts",
    )(tile_w, exp_w, lo_w, hi_w, flag_w, xs, wg_all, wu_all, wd_all)


def _work_items(counts, n_rows, tm):
    E = counts.shape[0]
    n_tiles = n_rows // tm
    n_work = n_tiles + E - 1
    start = jnp.cumsum(counts) - counts
    end = start + counts
    first_tile = start // tm
    last_tile = jnp.maximum(end - 1, 0) // tm
    n_e = jnp.where(counts > 0, last_tile - first_tile + 1, 0)
    wend = jnp.cumsum(n_e)
    wstart = wend - n_e
    total = wend[-1]
    w = jnp.arange(n_work, dtype=jnp.int32)
    wc = jnp.minimum(w, total - 1)
    ew = jnp.sum((wc[:, None] >= wend[None, :]).astype(jnp.int32), axis=1)
    tile_w = first_tile[ew] + (wc - wstart[ew])
    valid = w < total
    lo = jnp.where(valid, jnp.maximum(start[ew], tile_w * tm), 0)
    hi = jnp.where(valid, jnp.minimum(end[ew], (tile_w + 1) * tm), 0)
    prev_tile = jnp.concatenate([jnp.full((1,), -1, jnp.int32), tile_w[:-1]])
    next_tile = jnp.concatenate([tile_w[1:], jnp.full((1,), -1, jnp.int32)])
    prev_e = jnp.concatenate([jnp.full((1,), -1, jnp.int32), ew[:-1]])
    first = valid & (tile_w != prev_tile)
    last = valid & ((tile_w != next_tile) | (w == total - 1))
    new_e = ew != prev_e
    present = n_e > 0
    slot_e = (jnp.cumsum(present.astype(jnp.int32)) - 1) % 2
    ids = jnp.where(present, jnp.arange(E, dtype=jnp.int32), E)
    at_or_after = jnp.flip(lax.cummin(jnp.flip(ids)))
    next_e = jnp.concatenate([at_or_after[1:], jnp.full((1,), E, jnp.int32)])
    next_e = jnp.where(next_e < E, next_e, -1)
    i32 = lambda a: a.astype(jnp.int32)
    flags = (FLAG_FIRST * i32(first) + FLAG_LAST * i32(last) + FLAG_NEW_EXPERT * i32(new_e)
             + FLAG_SLOT * slot_e[ew] + FLAG_NEXT * (next_e[ew] + 1))
    return i32(tile_w), i32(ew), i32(lo), i32(hi), i32(flags)


def _combine_kernel(dest_ref, x1_ref, route_ref, ys_hbm, o_ref, buf_ref, sem, *, tt):
    i = pl.program_id(0)
    n = pl.num_programs(0)

    def issue(step, slot):
        def body(j, carry):
            t = step * tt + j
            _row_copy(ys_hbm, dest_ref[2 * t], buf_ref.at[slot, 0], j, sem.at[slot]).start()
            _row_copy(ys_hbm, dest_ref[2 * t + 1], buf_ref.at[slot, 1], j, sem.at[slot]).start()
            return carry
        lax.fori_loop(0, tt, body, 0, unroll=8)

    @pl.when(i == 0)
    def _():
        issue(0, 0)

    @pl.when(i + 1 < n)
    def _():
        issue(i + 1, (i + 1) % 2)

    slot = i % 2
    for k in range(2):
        pltpu.make_async_copy(ys_hbm.at[pl.ds(0, tt), :], buf_ref.at[slot, k], sem.at[slot]).wait()

    lane = lax.broadcasted_iota(jnp.int32, route_ref.shape, 1)
    route = route_ref[...]
    w0 = jnp.sum(jnp.where(lane == ROUTE_W, route, 0.0), axis=1, keepdims=True)
    w1 = jnp.sum(jnp.where(lane == ROUTE_W + 1, route, 0.0), axis=1, keepdims=True)
    y0 = _unpack_rows(buf_ref[slot, 0], F32)
    y1 = _unpack_rows(buf_ref[slot, 1], F32)
    o_ref[...] = x1_ref[...] + (w0 * y0 + w1 * y1)


def _combine(dest, x1, route, ys, tt=256):
    T, D = x1.shape
    W = ys.shape[1]
    return pl.pallas_call(
        functools.partial(_combine_kernel, tt=tt),
        grid_spec=pltpu.PrefetchScalarGridSpec(
            num_scalar_prefetch=1,
            grid=(T // tt,),
            in_specs=[
                pl.BlockSpec((tt, D), lambda i, d: (i, 0)),
                pl.BlockSpec((tt, LANE), lambda i, d: (i, 0)),
                pl.BlockSpec(memory_space=pl.ANY),
            ],
            out_specs=pl.BlockSpec((tt, D), lambda i, d: (i, 0)),
            scratch_shapes=[pltpu.VMEM((2, 2, tt, W), ys.dtype), pltpu.SemaphoreType.DMA((2,))],
        ),
        out_shape=jax.ShapeDtypeStruct((T, D), F32),
        compiler_params=_params(("arbitrary",)),
        name="combine",
    )(dest, x1, route, ys)


def _moe(x1, hnp, route, counts8, wg_all, wu_all, wd_all, layer, tm=512):
    T = x1.shape[0]
    E = wg_all.shape[1]
    counts = counts8[0, :E].astype(jnp.int32)
    eid = route[:, ROUTE_E:ROUTE_E + 2].astype(jnp.int32)
    rank = route[:, ROUTE_RANK:ROUTE_RANK + 2].astype(jnp.int32)
    start = jnp.cumsum(counts) - counts
    onehot = eid[..., None] == jnp.arange(E, dtype=jnp.int32)
    dest = (jnp.sum(jnp.where(onehot, start, 0), axis=-1) + rank).reshape(2 * T)
    xs = _dispatch(dest, hnp)
    ys = _experts(_work_items(counts, 2 * T, tm), xs, wg_all, wu_all, wd_all, layer, tm)
    return _combine(dest, x1, route, ys)


def kernel(x, norm1_w, w_in, pool_w, pool_scale, conv_w, cmp_pe_k, cmp_w1_k, cmp_w2_k, cmp_pe_v, cmp_w1_v, cmp_w2_v, q_norm_w, k_norm_w, w_out, norm2_w, router_grp_w, router_grp_b, router_exp_w, router_exp_b, exp_w_gate, exp_w_up, exp_w_down):
    B, S, D = x.shape
    depth = w_in.shape[0]
    T = B * S
    xf = x.reshape(T, D)

    def w1_pair(w1):
        return jnp.concatenate([w1[:CMP_STRIDE], w1[CMP_STRIDE:]], axis=-1).astype(BF16)

    def pe_rows(pe):
        return jnp.broadcast_to(pe.reshape(1, CMP_LEN * HEAD_DIM), (8, CMP_LEN * HEAD_DIM)).astype(BF16)

    w_in_all = w_in.astype(BF16)
    w_gate_all = jnp.pad(w_in[:, :, COL_GATE:], ((0, 0), (0, 0), (0, LANE - (D_IN - COL_GATE)))).astype(BF16)
    w_out_all = w_out.astype(BF16)

    for l in range(depth):
        z, zgate = _inproj(xf, norm1_w[l].reshape(1, D), w_in_all, w_gate_all, l)

        yab = _mix_ab(z, pool_w[l].astype(BF16), pool_scale[l].reshape(1, POOL_WIDTH), conv_w[l], B, S)

        kc, vc = _compress(
            z, w1_pair(cmp_w1_k[l]), w1_pair(cmp_w1_v[l]), pe_rows(cmp_pe_k[l]), pe_rows(cmp_pe_v[l]),
            cmp_w1_k[l].reshape(CMP_LEN * HEAD_DIM, CMP_HIDDEN).astype(BF16),
            cmp_w1_v[l].reshape(CMP_LEN * HEAD_DIM, CMP_HIDDEN).astype(BF16),
            cmp_w2_k[l].astype(BF16), cmp_w2_v[l].astype(BF16), k_norm_w[l], B, S)
        yc = _nsa(z, zgate, kc, vc, q_norm_w[l], k_norm_w[l], B, S)

        wr = jnp.concatenate([router_exp_w[l], router_grp_w[l]], axis=1)
        wr = jnp.pad(wr, ((0, 0), (0, LANE - wr.shape[1])))
        wr_hi = wr.astype(BF16)
        wr_lo = (wr - wr_hi.astype(F32)).astype(BF16)
        br = jnp.concatenate([router_exp_b[l], router_grp_b[l]])
        br = jnp.pad(br, (0, LANE - br.shape[0])).reshape(1, LANE)
        x1, hnp, route, counts8 = _outproj_router(yab, yc, w_out_all, l, xf, norm2_w[l].reshape(1, D),
                                                  wr_hi, wr_lo, br)

        xf = _moe(x1, hnp, route, counts8, exp_w_gate, exp_w_up, exp_w_down, l)
    return xf.reshape(B, S, D)
```

```python
import functools

import numpy as np
import jax
import jax.numpy as jnp
from jax import lax
from jax.experimental import pallas as pl
from jax.experimental.pallas import tpu as pltpu

F32 = jnp.float32
BF16 = jnp.bfloat16

POOL_WINDOWS = (2, 4, 8, 16)
LANE = 128
POOL_WIDTH = 512
CONV_WIDTH = 512
CONV_K = 3
NSA_WIDTH = 1024
HEAD_DIM = 128
NSA_KV_HEADS = 2
NSA_GROUP = 4
N_BRANCH = 3
CMP_LEN = 32
CMP_STRIDE = 16
CMP_HIDDEN = 256
SEL_LEN = 64
N_SELECT = 16
SEL_FORCE = 1.0e4
WINDOW = 512
N_GROUPS_MOE = 4
EXPERTS_PER_GROUP = 8
N_EXPERTS = 32
D_EXPERT = 256
EPS = 1e-6
NEG = -1e30
LOG2_E = 1.4426950408889634

COL_U = 0
COL_B = 512
COL_C = 1024
COL_V = 1536
COL_Q = 2048
COL_KC = 3072
COL_KS = 3584
COL_VS = 3840
COL_KW = 4096
COL_VW = 4352
COL_GATE = 4608
D_IN = 4632

VMEM_LIMIT = 56 * 1024 * 1024


def _params(sem):
    return pltpu.CompilerParams(dimension_semantics=sem, vmem_limit_bytes=VMEM_LIMIT)


def _rms(x, w):
    return x * lax.rsqrt(jnp.mean(x * x, axis=-1, keepdims=True) + EPS) * w


def _silu(x):
    return x / (1.0 + jnp.exp(-x))


def _dot(a, b):
    return jnp.dot(a, b, preferred_element_type=F32)


def _split3(a):
    hi = a.astype(BF16)
    r1 = a - hi.astype(F32)
    mid = r1.astype(BF16)
    lo = (r1 - mid.astype(F32)).astype(BF16)
    return hi, mid, lo


def _inproj_kernel(x_ref, nw_ref, w_ref, wgate_ref, o_ref, gate_ref, xn_ref):
    @pl.when(pl.program_id(1) == 0)
    def _():
        xn = _rms(x_ref[...], nw_ref[...]).astype(BF16)
        xn_ref[...] = xn
        gate_ref[...] = _dot(xn, wgate_ref[0])

    o_ref[...] = _dot(xn_ref[...], w_ref[0]).astype(o_ref.dtype)


def _inproj(x2d, nw, w_all, wgate_all, layer, tm=1024, tn=1536):
    T, D = x2d.shape
    N = COL_GATE
    assert N % tn == 0 and w_all.shape[2] >= N
    return pl.pallas_call(
        _inproj_kernel,
        grid=(T // tm, N // tn),
        in_specs=[
            pl.BlockSpec((tm, D), lambda i, j: (i, 0)),
            pl.BlockSpec((1, D), lambda i, j: (0, 0)),
            pl.BlockSpec((1, D, tn), lambda i, j: (layer, 0, j)),
            pl.BlockSpec((1, D, LANE), lambda i, j: (layer, 0, 0)),
        ],
        out_specs=[pl.BlockSpec((tm, tn), lambda i, j: (i, j)),
                   pl.BlockSpec((tm, LANE), lambda i, j: (i, 0))],
        out_shape=[jax.ShapeDtypeStruct((T, N), BF16), jax.ShapeDtypeStruct((T, LANE), F32)],
        scratch_shapes=[pltpu.VMEM((tm, D), BF16)],
        compiler_params=_params(("parallel", "arbitrary")),
        name="inproj",
    )(x2d, nw, w_all, wgate_all)


def _mix_ab_kernel(u_ref, b_ref, c_ref, v_ref, pw_ref, ps_ref, cw_ref, o_ref):
    S = u_ref.shape[0]
    row = lax.broadcasted_iota(jnp.int32, (S, LANE), 0)

    def shift(a, k):
        return jnp.where(row >= k, pltpu.roll(a, k, axis=0), 0.0)

    for g, w in enumerate(POOL_WINDOWS):
        sl = slice(g * LANE, (g + 1) * LANE)
        u = u_ref[:, sl].astype(F32)
        s = u
        k = 1
        while k < w:
            s = s + shift(s, k)
            k *= 2
        cnt = jnp.minimum(row + 1, w).astype(F32)
        mixed = (s / cnt - u).astype(BF16)
        y = _dot(mixed, pw_ref[g]) * ps_ref[:, sl]
        o_ref[:, sl] = y.astype(o_ref.dtype)

    for h in range(CONV_WIDTH // LANE):
        sl = slice(h * LANE, (h + 1) * LANE)
        u2 = c_ref[:, sl].astype(F32) * v_ref[:, sl].astype(F32)
        y = cw_ref[0:1, sl] * shift(u2, 2)
        y = y + cw_ref[1:2, sl] * shift(u2, 1)
        y = y + cw_ref[2:3, sl] * u2
        y = b_ref[:, sl].astype(F32) * y
        o_ref[:, POOL_WIDTH + h * LANE:POOL_WIDTH + (h + 1) * LANE] = y.astype(o_ref.dtype)


def _mix_ab(z, pool_w_bf16, pool_scale, conv_w, B, S):
    T = z.shape[0]
    blk = lambda c: pl.BlockSpec((S, 512), lambda b, c=c: (b, c))
    return pl.pallas_call(
        _mix_ab_kernel,
        grid=(B,),
        in_specs=[
            blk(COL_U // 512), blk(COL_B // 512), blk(COL_C // 512), blk(COL_V // 512),
            pl.BlockSpec((4, LANE, LANE), lambda b: (0, 0, 0)),
            pl.BlockSpec((1, POOL_WIDTH), lambda b: (0, 0)),
            pl.BlockSpec((CONV_K, CONV_WIDTH), lambda b: (0, 0)),
        ],
        out_specs=pl.BlockSpec((S, POOL_WIDTH + CONV_WIDTH), lambda b: (b, 0)),
        out_shape=jax.ShapeDtypeStruct((T, POOL_WIDTH + CONV_WIDTH), BF16),
        compiler_params=_params(("parallel",)),
        name="mix_ab",
    )(z, z, z, z, pool_w_bf16, pool_scale, conv_w)


def _compress_kernel(z_ref, w1k_ref, w1v_ref, pek_ref, pev_ref, w1kf_ref, w1vf_ref,
                     w2k_ref, w2v_ref, knw_ref, kc_ref, vc_ref, xf_ref):
    assert CMP_LEN == 2 * CMP_STRIDE
    n16 = z_ref.shape[0] // CMP_STRIDE
    for cg in range(z_ref.shape[1] // HEAD_DIM):
        xf_ref[cg] = z_ref[:, cg * HEAD_DIM:(cg + 1) * HEAD_DIM].astype(F32)
    streams = ((w1k_ref, pek_ref, w1kf_ref, w2k_ref, kc_ref),
               (w1v_ref, pev_ref, w1vf_ref, w2v_ref, vc_ref))
    for which, (w1_ref, pe_ref, w1f_ref, w2_ref, out_ref) in enumerate(streams):
        pe_term = _dot(pe_ref[...], w1f_ref[...])[0:1, :]
        for h in range(NSA_KV_HEADS):
            cg = which * NSA_KV_HEADS + h
            acc = None
            for l in range(CMP_STRIDE):
                rows = xf_ref[cg, pl.ds(l, n16, stride=CMP_STRIDE), :]
                part = _dot(rows.astype(BF16), w1_ref[l])
                acc = part if acc is None else acc + part
            first = acc[:, :CMP_HIDDEN]
            second = acc[:, CMP_HIDDEN:]
            hid = first + pltpu.roll(second, n16 - 1, axis=0) + pe_term
            out = _dot(_silu(hid).astype(BF16), w2_ref[...])
            if which == 0:
                out = _rms(out, knw_ref[0:1, :])
            out_ref[0, h] = out.astype(out_ref.dtype)


def _compress(z, w1k, w1v, pek8, pev8, w1kf, w1vf, w2k, w2v, knw, B, S):
    n16 = S // CMP_STRIDE
    full = lambda a: pl.BlockSpec(a.shape, lambda b, nd=a.ndim: (0,) * nd)
    out_sds = jax.ShapeDtypeStruct((B, NSA_KV_HEADS, n16, HEAD_DIM), BF16)
    out_spec = pl.BlockSpec((1, NSA_KV_HEADS, n16, HEAD_DIM), lambda b: (b, 0, 0, 0))
    return pl.pallas_call(
        _compress_kernel,
        grid=(B,),
        in_specs=[pl.BlockSpec((S, 512), lambda b: (b, COL_KC // 512)),
                  full(w1k), full(w1v), full(pek8), full(pev8), full(w1kf), full(w1vf),
                  full(w2k), full(w2v), full(knw)],
        out_specs=[out_spec, out_spec],
        out_shape=[out_sds, out_sds],
        scratch_shapes=[pltpu.VMEM((512 // HEAD_DIM, S, HEAD_DIM), F32)],
        compiler_params=_params(("parallel",)),
        name="compress",
    )(z, w1k, w1v, pek8, pev8, w1kf, w1vf, w2k, w2v, knw)


V_ROWS = HEAD_DIM + 16

def _nsa_kernel(zq_ref, zg_ref, ks_ref, vs_ref, kw_ref, vw_ref, kc_ref, vc_ref,
                qnw_ref, knw_ref, ovlt_ref, negexp_ref, wband_ref, o_ref,
                kaug_ref, kwn_ref, vst_ref, vwt_ref, vct_ref, gt_ref, m_ref, acc_ref,
                ocmp_ref, owin_ref, qaug_ref, sbuf_ref, *, tq):
    G = NSA_GROUP
    S = ks_ref.shape[0]
    hkv = pl.program_id(1)
    i = pl.program_id(2)
    tk = tq
    t0 = i * tq

    def transpose_to_bf16(a):
        return a.astype(F32).T.astype(BF16)

    @pl.when(i == 0)
    def _():
        kaug_ref[:, 0:HEAD_DIM] = _rms(ks_ref[...].astype(F32), knw_ref[1:2, :]).astype(BF16)
        kaug_ref[:, HEAD_DIM:] = negexp_ref[...]
        kwn_ref[0:WINDOW, :] = jnp.zeros((WINDOW, HEAD_DIM), BF16)
        kwn_ref[WINDOW:, :] = _rms(kw_ref[...].astype(F32), knw_ref[2:3, :]).astype(BF16)
        ones_row = jnp.where(lax.broadcasted_iota(jnp.int32, (V_ROWS - HEAD_DIM, tk), 0) == 0, 1.0, 0.0)
        for j in range(S // tk):
            vst_ref[j, 0:HEAD_DIM, :] = transpose_to_bf16(vs_ref[j * tk:(j + 1) * tk, :])
            vst_ref[j, HEAD_DIM:, :] = ones_row.astype(BF16)
        vwt_ref[0:HEAD_DIM, 0:WINDOW] = jnp.zeros((HEAD_DIM, WINDOW), BF16)
        for j in range(S // tk):
            vwt_ref[0:HEAD_DIM, WINDOW + j * tk:WINDOW + (j + 1) * tk] = transpose_to_bf16(
                vw_ref[j * tk:(j + 1) * tk, :])
        vwt_ref[HEAD_DIM:, :] = jnp.where(
            lax.broadcasted_iota(jnp.int32, (V_ROWS - HEAD_DIM, S + WINDOW), 0) == 0, 1.0, 0.0).astype(BF16)
        vct_ref[...] = transpose_to_bf16(vc_ref[0, 0])

    qts = []
    for g in range(G):
        qt = zq_ref[:, g * HEAD_DIM:(g + 1) * HEAD_DIM].astype(F32).T
        ms = jnp.mean(qt * qt, axis=0, keepdims=True)
        qt = qt * lax.rsqrt(ms + EPS) * qnw_ref[...] * (HEAD_DIM ** -0.5 * LOG2_E)
        qts.append(qt.astype(BF16))
    q4t = jnp.concatenate(qts, axis=1)
    per_head = tq // LANE
    n_ch = G * per_head
    cols = [slice(c * LANE, (c + 1) * LANE) for c in range(n_ch)]
    qcols = [slice((c % per_head) * LANE, (c % per_head + 1) * LANE) for c in range(n_ch)]

    wk = WINDOW + tq
    band = wband_ref.at[jnp.minimum(i, 1)]
    sw = _dot(kwn_ref[pl.ds(pl.multiple_of(t0, tq), wk), :], q4t)
    span = WINDOW + 2 * LANE
    for c in range(0, n_ch, 2):
        q0 = (c % per_head) * LANE
        ps = []
        for cc in (c, c + 1):
            sg = sw[q0:q0 + span, cols[cc]] + band[q0:q0 + span, qcols[cc]]
            ps.append(jnp.exp2(sg - jnp.max(sg, axis=0, keepdims=True)).astype(BF16))
        acc_w = _dot(vwt_ref[:, pl.ds(pl.multiple_of(t0 + q0, LANE), span)],
                     jnp.concatenate(ps, axis=1))
        both = slice(c * LANE, (c + 2) * LANE)
        owin_ref[:, both] = acc_w[0:HEAD_DIM] * (1.0 / acc_w[HEAD_DIM:HEAD_DIM + 1])

    sc = _dot(kc_ref[0, 0], q4t)
    n_sub = lax.broadcasted_iota(jnp.int32, (LANE, LANE), 0)
    pts = []
    psums = [None] * per_head
    for c in range(n_ch):
        t_lane = t0 + (c % per_head) * LANE + lax.broadcasted_iota(jnp.int32, (LANE, LANE), 1)
        cm = (n_sub * CMP_STRIDE + (CMP_LEN - 1)) <= t_lane
        sg = jnp.where(cm, sc[:, cols[c]], NEG)
        e = jnp.exp2(sg - jnp.max(sg, axis=0, keepdims=True))
        p = e * (1.0 / jnp.sum(e, axis=0, keepdims=True))
        p = jnp.where(cm, p, 0.0)
        pts.append(p.astype(BF16))
        k = c % per_head
        psums[k] = p if psums[k] is None else psums[k] + p
    ocmp_ref[...] = _dot(vct_ref[...], jnp.concatenate(pts, axis=1))

    hi, mid, lo = _split3(jnp.concatenate(psums, axis=1))
    imp = _dot(ovlt_ref[...], hi) + _dot(ovlt_ref[...], mid) + _dot(ovlt_ref[...], lo)
    n_sel = S // SEL_LEN
    n_top = min(N_SELECT, n_sel)
    j_sub = lax.broadcasted_iota(jnp.int32, (n_sel, tq), 0)
    t_sel = t0 + lax.broadcasted_iota(jnp.int32, (n_sel, tq), 1)
    forced = (j_sub == (t_sel // SEL_LEN)) | (j_sub == 0)
    valid = (j_sub * SEL_LEN) <= t_sel
    score = jnp.where(forced, SEL_FORCE, jnp.where(valid, imp[0:n_sel, :], -1.0))
    rank = jnp.zeros((n_sel, tq), F32)
    for c in range(n_sel):
        other = score[c:c + 1, :]
        beats = jnp.where(other > score, 1.0, jnp.where((other == score) & (j_sub > c), 1.0, 0.0))
        rank = rank + beats
    unsel = jnp.where(rank < n_top, 0.0, 1.0)
    unsel = jnp.concatenate([unsel, jnp.zeros((LANE - n_sel, tq), F32)], axis=0).astype(BF16)
    qaug_ref[0:HEAD_DIM, :] = q4t
    qaug_ref[HEAD_DIM:, :] = jnp.concatenate([unsel] * G, axis=1)

    m_ref[...] = jnp.full(m_ref.shape, NEG, F32)
    acc_ref[...] = jnp.zeros(acc_ref.shape, F32)

    def sel_scores(kt, slot):
        k0 = pl.multiple_of(kt * tk, tk)
        sbuf_ref[slot] = _dot(kaug_ref[pl.ds(k0, tk), :], qaug_ref[...])

    def sel_update(kt, slot, diagonal):
        vt = vst_ref[kt]
        scores = sbuf_ref.at[slot]
        for c in range(0, n_ch, 2):
            pair = (c, c + 1)
            n_k = (c % per_head + 2) * LANE if diagonal else tk
            ps, alphas = [], []
            for cc in pair:
                sg = scores[0:n_k, cols[cc]]
                if diagonal:
                    k_sub = lax.broadcasted_iota(jnp.int32, (n_k, LANE), 0)
                    q_lane = lax.broadcasted_iota(jnp.int32, (n_k, LANE), 1)
                    sg = jnp.where(k_sub <= q_lane + (cc % per_head) * LANE, sg, NEG)
                m_old = m_ref[:, cols[cc]]
                m_new = jnp.maximum(m_old, jnp.max(sg, axis=0, keepdims=True))
                alphas.append(jnp.exp2(m_old - m_new))
                ps.append(jnp.exp2(sg - m_new).astype(BF16))
                m_ref[:, cols[cc]] = m_new
            both = slice(c * LANE, (c + 2) * LANE)
            acc_ref[:, both] = (jnp.concatenate(alphas, axis=1) * acc_ref[:, both]
                                + _dot(vt[:, 0:n_k], jnp.concatenate(ps, axis=1)))

    def sel_body(kt, carry):
        sel_update(kt, kt % 2, False)
        sel_scores(kt + 1, (kt + 1) % 2)
        return carry

    sel_scores(0, 0)
    lax.fori_loop(0, i, sel_body, 0)
    sel_update(i, i % 2, True)

    gt_ref[...] = (1.0 / (1.0 + jnp.exp(-zg_ref[...].astype(F32)))).T
    n_heads = NSA_KV_HEADS * G
    for c in range(n_ch):
        g = c // per_head
        col = hkv * G + g
        gate = lambda branch: gt_ref[pl.ds(branch * n_heads + col, 1), :][:, qcols[c]]
        o_sel = acc_ref[0:HEAD_DIM, cols[c]] * (1.0 / acc_ref[HEAD_DIM:HEAD_DIM + 1, cols[c]])
        out = gate(0) * ocmp_ref[:, cols[c]] + gate(1) * o_sel + gate(2) * owin_ref[:, cols[c]]
        o_ref[qcols[c], g * HEAD_DIM:(g + 1) * HEAD_DIM] = out.T.astype(o_ref.dtype)


def _nsa(z, zgate, kc, vc, qnw, knw, B, S, tq=512):
    T = z.shape[0]
    nq = S // tq
    G = NSA_GROUP
    n_c = (S - CMP_LEN) // CMP_STRIDE + 1
    n_sel = S // SEL_LEN
    assert S % tq == 0 and n_sel <= LANE and n_sel % 8 == 0 and n_c <= LANE and WINDOW % tq == 0
    assert S >= WINDOW + tq
    ci = np.arange(LANE)[None, :] * CMP_STRIDE
    sj = np.arange(LANE)[:, None] * SEL_LEN
    ovlt = ((ci < sj + SEL_LEN) & (ci + CMP_LEN > sj) & (np.arange(LANE)[None, :] < n_c)
            & (np.arange(LANE)[:, None] < n_sel))
    ovlt = jnp.asarray(ovlt.astype(np.float32), BF16)
    in_block = (np.arange(S)[:, None] // SEL_LEN) == np.arange(LANE)[None, :]
    negexp = jnp.asarray(in_block.astype(np.float32) * NEG, BF16)
    qnw_b = jnp.broadcast_to(qnw.reshape(HEAD_DIM, 1), (HEAD_DIM, tq))
    r_idx = np.arange(WINDOW + tq)[:, None]
    q_idx = np.arange(tq)[None, :]
    in_band = (r_idx > q_idx) & (r_idx <= q_idx + WINDOW)
    wband = jnp.asarray(np.stack([np.where(in_band & (r_idx >= WINDOW), 0.0, NEG),
                                  np.where(in_band, 0.0, NEG)]), F32)

    kvblk = lambda c: pl.BlockSpec((S, HEAD_DIM), lambda b, h, i, c=c: (b, c + h))
    cblk = pl.BlockSpec((1, 1, LANE, HEAD_DIM), lambda b, h, i: (b, h, 0, 0))
    full = lambda a: pl.BlockSpec(a.shape, lambda b, h, i, nd=a.ndim: (0,) * nd)
    return pl.pallas_call(
        functools.partial(_nsa_kernel, tq=tq),
        grid=(B, NSA_KV_HEADS, nq),
        in_specs=[
            pl.BlockSpec((tq, G * HEAD_DIM), lambda b, h, i: (b * nq + i, COL_Q // 512 + h)),
            pl.BlockSpec((tq, LANE), lambda b, h, i: (b * nq + i, 0)),
            kvblk(COL_KS // LANE), kvblk(COL_VS // LANE), kvblk(COL_KW // LANE), kvblk(COL_VW // LANE),
            cblk, cblk, full(qnw_b), full(knw), full(ovlt), full(negexp), full(wband),
        ],
        out_specs=pl.BlockSpec((tq, G * HEAD_DIM), lambda b, h, i: (b * nq + i, h)),
        out_shape=jax.ShapeDtypeStruct((T, NSA_WIDTH), BF16),
        scratch_shapes=[
            pltpu.VMEM((S, 2 * HEAD_DIM), BF16), pltpu.VMEM((S + WINDOW, HEAD_DIM), BF16),
            pltpu.VMEM((S // tq, V_ROWS, tq), BF16), pltpu.VMEM((V_ROWS, S + WINDOW), BF16),
            pltpu.VMEM((HEAD_DIM, LANE), BF16), pltpu.VMEM((LANE, tq), F32),
            pltpu.VMEM((1, G * tq), F32), pltpu.VMEM((V_ROWS, G * tq), F32),
            pltpu.VMEM((HEAD_DIM, G * tq), F32), pltpu.VMEM((HEAD_DIM, G * tq), F32),
            pltpu.VMEM((2 * HEAD_DIM, G * tq), BF16), pltpu.VMEM((2, tq, G * tq), F32),
        ],
        compiler_params=_params(("parallel", "parallel", "arbitrary")),
        name="nsa",
    )(z, zgate, z, z, z, z, kc, vc, qnw_b, knw, ovlt, negexp, wband)


def _pack_rows(x):
    w = x.shape[1] // 2
    return pltpu.pack_elementwise([x[:, :w], x[:, w:]], packed_dtype=BF16)


def _unpack_rows(p, dtype):
    lo = pltpu.unpack_elementwise(p, index=0, packed_dtype=BF16, unpacked_dtype=F32)
    hi = pltpu.unpack_elementwise(p, index=1, packed_dtype=BF16, unpacked_dtype=F32)
    return jnp.concatenate([lo.astype(dtype), hi.astype(dtype)], axis=1)


ROUTE_E = 0
ROUTE_W = 2
ROUTE_RANK = 4


def _outproj_router_kernel(yab_ref, yc_ref, wo_ref, x_ref, nw_ref, wrh_ref, wrl_ref, br_ref,
                           x1_ref, hnp_ref, route_ref, counts_ref, cnt_ref):
    @pl.when(pl.program_id(0) == 0)
    def _():
        cnt_ref[...] = jnp.zeros_like(cnt_ref)

    ka = yab_ref.shape[1]
    acc = _dot(yab_ref[...], wo_ref[0, 0:ka, :]) + _dot(yc_ref[...], wo_ref[0, ka:, :])
    x1 = x_ref[...] + acc
    x1_ref[...] = x1
    hn = _rms(x1, nw_ref[...])
    hnp_ref[...] = _pack_rows(hn)
    hi = hn.astype(BF16)
    lo = (hn - hi.astype(F32)).astype(BF16)
    lg = _dot(hi, wrh_ref[...]) + _dot(lo, wrh_ref[...]) + _dot(hi, wrl_ref[...]) + br_ref[...]

    tm = lg.shape[0]
    lane = lax.broadcasted_iota(jnp.int32, (tm, LANE), 1)
    lane_f = lane.astype(F32)
    big = float(LANE)

    def first_max(v):
        m = jnp.max(v, axis=1, keepdims=True)
        idx = jnp.min(jnp.where(v == m, lane_f, big), axis=1, keepdims=True)
        return m, idx

    is_grp = (lane >= N_EXPERTS) & (lane < N_EXPERTS + N_GROUPS_MOE)
    lgm = jnp.where(is_grp, lg, NEG)
    mg, grp_lane = first_max(lgm)
    p_grp = 1.0 / jnp.sum(jnp.where(is_grp, jnp.exp(lgm - mg), 0.0), axis=1, keepdims=True)
    grp = grp_lane - float(N_EXPERTS)
    in_grp = (lane < N_EXPERTS) & ((lane // EXPERTS_PER_GROUP).astype(F32) == grp)
    le = jnp.where(in_grp, lg, NEG)
    m1, i1 = first_max(le)
    le2 = jnp.where(lane_f == i1, NEG, le)
    m2, i2 = first_max(le2)
    e2 = jnp.exp(m2 - m1)
    den = 1.0 + e2
    w1 = p_grp * (1.0 / den)
    w2 = p_grp * (e2 / den)

    onehot = jnp.where((lane_f == i1) | (lane_f == i2), 1.0, 0.0)
    r_i = lax.broadcasted_iota(jnp.int32, (tm, tm), 0)
    c_i = lax.broadcasted_iota(jnp.int32, (tm, tm), 1)
    before = jnp.where(c_i < r_i, 1.0, 0.0).astype(BF16)
    base = cnt_ref[0:1, :] + _dot(before, onehot.astype(BF16))
    r1 = jnp.sum(jnp.where(lane_f == i1, base, 0.0), axis=1, keepdims=True)
    r2 = jnp.sum(jnp.where(lane_f == i2, base, 0.0), axis=1, keepdims=True)
    cnt_ref[0:1, :] = cnt_ref[0:1, :] + jnp.sum(onehot, axis=0, keepdims=True)
    counts_ref[...] = jnp.broadcast_to(cnt_ref[0:1, :], counts_ref.shape)

    route = jnp.zeros((tm, LANE), F32)
    for k, v in enumerate((i1, i2, w1, w2, r1, r2)):
        route = jnp.where(lane == k, v, route)
    route_ref[...] = route


def _outproj_router(yab, yc, wo_all, layer, x2d, nw, wr_hi, wr_lo, br, tm=512):
    T, D = x2d.shape
    full = lambda a: pl.BlockSpec(a.shape, lambda i, nd=a.ndim: (0,) * nd)
    return pl.pallas_call(
        _outproj_router_kernel,
        grid=(T // tm,),
        in_specs=[
            pl.BlockSpec((tm, yab.shape[1]), lambda i: (i, 0)),
            pl.BlockSpec((tm, yc.shape[1]), lambda i: (i, 0)),
            pl.BlockSpec((1,) + wo_all.shape[1:], lambda i: (layer, 0, 0)),
            pl.BlockSpec((tm, D), lambda i: (i, 0)),
            full(nw), full(wr_hi), full(wr_lo), full(br),
        ],
        out_specs=[pl.BlockSpec((tm, D), lambda i: (i, 0)),
                   pl.BlockSpec((tm, D // 2), lambda i: (i, 0)),
                   pl.BlockSpec((tm, LANE), lambda i: (i, 0)),
                   pl.BlockSpec((8, LANE), lambda i: (0, 0))],
        out_shape=[jax.ShapeDtypeStruct((T, D), F32),
                   jax.ShapeDtypeStruct((T, D // 2), jnp.uint32),
                   jax.ShapeDtypeStruct((T, LANE), F32),
                   jax.ShapeDtypeStruct((8, LANE), F32)],
        scratch_shapes=[pltpu.VMEM((8, LANE), F32)],
        compiler_params=_params(("arbitrary",)),
        name="outproj_router",
    )(yab, yc, wo_all, x2d, nw, wr_hi, wr_lo, br)


def _row_copy(src_ref, src_row, dst_ref, dst_row, sem):
    return pltpu.make_async_copy(src_ref.at[pl.ds(src_row, 1), :], dst_ref.at[pl.ds(dst_row, 1), :], sem)


def _dispatch_kernel(dest_ref, hnp_ref, xs_hbm, sem, *, chunk):
    i = pl.program_id(0)

    def body(j, carry):
        t = i * chunk + j
        _row_copy(hnp_ref, j, xs_hbm, dest_ref[2 * t], sem).start()
        _row_copy(hnp_ref, j, xs_hbm, dest_ref[2 * t + 1], sem).start()
        return carry

    lax.fori_loop(0, chunk, body, 0, unroll=8)
    for _ in range(2):
        pltpu.make_async_copy(hnp_ref, xs_hbm.at[pl.ds(0, chunk), :], sem).wait()


def _dispatch(dest, hnp, chunk=2048):
    T, W = hnp.shape
    return pl.pallas_call(
        functools.partial(_dispatch_kernel, chunk=chunk),
        grid_spec=pltpu.PrefetchScalarGridSpec(
            num_scalar_prefetch=1,
            grid=(T // chunk,),
            in_specs=[pl.BlockSpec((chunk, W), lambda i, d: (i, 0))],
            out_specs=pl.BlockSpec(memory_space=pl.ANY),
            scratch_shapes=[pltpu.SemaphoreType.DMA(())],
        ),
        out_shape=jax.ShapeDtypeStruct((2 * T, W), hnp.dtype),
        compiler_params=_params(("arbitrary",)),
        name="dispatch",
    )(dest, hnp)


FLAG_FIRST = 1
FLAG_LAST = 2
FLAG_NEW_EXPERT = 4
FLAG_SLOT = 8
FLAG_NEXT = 16


def _experts_kernel(tile_ref, exp_ref, lo_ref, hi_ref, flag_ref, xs_ref, wg_hbm, wu_hbm, wd_hbm,
                    ys_ref, acc_ref, wgb_ref, wub_ref, wdb_ref, wgf_ref, wuf_ref, wdf_ref, sem, *, tm, layer):
    w = pl.program_id(0)
    lo = lo_ref[w]
    hi = hi_ref[w]
    flags = flag_ref[w]
    slot = (flags // FLAG_SLOT) & 1
    next_e = flags // FLAG_NEXT - 1

    def weight_copies(expert, s):
        return [pltpu.make_async_copy(src.at[layer, expert], dst.at[s], sem.at[s])
                for src, dst in ((wg_hbm, wgf_ref), (wu_hbm, wuf_ref), (wd_hbm, wdf_ref))]

    @pl.when(w == 0)
    def _():
        for c in weight_copies(exp_ref[0], slot):
            c.start()

    @pl.when((flags & FLAG_NEW_EXPERT) != 0)
    def _():
        for c in weight_copies(exp_ref[w], slot):
            c.wait()
        wgb_ref[...] = wgf_ref[slot].astype(BF16)
        wub_ref[...] = wuf_ref[slot].astype(BF16)
        wdb_ref[...] = wdf_ref[slot].astype(BF16)

        @pl.when(next_e >= 0)
        def _():
            for c in weight_copies(next_e, 1 - slot):
                c.start()

    first = (flags & FLAG_FIRST) != 0

    def ffn(rows, skipped):
        r0 = tile_ref[w] * tm + rows.start
        x = _unpack_rows(xs_ref[rows, :], BF16)
        hg = _dot(x, wgb_ref[...])
        hu = _dot(x, wub_ref[...])
        row = r0 + lax.broadcasted_iota(jnp.int32, hg.shape, 0)
        h = jnp.where((row >= lo) & (row < hi), _silu(hg) * hu, 0.0).astype(BF16)

        @pl.when(first)
        def _():
            acc_ref[rows, :] = _dot(h, wdb_ref[...])
            if skipped is not None:
                acc_ref[skipped, :] = jnp.zeros((skipped.stop - skipped.start, acc_ref.shape[1]), F32)

        @pl.when(jnp.logical_not(first))
        def _():
            acc_ref[rows, :] += _dot(h, wdb_ref[...])

    half = tm // 2
    mid = tile_ref[w] * tm + half
    lower, upper = slice(0, half), slice(half, tm)
    in_lower = lo < mid
    in_upper = hi > mid

    @pl.when((hi > lo) & in_lower & in_upper)
    def _():
        ffn(slice(0, tm), None)

    @pl.when((hi > lo) & in_lower & jnp.logical_not(in_upper))
    def _():
        ffn(lower, upper)

    @pl.when((hi > lo) & in_upper & jnp.logical_not(in_lower))
    def _():
        ffn(upper, lower)

    @pl.when((flags & FLAG_LAST) != 0)
    def _():
        ys_ref[...] = _pack_rows(acc_ref[...])


def _experts(meta, xs, wg_all, wu_all, wd_all, layer, tm):
    N, W = xs.shape
    _, E, D, F = wg_all.shape
    tile_w, exp_w, lo_w, hi_w, flag_w = meta
    n_work = tile_w.shape[0]
    hbm = pl.BlockSpec(memory_space=pl.ANY)
    return pl.pallas_call(
        functools.partial(_experts_kernel, tm=tm, layer=layer),
        grid_spec=pltpu.PrefetchScalarGridSpec(
            num_scalar_prefetch=5,
            grid=(n_work,),
            in_specs=[
                pl.BlockSpec((tm, W), lambda w, t, e, lo, hi, f: (t[w], 0)),
                hbm, hbm, hbm,
            ],
            out_specs=pl.BlockSpec((tm, W), lambda w, t, e, lo, hi, f: (t[w], 0)),
            scratch_shapes=[pltpu.VMEM((tm, D), F32), pltpu.VMEM((D, F), BF16),
                            pltpu.VMEM((D, F), BF16), pltpu.VMEM((F, D), BF16),
                            pltpu.VMEM((2, D, F), F32), pltpu.VMEM((2, D, F), F32),
                            pltpu.VMEM((2, F, D), F32), pltpu.SemaphoreType.DMA((2,))],
        ),
        out_shape=jax.ShapeDtypeStruct((N, W), xs.dtype),
        compiler_params=_params(("arbitrary",)),
        name="experts",
    )(tile_w, exp_w, lo_w, hi_w, flag_w, xs, wg_all, wu_all, wd_all)


def _work_items(counts, n_rows, tm):
    E = counts.shape[0]
    n_tiles = n_rows // tm
    n_work = n_tiles + E - 1
    start = jnp.cumsum(counts) - counts
    end = start + counts
    first_tile = start // tm
    last_tile = jnp.maximum(end - 1, 0) // tm
    n_e = jnp.where(counts > 0, last_tile - first_tile + 1, 0)
    wend = jnp.cumsum(n_e)
    wstart = wend - n_e
    total = wend[-1]
    w = jnp.arange(n_work, dtype=jnp.int32)
    wc = jnp.minimum(w, total - 1)
    ew = jnp.sum((wc[:, None] >= wend[None, :]).astype(jnp.int32), axis=1)
    tile_w = first_tile[ew] + (wc - wstart[ew])
    valid = w < total
    lo = jnp.where(valid, jnp.maximum(start[ew], tile_w * tm), 0)
    hi = jnp.where(valid, jnp.minimum(end[ew], (tile_w + 1) * tm), 0)
    prev_tile = jnp.concatenate([jnp.full((1,), -1, jnp.int32), tile_w[:-1]])
    next_tile = jnp.concatenate([tile_w[1:], jnp.full((1,), -1, jnp.int32)])
    prev_e = jnp.concatenate([jnp.full((1,), -1, jnp.int32), ew[:-1]])
    first = valid & (tile_w != prev_tile)
    last = valid & ((tile_w != next_tile) | (w == total - 1))
    new_e = ew != prev_e
    present = n_e > 0
    slot_e = (jnp.cumsum(present.astype(jnp.int32)) - 1) % 2
    ids = jnp.where(present, jnp.arange(E, dtype=jnp.int32), E)
    at_or_after = jnp.flip(lax.cummin(jnp.flip(ids)))
    next_e = jnp.concatenate([at_or_after[1:], jnp.full((1,), E, jnp.int32)])
    next_e = jnp.where(next_e < E, next_e, -1)
    i32 = lambda a: a.astype(jnp.int32)
    flags = (FLAG_FIRST * i32(first) + FLAG_LAST * i32(last) + FLAG_NEW_EXPERT * i32(new_e)
             + FLAG_SLOT * slot_e[ew] + FLAG_NEXT * (next_e[ew] + 1))
    return i32(tile_w), i32(ew), i32(lo), i32(hi), i32(flags)


def _combine_kernel(dest_ref, x1_ref, route_ref, ys_hbm, o_ref, buf_ref, sem, *, tt):
    i = pl.program_id(0)
    n = pl.num_programs(0)

    def issue(step, slot):
        def body(j, carry):
            t = step * tt + j
            _row_copy(ys_hbm, dest_ref[2 * t], buf_ref.at[slot, 0], j, sem.at[slot]).start()
            _row_copy(ys_hbm, dest_ref[2 * t + 1], buf_ref.at[slot, 1], j, sem.at[slot]).start()
            return carry
        lax.fori_loop(0, tt, body, 0, unroll=8)

    @pl.when(i == 0)
    def _():
        issue(0, 0)

    @pl.when(i + 1 < n)
    def _():
        issue(i + 1, (i + 1) % 2)

    slot = i % 2
    for k in range(2):
        pltpu.make_async_copy(ys_hbm.at[pl.ds(0, tt), :], buf_ref.at[slot, k], sem.at[slot]).wait()

    lane = lax.broadcasted_iota(jnp.int32, route_ref.shape, 1)
    route = route_ref[...]
    w0 = jnp.sum(jnp.where(lane == ROUTE_W, route, 0.0), axis=1, keepdims=True)
    w1 = jnp.sum(jnp.where(lane == ROUTE_W + 1, route, 0.0), axis=1, keepdims=True)
    y0 = _unpack_rows(buf_ref[slot, 0], F32)
    y1 = _unpack_rows(buf_ref[slot, 1], F32)
    o_ref[...] = x1_ref[...] + (w0 * y0 + w1 * y1)


def _combine(dest, x1, route, ys, tt=256):
    T, D = x1.shape
    W = ys.shape[1]
    return pl.pallas_call(
        functools.partial(_combine_kernel, tt=tt),
        grid_spec=pltpu.PrefetchScalarGridSpec(
            num_scalar_prefetch=1,
            grid=(T // tt,),
            in_specs=[
                pl.BlockSpec((tt, D), lambda i, d: (i, 0)),
                pl.BlockSpec((tt, LANE), lambda i, d: (i, 0)),
                pl.BlockSpec(memory_space=pl.ANY),
            ],
            out_specs=pl.BlockSpec((tt, D), lambda i, d: (i, 0)),
            scratch_shapes=[pltpu.VMEM((2, 2, tt, W), ys.dtype), pltpu.SemaphoreType.DMA((2,))],
        ),
        out_shape=jax.ShapeDtypeStruct((T, D), F32),
        compiler_params=_params(("arbitrary",)),
        name="combine",
    )(dest, x1, route, ys)


def _moe(x1, hnp, route, counts8, wg_all, wu_all, wd_all, layer, tm=512):
    T = x1.shape[0]
    E = wg_all.shape[1]
    counts = counts8[0, :E].astype(jnp.int32)
    eid = route[:, ROUTE_E:ROUTE_E + 2].astype(jnp.int32)
    rank = route[:, ROUTE_RANK:ROUTE_RANK + 2].astype(jnp.int32)
    start = jnp.cumsum(counts) - counts
    onehot = eid[..., None] == jnp.arange(E, dtype=jnp.int32)
    dest = (jnp.sum(jnp.where(onehot, start, 0), axis=-1) + rank).reshape(2 * T)
    xs = _dispatch(dest, hnp)
    ys = _experts(_work_items(counts, 2 * T, tm), xs, wg_all, wu_all, wd_all, layer, tm)
    return _combine(dest, x1, route, ys)


def kernel(x, norm1_w, w_in, pool_w, pool_scale, conv_w, cmp_pe_k, cmp_w1_k, cmp_w2_k, cmp_pe_v, cmp_w1_v, cmp_w2_v, q_norm_w, k_norm_w, w_out, norm2_w, router_grp_w, router_grp_b, router_exp_w, router_exp_b, exp_w_gate, exp_w_up, exp_w_down):
    B, S, D = x.shape
    depth = w_in.shape[0]
    T = B * S
    xf = x.reshape(T, D)

    def w1_pair(w1):
        return jnp.concatenate([w1[:CMP_STRIDE], w1[CMP_STRIDE:]], axis=-1).astype(BF16)

    def pe_rows(pe):
        return jnp.broadcast_to(pe.reshape(1, CMP_LEN * HEAD_DIM), (8, CMP_LEN * HEAD_DIM)).astype(BF16)

    w_in_all = w_in.astype(BF16)
    w_gate_all = jnp.pad(w_in[:, :, COL_GATE:], ((0, 0), (0, 0), (0, LANE - (D_IN - COL_GATE)))).astype(BF16)
    w_out_all = w_out.astype(BF16)

    for l in range(depth):
        z, zgate = _inproj(xf, norm1_w[l].reshape(1, D), w_in_all, w_gate_all, l)

        yab = _mix_ab(z, pool_w[l].astype(BF16), pool_scale[l].reshape(1, POOL_WIDTH), conv_w[l], B, S)

        kc, vc = _compress(
            z, w1_pair(cmp_w1_k[l]), w1_pair(cmp_w1_v[l]), pe_rows(cmp_pe_k[l]), pe_rows(cmp_pe_v[l]),
            cmp_w1_k[l].reshape(CMP_LEN * HEAD_DIM, CMP_HIDDEN).astype(BF16),
            cmp_w1_v[l].reshape(CMP_LEN * HEAD_DIM, CMP_HIDDEN).astype(BF16),
            cmp_w2_k[l].astype(BF16), cmp_w2_v[l].astype(BF16), k_norm_w[l], B, S)
        yc = _nsa(z, zgate, kc, vc, q_norm_w[l], k_norm_w[l], B, S)

        wr = jnp.concatenate([router_exp_w[l], router_grp_w[l]], axis=1)
        wr = jnp.pad(wr, ((0, 0), (0, LANE - wr.shape[1])))
        wr_hi = wr.astype(BF16)
        wr_lo = (wr - wr_hi.astype(F32)).astype(BF16)
        br = jnp.concatenate([router_exp_b[l], router_grp_b[l]])
        br = jnp.pad(br, (0, LANE - br.shape[0])).reshape(1, LANE)
        x1, hnp, route, counts8 = _outproj_router(yab, yc, w_out_all, l, xf, norm2_w[l].reshape(1, D),
                                                  wr_hi, wr_lo, br)

        xf = _moe(x1, hnp, route, counts8, exp_w_gate, exp_w_up, exp_w_down, l)
    return xf.reshape(B, S, D)
```

```python
import functools

import numpy as np
import jax
import jax.numpy as jnp
from jax import lax
from jax.experimental import pallas as pl
from jax.experimental.pallas import tpu as pltpu

F32 = jnp.float32
BF16 = jnp.bfloat16

POOL_WINDOWS = (2, 4, 8, 16)
LANE = 128
POOL_WIDTH = 512
CONV_WIDTH = 512
CONV_K = 3
NSA_WIDTH = 1024
HEAD_DIM = 128
NSA_KV_HEADS = 2
NSA_GROUP = 4
N_BRANCH = 3
CMP_LEN = 32
CMP_STRIDE = 16
CMP_HIDDEN = 256
SEL_LEN = 64
N_SELECT = 16
SEL_FORCE = 1.0e4
WINDOW = 512
N_GROUPS_MOE = 4
EXPERTS_PER_GROUP = 8
N_EXPERTS = 32
D_EXPERT = 256
EPS = 1e-6
NEG = -1e30
LOG2_E = 1.4426950408889634

COL_U = 0
COL_B = 512
COL_C = 1024
COL_V = 1536
COL_Q = 2048
COL_KC = 3072
COL_KS = 3584
COL_VS = 3840
COL_KW = 4096
COL_VW = 4352
COL_GATE = 4608
D_IN = 4632

VMEM_LIMIT = 56 * 1024 * 1024


def _params(sem):
    return pltpu.CompilerParams(dimension_semantics=sem, vmem_limit_bytes=VMEM_LIMIT)


def _rms(x, w):
    return x * lax.rsqrt(jnp.mean(x * x, axis=-1, keepdims=True) + EPS) * w


def _silu(x):
    return x / (1.0 + jnp.exp(-x))


def _dot(a, b):
    return jnp.dot(a, b, preferred_element_type=F32)


def _split3(a):
    hi = a.astype(BF16)
    r1 = a - hi.astype(F32)
    mid = r1.astype(BF16)
    lo = (r1 - mid.astype(F32)).astype(BF16)
    return hi, mid, lo


def _inproj_kernel(x_ref, nw_ref, w_ref, wgate_ref, o_ref, gate_ref, xn_ref):
    @pl.when(pl.program_id(1) == 0)
    def _():
        xn = _rms(x_ref[...], nw_ref[...]).astype(BF16)
        xn_ref[...] = xn
        gate_ref[...] = _dot(xn, wgate_ref[0])

    o_ref[...] = _dot(xn_ref[...], w_ref[0]).astype(o_ref.dtype)


def _inproj(x2d, nw, w_all, wgate_all, layer, tm=1024, tn=1536):
    T, D = x2d.shape
    N = COL_GATE
    assert N % tn == 0 and w_all.shape[2] >= N
    return pl.pallas_call(
        _inproj_kernel,
        grid=(T // tm, N // tn),
        in_specs=[
            pl.BlockSpec((tm, D), lambda i, j: (i, 0)),
            pl.BlockSpec((1, D), lambda i, j: (0, 0)),
            pl.BlockSpec((1, D, tn), lambda i, j: (layer, 0, j)),
            pl.BlockSpec((1, D, LANE), lambda i, j: (layer, 0, 0)),
        ],
        out_specs=[pl.BlockSpec((tm, tn), lambda i, j: (i, j)),
                   pl.BlockSpec((tm, LANE), lambda i, j: (i, 0))],
        out_shape=[jax.ShapeDtypeStruct((T, N), BF16), jax.ShapeDtypeStruct((T, LANE), F32)],
        scratch_shapes=[pltpu.VMEM((tm, D), BF16)],
        compiler_params=_params(("parallel", "arbitrary")),
        name="inproj",
    )(x2d, nw, w_all, wgate_all)


def _mix_ab_kernel(u_ref, b_ref, c_ref, v_ref, pw_ref, ps_ref, cw_ref, o_ref):
    S = u_ref.shape[0]
    row = lax.broadcasted_iota(jnp.int32, (S, LANE), 0)

    def shift(a, k):
        return jnp.where(row >= k, pltpu.roll(a, k, axis=0), 0.0)

    for g, w in enumerate(POOL_WINDOWS):
        sl = slice(g * LANE, (g + 1) * LANE)
        u = u_ref[:, sl].astype(F32)
        s = u
        k = 1
        while k < w:
            s = s + shift(s, k)
            k *= 2
        cnt = jnp.minimum(row + 1, w).astype(F32)
        mixed = (s / cnt - u).astype(BF16)
        y = _dot(mixed, pw_ref[g]) * ps_ref[:, sl]
        o_ref[:, sl] = y.astype(o_ref.dtype)

    for h in range(CONV_WIDTH // LANE):
        sl = slice(h * LANE, (h + 1) * LANE)
        u2 = c_ref[:, sl].astype(F32) * v_ref[:, sl].astype(F32)
        y = cw_ref[0:1, sl] * shift(u2, 2)
        y = y + cw_ref[1:2, sl] * shift(u2, 1)
        y = y + cw_ref[2:3, sl] * u2
        y = b_ref[:, sl].astype(F32) * y
        o_ref[:, POOL_WIDTH + h * LANE:POOL_WIDTH + (h + 1) * LANE] = y.astype(o_ref.dtype)


def _mix_ab(z, pool_w_bf16, pool_scale, conv_w, B, S):
    T = z.shape[0]
    blk = lambda c: pl.BlockSpec((S, 512), lambda b, c=c: (b, c))
    return pl.pallas_call(
        _mix_ab_kernel,
        grid=(B,),
        in_specs=[
            blk(COL_U // 512), blk(COL_B // 512), blk(COL_C // 512), blk(COL_V // 512),
            pl.BlockSpec((4, LANE, LANE), lambda b: (0, 0, 0)),
            pl.BlockSpec((1, POOL_WIDTH), lambda b: (0, 0)),
            pl.BlockSpec((CONV_K, CONV_WIDTH), lambda b: (0, 0)),
        ],
        out_specs=pl.BlockSpec((S, POOL_WIDTH + CONV_WIDTH), lambda b: (b, 0)),
        out_shape=jax.ShapeDtypeStruct((T, POOL_WIDTH + CONV_WIDTH), BF16),
        compiler_params=_params(("parallel",)),
        name="mix_ab",
    )(z, z, z, z, pool_w_bf16, pool_scale, conv_w)


def _compress_kernel(z_ref, w1k_ref, w1v_ref, pek_ref, pev_ref, w1kf_ref, w1vf_ref,
                     w2k_ref, w2v_ref, knw_ref, kc_ref, vc_ref, xf_ref):
    assert CMP_LEN == 2 * CMP_STRIDE
    n16 = z_ref.shape[0] // CMP_STRIDE
    for cg in range(z_ref.shape[1] // HEAD_DIM):
        xf_ref[cg] = z_ref[:, cg * HEAD_DIM:(cg + 1) * HEAD_DIM].astype(F32)
    streams = ((w1k_ref, pek_ref, w1kf_ref, w2k_ref, kc_ref),
               (w1v_ref, pev_ref, w1vf_ref, w2v_ref, vc_ref))
    for which, (w1_ref, pe_ref, w1f_ref, w2_ref, out_ref) in enumerate(streams):
        pe_term = _dot(pe_ref[...], w1f_ref[...])[0:1, :]
        for h in range(NSA_KV_HEADS):
            cg = which * NSA_KV_HEADS + h
            acc = None
            for l in range(CMP_STRIDE):
                rows = xf_ref[cg, pl.ds(l, n16, stride=CMP_STRIDE), :]
                part = _dot(rows.astype(BF16), w1_ref[l])
                acc = part if acc is None else acc + part
            first = acc[:, :CMP_HIDDEN]
            second = acc[:, CMP_HIDDEN:]
            hid = first + pltpu.roll(second, n16 - 1, axis=0) + pe_term
            out = _dot(_silu(hid).astype(BF16), w2_ref[...])
            if which == 0:
                out = _rms(out, knw_ref[0:1, :])
            out_ref[0, h] = out.astype(out_ref.dtype)


def _compress(z, w1k, w1v, pek8, pev8, w1kf, w1vf, w2k, w2v, knw, B, S):
    n16 = S // CMP_STRIDE
    full = lambda a: pl.BlockSpec(a.shape, lambda b, nd=a.ndim: (0,) * nd)
    out_sds = jax.ShapeDtypeStruct((B, NSA_KV_HEADS, n16, HEAD_DIM), BF16)
    out_spec = pl.BlockSpec((1, NSA_KV_HEADS, n16, HEAD_DIM), lambda b: (b, 0, 0, 0))
    return pl.pallas_call(
        _compress_kernel,
        grid=(B,),
        in_specs=[pl.BlockSpec((S, 512), lambda b: (b, COL_KC // 512)),
                  full(w1k), full(w1v), full(pek8), full(pev8), full(w1kf), full(w1vf),
                  full(w2k), full(w2v), full(knw)],
        out_specs=[out_spec, out_spec],
        out_shape=[out_sds, out_sds],
        scratch_shapes=[pltpu.VMEM((512 // HEAD_DIM, S, HEAD_DIM), F32)],
        compiler_params=_params(("parallel",)),
        name="compress",
    )(z, w1k, w1v, pek8, pev8, w1kf, w1vf, w2k, w2v, knw)


V_ROWS = HEAD_DIM + 16

def _nsa_kernel(zq_ref, zg_ref, ks_ref, vs_ref, kw_ref, vw_ref, kc_ref, vc_ref,
                qnw_ref, knw_ref, ovlt_ref, negexp_ref, wband_ref, o_ref,
                kaug_ref, kwn_ref, vst_ref, vwt_ref, vct_ref, gt_ref, m_ref, acc_ref,
                ocmp_ref, owin_ref, qaug_ref, sbuf_ref, *, tq):
    G = NSA_GROUP
    S = ks_ref.shape[0]
    hkv = pl.program_id(1)
    i = pl.program_id(2)
    tk = tq
    t0 = i * tq

    def transpose_to_bf16(a):
        return a.astype(F32).T.astype(BF16)

    @pl.when(i == 0)
    def _():
        kaug_ref[:, 0:HEAD_DIM] = _rms(ks_ref[...].astype(F32), knw_ref[1:2, :]).astype(BF16)
        kaug_ref[:, HEAD_DIM:] = negexp_ref[...]
        kwn_ref[0:WINDOW, :] = jnp.zeros((WINDOW, HEAD_DIM), BF16)
        kwn_ref[WINDOW:, :] = _rms(kw_ref[...].astype(F32), knw_ref[2:3, :]).astype(BF16)
        ones_row = jnp.where(lax.broadcasted_iota(jnp.int32, (V_ROWS - HEAD_DIM, tk), 0) == 0, 1.0, 0.0)
        for j in range(S // tk):
            vst_ref[j, 0:HEAD_DIM, :] = transpose_to_bf16(vs_ref[j * tk:(j + 1) * tk, :])
            vst_ref[j, HEAD_DIM:, :] = ones_row.astype(BF16)
        vwt_ref[0:HEAD_DIM, 0:WINDOW] = jnp.zeros((HEAD_DIM, WINDOW), BF16)
        for j in range(S // tk):
            vwt_ref[0:HEAD_DIM, WINDOW + j * tk:WINDOW + (j + 1) * tk] = transpose_to_bf16(
                vw_ref[j * tk:(j + 1) * tk, :])
        vwt_ref[HEAD_DIM:, :] = jnp.where(
            lax.broadcasted_iota(jnp.int32, (V_ROWS - HEAD_DIM, S + WINDOW), 0) == 0, 1.0, 0.0).astype(BF16)
        vct_ref[...] = transpose_to_bf16(vc_ref[0, 0])

    qts = []
    for g in range(G):
        qt = zq_ref[:, g * HEAD_DIM:(g + 1) * HEAD_DIM].astype(F32).T
        ms = jnp.mean(qt * qt, axis=0, keepdims=True)
        qt = qt * lax.rsqrt(ms + EPS) * qnw_ref[...] * (HEAD_DIM ** -0.5 * LOG2_E)
        qts.append(qt.astype(BF16))
    q4t = jnp.concatenate(qts, axis=1)
    per_head = tq // LANE
    n_ch = G * per_head
    cols = [slice(c * LANE, (c + 1) * LANE) for c in range(n_ch)]
    qcols = [slice((c % per_head) * LANE, (c % per_head + 1) * LANE) for c in range(n_ch)]

    wk = WINDOW + tq
    band = wband_ref.at[jnp.minimum(i, 1)]
    sw = _dot(kwn_ref[pl.ds(pl.multiple_of(t0, tq), wk), :], q4t)
    span = WINDOW + 2 * LANE

    def window_pair(c):
        q0 = (c % per_head) * LANE
        ps = []
        for cc in (c, c + 1):
            sg = sw[q0:q0 + span, cols[cc]] + band[q0:q0 + span, qcols[cc]]
            ps.append(jnp.exp2(sg - jnp.max(sg, axis=0, keepdims=True)).astype(BF16))
        acc_w = _dot(vwt_ref[:, pl.ds(pl.multiple_of(t0 + q0, LANE), span)],
                     jnp.concatenate(ps, axis=1))
        both = slice(c * LANE, (c + 2) * LANE)
        owin_ref[:, both] = acc_w[0:HEAD_DIM] * (1.0 / acc_w[HEAD_DIM:HEAD_DIM + 1])

    pairs = list(range(0, n_ch, 2))
    stage = lambda k: [window_pair(c) for c in pairs[k::4]]

    sc = _dot(kc_ref[0, 0], q4t)
    stage(0)
    n_sub = lax.broadcasted_iota(jnp.int32, (LANE, LANE), 0)
    pts = []
    psums = [None] * per_head
    for c in range(n_ch):
        t_lane = t0 + (c % per_head) * LANE + lax.broadcasted_iota(jnp.int32, (LANE, LANE), 1)
        cm = (n_sub * CMP_STRIDE + (CMP_LEN - 1)) <= t_lane
        sg = jnp.where(cm, sc[:, cols[c]], NEG)
        e = jnp.exp2(sg - jnp.max(sg, axis=0, keepdims=True))
        p = e * (1.0 / jnp.sum(e, axis=0, keepdims=True))
        p = jnp.where(cm, p, 0.0)
        pts.append(p.astype(BF16))
        k = c % per_head
        psums[k] = p if psums[k] is None else psums[k] + p
    ocmp_ref[...] = _dot(vct_ref[...], jnp.concatenate(pts, axis=1))

    hi, mid, lo = _split3(jnp.concatenate(psums, axis=1))
    imp = _dot(ovlt_ref[...], hi) + _dot(ovlt_ref[...], mid) + _dot(ovlt_ref[...], lo)
    stage(1)
    n_sel = S // SEL_LEN
    n_top = min(N_SELECT, n_sel)
    j_sub = lax.broadcasted_iota(jnp.int32, (n_sel, tq), 0)
    t_sel = t0 + lax.broadcasted_iota(jnp.int32, (n_sel, tq), 1)
    forced = (j_sub == (t_sel // SEL_LEN)) | (j_sub == 0)
    valid = (j_sub * SEL_LEN) <= t_sel
    score = jnp.where(forced, SEL_FORCE, jnp.where(valid, imp[0:n_sel, :], -1.0))
    rank = jnp.zeros((n_sel, tq), F32)
    for c in range(n_sel):
        other = score[c:c + 1, :]
        beats = jnp.where(other > score, 1.0, jnp.where((other == score) & (j_sub > c), 1.0, 0.0))
        rank = rank + beats
    unsel = jnp.where(rank < n_top, 0.0, 1.0)
    unsel = jnp.concatenate([unsel, jnp.zeros((LANE - n_sel, tq), F32)], axis=0).astype(BF16)
    qaug_ref[0:HEAD_DIM, :] = q4t
    qaug_ref[HEAD_DIM:, :] = jnp.concatenate([unsel] * G, axis=1)
    stage(2)

    m_ref[...] = jnp.full(m_ref.shape, NEG, F32)
    acc_ref[...] = jnp.zeros(acc_ref.shape, F32)

    def sel_scores(kt, slot):
        k0 = pl.multiple_of(kt * tk, tk)
        sbuf_ref[slot] = _dot(kaug_ref[pl.ds(k0, tk), :], qaug_ref[...])

    def sel_update(kt, slot, diagonal):
        vt = vst_ref[kt]
        scores = sbuf_ref.at[slot]
        for c in range(0, n_ch, 2):
            pair = (c, c + 1)
            n_k = (c % per_head + 2) * LANE if diagonal else tk
            ps, alphas = [], []
            for cc in pair:
                sg = scores[0:n_k, cols[cc]]
                if diagonal:
                    k_sub = lax.broadcasted_iota(jnp.int32, (n_k, LANE), 0)
                    q_lane = lax.broadcasted_iota(jnp.int32, (n_k, LANE), 1)
                    sg = jnp.where(k_sub <= q_lane + (cc % per_head) * LANE, sg, NEG)
                m_old = m_ref[:, cols[cc]]
                m_new = jnp.maximum(m_old, jnp.max(sg, axis=0, keepdims=True))
                alphas.append(jnp.exp2(m_old - m_new))
                ps.append(jnp.exp2(sg - m_new).astype(BF16))
                m_ref[:, cols[cc]] = m_new
            both = slice(c * LANE, (c + 2) * LANE)
            acc_ref[:, both] = (jnp.concatenate(alphas, axis=1) * acc_ref[:, both]
                                + _dot(vt[:, 0:n_k], jnp.concatenate(ps, axis=1)))

    def sel_body(kt, carry):
        sel_update(kt, kt % 2, False)
        sel_scores(kt + 1, (kt + 1) % 2)
        return carry

    sel_scores(0, 0)
    stage(3)
    lax.fori_loop(0, i, sel_body, 0)
    sel_update(i, i % 2, True)

    gt_ref[...] = (1.0 / (1.0 + jnp.exp(-zg_ref[...].astype(F32)))).T
    n_heads = NSA_KV_HEADS * G
    for c in range(n_ch):
        g = c // per_head
        col = hkv * G + g
        gate = lambda branch: gt_ref[pl.ds(branch * n_heads + col, 1), :][:, qcols[c]]
        o_sel = acc_ref[0:HEAD_DIM, cols[c]] * (1.0 / acc_ref[HEAD_DIM:HEAD_DIM + 1, cols[c]])
        out = gate(0) * ocmp_ref[:, cols[c]] + gate(1) * o_sel + gate(2) * owin_ref[:, cols[c]]
        o_ref[qcols[c], g * HEAD_DIM:(g + 1) * HEAD_DIM] = out.T.astype(o_ref.dtype)


def _nsa(z, zgate, kc, vc, qnw, knw, B, S, tq=512):
    T = z.shape[0]
    nq = S // tq
    G = NSA_GROUP
    n_c = (S - CMP_LEN) // CMP_STRIDE + 1
    n_sel = S // SEL_LEN
    assert S % tq == 0 and n_sel <= LANE and n_sel % 8 == 0 and n_c <= LANE and WINDOW % tq == 0
    assert S >= WINDOW + tq
    ci = np.arange(LANE)[None, :] * CMP_STRIDE
    sj = np.arange(LANE)[:, None] * SEL_LEN
    ovlt = ((ci < sj + SEL_LEN) & (ci + CMP_LEN > sj) & (np.arange(LANE)[None, :] < n_c)
            & (np.arange(LANE)[:, None] < n_sel))
    ovlt = jnp.asarray(ovlt.astype(np.float32), BF16)
    in_block = (np.arange(S)[:, None] // SEL_LEN) == np.arange(LANE)[None, :]
    negexp = jnp.asarray(in_block.astype(np.float32) * NEG, BF16)
    qnw_b = jnp.broadcast_to(qnw.reshape(HEAD_DIM, 1), (HEAD_DIM, tq))
    r_idx = np.arange(WINDOW + tq)[:, None]
    q_idx = np.arange(tq)[None, :]
    in_band = (r_idx > q_idx) & (r_idx <= q_idx + WINDOW)
    wband = jnp.asarray(np.stack([np.where(in_band & (r_idx >= WINDOW), 0.0, NEG),
                                  np.where(in_band, 0.0, NEG)]), F32)

    kvblk = lambda c: pl.BlockSpec((S, HEAD_DIM), lambda b, h, i, c=c: (b, c + h))
    cblk = pl.BlockSpec((1, 1, LANE, HEAD_DIM), lambda b, h, i: (b, h, 0, 0))
    full = lambda a: pl.BlockSpec(a.shape, lambda b, h, i, nd=a.ndim: (0,) * nd)
    return pl.pallas_call(
        functools.partial(_nsa_kernel, tq=tq),
        grid=(B, NSA_KV_HEADS, nq),
        in_specs=[
            pl.BlockSpec((tq, G * HEAD_DIM), lambda b, h, i: (b * nq + i, COL_Q // 512 + h)),
            pl.BlockSpec((tq, LANE), lambda b, h, i: (b * nq + i, 0)),
            kvblk(COL_KS // LANE), kvblk(COL_VS // LANE), kvblk(COL_KW // LANE), kvblk(COL_VW // LANE),
            cblk, cblk, full(qnw_b), full(knw), full(ovlt), full(negexp), full(wband),
        ],
        out_specs=pl.BlockSpec((tq, G * HEAD_DIM), lambda b, h, i: (b * nq + i, h)),
        out_shape=jax.ShapeDtypeStruct((T, NSA_WIDTH), BF16),
        scratch_shapes=[
            pltpu.VMEM((S, 2 * HEAD_DIM), BF16), pltpu.VMEM((S + WINDOW, HEAD_DIM), BF16),
            pltpu.VMEM((S // tq, V_ROWS, tq), BF16), pltpu.VMEM((V_ROWS, S + WINDOW), BF16),
            pltpu.VMEM((HEAD_DIM, LANE), BF16), pltpu.VMEM((LANE, tq), F32),
            pltpu.VMEM((1, G * tq), F32), pltpu.VMEM((V_ROWS, G * tq), F32),
            pltpu.VMEM((HEAD_DIM, G * tq), F32), pltpu.VMEM((HEAD_DIM, G * tq), F32),
            pltpu.VMEM((2 * HEAD_DIM, G * tq), BF16), pltpu.VMEM((2, tq, G * tq), F32),
        ],
        compiler_params=_params(("parallel", "parallel", "arbitrary")),
        name="nsa",
    )(z, zgate, z, z, z, z, kc, vc, qnw_b, knw, ovlt, negexp, wband)


def _pack_rows(x):
    w = x.shape[1] // 2
    return pltpu.pack_elementwise([x[:, :w], x[:, w:]], packed_dtype=BF16)


def _unpack_rows(p, dtype):
    lo = pltpu.unpack_elementwise(p, index=0, packed_dtype=BF16, unpacked_dtype=F32)
    hi = pltpu.unpack_elementwise(p, index=1, packed_dtype=BF16, unpacked_dtype=F32)
    return jnp.concatenate([lo.astype(dtype), hi.astype(dtype)], axis=1)


ROUTE_E = 0
ROUTE_W = 2
ROUTE_RANK = 4


def _outproj_router_kernel(yab_ref, yc_ref, wo_ref, x_ref, nw_ref, wrh_ref, wrl_ref, br_ref,
                           x1_ref, hnp_ref, route_ref, counts_ref, cnt_ref):
    @pl.when(pl.program_id(0) == 0)
    def _():
        cnt_ref[...] = jnp.zeros_like(cnt_ref)

    ka = yab_ref.shape[1]
    acc = _dot(yab_ref[...], wo_ref[0, 0:ka, :]) + _dot(yc_ref[...], wo_ref[0, ka:, :])
    x1 = x_ref[...] + acc
    x1_ref[...] = x1
    hn = _rms(x1, nw_ref[...])
    hnp_ref[...] = _pack_rows(hn)
    hi = hn.astype(BF16)
    lo = (hn - hi.astype(F32)).astype(BF16)
    lg = _dot(hi, wrh_ref[...]) + _dot(lo, wrh_ref[...]) + _dot(hi, wrl_ref[...]) + br_ref[...]

    tm = lg.shape[0]
    lane = lax.broadcasted_iota(jnp.int32, (tm, LANE), 1)
    lane_f = lane.astype(F32)
    big = float(LANE)

    def first_max(v):
        m = jnp.max(v, axis=1, keepdims=True)
        idx = jnp.min(jnp.where(v == m, lane_f, big), axis=1, keepdims=True)
        return m, idx

    is_grp = (lane >= N_EXPERTS) & (lane < N_EXPERTS + N_GROUPS_MOE)
    lgm = jnp.where(is_grp, lg, NEG)
    mg, grp_lane = first_max(lgm)
    p_grp = 1.0 / jnp.sum(jnp.where(is_grp, jnp.exp(lgm - mg), 0.0), axis=1, keepdims=True)
    grp = grp_lane - float(N_EXPERTS)
    in_grp = (lane < N_EXPERTS) & ((lane // EXPERTS_PER_GROUP).astype(F32) == grp)
    le = jnp.where(in_grp, lg, NEG)
    m1, i1 = first_max(le)
    le2 = jnp.where(lane_f == i1, NEG, le)
    m2, i2 = first_max(le2)
    e2 = jnp.exp(m2 - m1)
    den = 1.0 + e2
    w1 = p_grp * (1.0 / den)
    w2 = p_grp * (e2 / den)

    onehot = jnp.where((lane_f == i1) | (lane_f == i2), 1.0, 0.0)
    r_i = lax.broadcasted_iota(jnp.int32, (tm, tm), 0)
    c_i = lax.broadcasted_iota(jnp.int32, (tm, tm), 1)
    before = jnp.where(c_i < r_i, 1.0, 0.0).astype(BF16)
    base = cnt_ref[0:1, :] + _dot(before, onehot.astype(BF16))
    r1 = jnp.sum(jnp.where(lane_f == i1, base, 0.0), axis=1, keepdims=True)
    r2 = jnp.sum(jnp.where(lane_f == i2, base, 0.0), axis=1, keepdims=True)
    cnt_ref[0:1, :] = cnt_ref[0:1, :] + jnp.sum(onehot, axis=0, keepdims=True)
    counts_ref[...] = jnp.broadcast_to(cnt_ref[0:1, :], counts_ref.shape)

    route = jnp.zeros((tm, LANE), F32)
    for k, v in enumerate((i1, i2, w1, w2, r1, r2)):
        route = jnp.where(lane == k, v, route)
    route_ref[...] = route


def _outproj_router(yab, yc, wo_all, layer, x2d, nw, wr_hi, wr_lo, br, tm=512):
    T, D = x2d.shape
    full = lambda a: pl.BlockSpec(a.shape, lambda i, nd=a.ndim: (0,) * nd)
    return pl.pallas_call(
        _outproj_router_kernel,
        grid=(T // tm,),
        in_specs=[
            pl.BlockSpec((tm, yab.shape[1]), lambda i: (i, 0)),
            pl.BlockSpec((tm, yc.shape[1]), lambda i: (i, 0)),
            pl.BlockSpec((1,) + wo_all.shape[1:], lambda i: (layer, 0, 0)),
            pl.BlockSpec((tm, D), lambda i: (i, 0)),
            full(nw), full(wr_hi), full(wr_lo), full(br),
        ],
        out_specs=[pl.BlockSpec((tm, D), lambda i: (i, 0)),
                   pl.BlockSpec((tm, D // 2), lambda i: (i, 0)),
                   pl.BlockSpec((tm, LANE), lambda i: (i, 0)),
                   pl.BlockSpec((8, LANE), lambda i: (0, 0))],
        out_shape=[jax.ShapeDtypeStruct((T, D), F32),
                   jax.ShapeDtypeStruct((T, D // 2), jnp.uint32),
                   jax.ShapeDtypeStruct((T, LANE), F32),
                   jax.ShapeDtypeStruct((8, LANE), F32)],
        scratch_shapes=[pltpu.VMEM((8, LANE), F32)],
        compiler_params=_params(("arbitrary",)),
        name="outproj_router",
    )(yab, yc, wo_all, x2d, nw, wr_hi, wr_lo, br)


def _row_copy(src_ref, src_row, dst_ref, dst_row, sem):
    return pltpu.make_async_copy(src_ref.at[pl.ds(src_row, 1), :], dst_ref.at[pl.ds(dst_row, 1), :], sem)


def _dispatch_kernel(dest_ref, hnp_ref, xs_hbm, sem, *, chunk):
    i = pl.program_id(0)

    def body(j, carry):
        t = i * chunk + j
        _row_copy(hnp_ref, j, xs_hbm, dest_ref[2 * t], sem).start()
        _row_copy(hnp_ref, j, xs_hbm, dest_ref[2 * t + 1], sem).start()
        return carry

    lax.fori_loop(0, chunk, body, 0, unroll=8)
    for _ in range(2):
        pltpu.make_async_copy(hnp_ref, xs_hbm.at[pl.ds(0, chunk), :], sem).wait()


def _dispatch(dest, hnp, chunk=2048):
    T, W = hnp.shape
    return pl.pallas_call(
        functools.partial(_dispatch_kernel, chunk=chunk),
        grid_spec=pltpu.PrefetchScalarGridSpec(
            num_scalar_prefetch=1,
            grid=(T // chunk,),
            in_specs=[pl.BlockSpec((chunk, W), lambda i, d: (i, 0))],
            out_specs=pl.BlockSpec(memory_space=pl.ANY),
            scratch_shapes=[pltpu.SemaphoreType.DMA(())],
        ),
        out_shape=jax.ShapeDtypeStruct((2 * T, W), hnp.dtype),
        compiler_params=_params(("arbitrary",)),
        name="dispatch",
    )(dest, hnp)


FLAG_FIRST = 1
FLAG_LAST = 2
FLAG_NEW_EXPERT = 4
FLAG_SLOT = 8
FLAG_NEXT = 16


def _experts_kernel(tile_ref, exp_ref, lo_ref, hi_ref, flag_ref, xs_ref, wg_hbm, wu_hbm, wd_hbm,
                    ys_ref, acc_ref, wgb_ref, wub_ref, wdb_ref, wgf_ref, wuf_ref, wdf_ref, sem, *, tm, layer):
    w = pl.program_id(0)
    lo = lo_ref[w]
    hi = hi_ref[w]
    flags = flag_ref[w]
    slot = (flags // FLAG_SLOT) & 1
    next_e = flags // FLAG_NEXT - 1

    def weight_copies(expert, s):
        return [pltpu.make_async_copy(src.at[layer, expert], dst.at[s], sem.at[s])
                for src, dst in ((wg_hbm, wgf_ref), (wu_hbm, wuf_ref), (wd_hbm, wdf_ref))]

    @pl.when(w == 0)
    def _():
        for c in weight_copies(exp_ref[0], slot):
            c.start()

    @pl.when((flags & FLAG_NEW_EXPERT) != 0)
    def _():
        for c in weight_copies(exp_ref[w], slot):
            c.wait()
        wgb_ref[...] = wgf_ref[slot].astype(BF16)
        wub_ref[...] = wuf_ref[slot].astype(BF16)
        wdb_ref[...] = wdf_ref[slot].astype(BF16)

        @pl.when(next_e >= 0)
        def _():
            for c in weight_copies(next_e, 1 - slot):
                c.start()

    first = (flags & FLAG_FIRST) != 0

    def ffn(rows, skipped):
        r0 = tile_ref[w] * tm + rows.start
        x = _unpack_rows(xs_ref[rows, :], BF16)
        hg = _dot(x, wgb_ref[...])
        hu = _dot(x, wub_ref[...])
        row = r0 + lax.broadcasted_iota(jnp.int32, hg.shape, 0)
        h = jnp.where((row >= lo) & (row < hi), _silu(hg) * hu, 0.0).astype(BF16)

        @pl.when(first)
        def _():
            acc_ref[rows, :] = _dot(h, wdb_ref[...])
            if skipped is not None:
                acc_ref[skipped, :] = jnp.zeros((skipped.stop - skipped.start, acc_ref.shape[1]), F32)

        @pl.when(jnp.logical_not(first))
        def _():
            acc_ref[rows, :] += _dot(h, wdb_ref[...])

    half = tm // 2
    mid = tile_ref[w] * tm + half
    lower, upper = slice(0, half), slice(half, tm)
    in_lower = lo < mid
    in_upper = hi > mid

    @pl.when((hi > lo) & in_lower & in_upper)
    def _():
        ffn(slice(0, tm), None)

    @pl.when((hi > lo) & in_lower & jnp.logical_not(in_upper))
    def _():
        ffn(lower, upper)

    @pl.when((hi > lo) & in_upper & jnp.logical_not(in_lower))
    def _():
        ffn(upper, lower)

    @pl.when((flags & FLAG_LAST) != 0)
    def _():
        ys_ref[...] = _pack_rows(acc_ref[...])


def _experts(meta, xs, wg_all, wu_all, wd_all, layer, tm):
    N, W = xs.shape
    _, E, D, F = wg_all.shape
    tile_w, exp_w, lo_w, hi_w, flag_w = meta
    n_work = tile_w.shape[0]
    hbm = pl.BlockSpec(memory_space=pl.ANY)
    return pl.pallas_call(
        functools.partial(_experts_kernel, tm=tm, layer=layer),
        grid_spec=pltpu.PrefetchScalarGridSpec(
            num_scalar_prefetch=5,
            grid=(n_work,),
            in_specs=[
                pl.BlockSpec((tm, W), lambda w, t, e, lo, hi, f: (t[w], 0)),
                hbm, hbm, hbm,
            ],
            out_specs=pl.BlockSpec((tm, W), lambda w, t, e, lo, hi, f: (t[w], 0)),
            scratch_shapes=[pltpu.VMEM((tm, D), F32), pltpu.VMEM((D, F), BF16),
                            pltpu.VMEM((D, F), BF16), pltpu.VMEM((F, D), BF16),
                            pltpu.VMEM((2, D, F), F32), pltpu.VMEM((2, D, F), F32),
                            pltpu.VMEM((2, F, D), F32), pltpu.SemaphoreType.DMA((2,))],
        ),
        out_shape=jax.ShapeDtypeStruct((N, W), xs.dtype),
        compiler_params=_params(("arbitrary",)),
        name="experts",
    )(tile_w, exp_w, lo_w, hi_w, flag_w, xs, wg_all, wu_all, wd_all)


def _work_items(counts, n_rows, tm):
    E = counts.shape[0]
    n_tiles = n_rows // tm
    n_work = n_tiles + E - 1
    start = jnp.cumsum(counts) - counts
    end = start + counts
    first_tile = start // tm
    last_tile = jnp.maximum(end - 1, 0) // tm
    n_e = jnp.where(counts > 0, last_tile - first_tile + 1, 0)
    wend = jnp.cumsum(n_e)
    wstart = wend - n_e
    total = wend[-1]
    w = jnp.arange(n_work, dtype=jnp.int32)
    wc = jnp.minimum(w, total - 1)
    ew = jnp.sum((wc[:, None] >= wend[None, :]).astype(jnp.int32), axis=1)
    tile_w = first_tile[ew] + (wc - wstart[ew])
    valid = w < total
    lo = jnp.where(valid, jnp.maximum(start[ew], tile_w * tm), 0)
    hi = jnp.where(valid, jnp.minimum(end[ew], (tile_w + 1) * tm), 0)
    prev_tile = jnp.concatenate([jnp.full((1,), -1, jnp.int32), tile_w[:-1]])
    next_tile = jnp.concatenate([tile_w[1:], jnp.full((1,), -1, jnp.int32)])
    prev_e = jnp.concatenate([jnp.full((1,), -1, jnp.int32), ew[:-1]])
    first = valid & (tile_w != prev_tile)
    last = valid & ((tile_w != next_tile) | (w == total - 1))
    new_e = ew != prev_e
    present = n_e > 0
    slot_e = (jnp.cumsum(present.astype(jnp.int32)) - 1) % 2
    ids = jnp.where(present, jnp.arange(E, dtype=jnp.int32), E)
    at_or_after = jnp.flip(lax.cummin(jnp.flip(ids)))
    next_e = jnp.concatenate([at_or_after[1:], jnp.full((1,), E, jnp.int32)])
    next_e = jnp.where(next_e < E, next_e, -1)
    i32 = lambda a: a.astype(jnp.int32)
    flags = (FLAG_FIRST * i32(first) + FLAG_LAST * i32(last) + FLAG_NEW_EXPERT * i32(new_e)
             + FLAG_SLOT * slot_e[ew] + FLAG_NEXT * (next_e[ew] + 1))
    return i32(tile_w), i32(ew), i32(lo), i32(hi), i32(flags)


def _combine_kernel(dest_ref, x1_ref, route_ref, ys_hbm, o_ref, buf_ref, sem, *, tt):
    i = pl.program_id(0)
    n = pl.num_programs(0)

    def issue(step, slot):
        def body(j, carry):
            t = step * tt + j
            _row_copy(ys_hbm, dest_ref[2 * t], buf_ref.at[slot, 0], j, sem.at[slot]).start()
            _row_copy(ys_hbm, dest_ref[2 * t + 1], buf_ref.at[slot, 1], j, sem.at[slot]).start()
            return carry
        lax.fori_loop(0, tt, body, 0, unroll=8)

    @pl.when(i == 0)
    def _():
        issue(0, 0)

    @pl.when(i + 1 < n)
    def _():
        issue(i + 1, (i + 1) % 2)

    slot = i % 2
    for k in range(2):
        pltpu.make_async_copy(ys_hbm.at[pl.ds(0, tt), :], buf_ref.at[slot, k], sem.at[slot]).wait()

    lane = lax.broadcasted_iota(jnp.int32, route_ref.shape, 1)
    route = route_ref[...]
    w0 = jnp.sum(jnp.where(lane == ROUTE_W, route, 0.0), axis=1, keepdims=True)
    w1 = jnp.sum(jnp.where(lane == ROUTE_W + 1, route, 0.0), axis=1, keepdims=True)
    y0 = _unpack_rows(buf_ref[slot, 0], F32)
    y1 = _unpack_rows(buf_ref[slot, 1], F32)
    o_ref[...] = x1_ref[...] + (w0 * y0 + w1 * y1)


def _combine(dest, x1, route, ys, tt=256):
    T, D = x1.shape
    W = ys.shape[1]
    return pl.pallas_call(
        functools.partial(_combine_kernel, tt=tt),
        grid_spec=pltpu.PrefetchScalarGridSpec(
            num_scalar_prefetch=1,
            grid=(T // tt,),
            in_specs=[
                pl.BlockSpec((tt, D), lambda i, d: (i, 0)),
                pl.BlockSpec((tt, LANE), lambda i, d: (i, 0)),
                pl.BlockSpec(memory_space=pl.ANY),
            ],
            out_specs=pl.BlockSpec((tt, D), lambda i, d: (i, 0)),
            scratch_shapes=[pltpu.VMEM((2, 2, tt, W), ys.dtype), pltpu.SemaphoreType.DMA((2,))],
        ),
        out_shape=jax.ShapeDtypeStruct((T, D), F32),
        compiler_params=_params(("arbitrary",)),
        name="combine",
    )(dest, x1, route, ys)


def _moe(x1, hnp, route, counts8, wg_all, wu_all, wd_all, layer, tm=512):
    T = x1.shape[0]
    E = wg_all.shape[1]
    counts = counts8[0, :E].astype(jnp.int32)
    eid = route[:, ROUTE_E:ROUTE_E + 2].astype(jnp.int32)
    rank = route[:, ROUTE_RANK:ROUTE_RANK + 2].astype(jnp.int32)
    start = jnp.cumsum(counts) - counts
    onehot = eid[..., None] == jnp.arange(E, dtype=jnp.int32)
    dest = (jnp.sum(jnp.where(onehot, start, 0), axis=-1) + rank).reshape(2 * T)
    xs = _dispatch(dest, hnp)
    ys = _experts(_work_items(counts, 2 * T, tm), xs, wg_all, wu_all, wd_all, layer, tm)
    return _combine(dest, x1, route, ys)


def kernel(x, norm1_w, w_in, pool_w, pool_scale, conv_w, cmp_pe_k, cmp_w1_k, cmp_w2_k, cmp_pe_v, cmp_w1_v, cmp_w2_v, q_norm_w, k_norm_w, w_out, norm2_w, router_grp_w, router_grp_b, router_exp_w, router_exp_b, exp_w_gate, exp_w_up, exp_w_down):
    B, S, D = x.shape
    depth = w_in.shape[0]
    T = B * S
    xf = x.reshape(T, D)

    def w1_pair(w1):
        return jnp.concatenate([w1[:CMP_STRIDE], w1[CMP_STRIDE:]], axis=-1).astype(BF16)

    def pe_rows(pe):
        return jnp.broadcast_to(pe.reshape(1, CMP_LEN * HEAD_DIM), (8, CMP_LEN * HEAD_DIM)).astype(BF16)

    w_in_all = w_in.astype(BF16)
    w_gate_all = jnp.pad(w_in[:, :, COL_GATE:], ((0, 0), (0, 0), (0, LANE - (D_IN - COL_GATE)))).astype(BF16)
    w_out_all = w_out.astype(BF16)

    for l in range(depth):
        z, zgate = _inproj(xf, norm1_w[l].reshape(1, D), w_in_all, w_gate_all, l)

        yab = _mix_ab(z, pool_w[l].astype(BF16), pool_scale[l].reshape(1, POOL_WIDTH), conv_w[l], B, S)

        kc, vc = _compress(
            z, w1_pair(cmp_w1_k[l]), w1_pair(cmp_w1_v[l]), pe_rows(cmp_pe_k[l]), pe_rows(cmp_pe_v[l]),
            cmp_w1_k[l].reshape(CMP_LEN * HEAD_DIM, CMP_HIDDEN).astype(BF16),
            cmp_w1_v[l].reshape(CMP_LEN * HEAD_DIM, CMP_HIDDEN).astype(BF16),
            cmp_w2_k[l].astype(BF16), cmp_w2_v[l].astype(BF16), k_norm_w[l], B, S)
        yc = _nsa(z, zgate, kc, vc, q_norm_w[l], k_norm_w[l], B, S)

        wr = jnp.concatenate([router_exp_w[l], router_grp_w[l]], axis=1)
        wr = jnp.pad(wr, ((0, 0), (0, LANE - wr.shape[1])))
        wr_hi = wr.astype(BF16)
        wr_lo = (wr - wr_hi.astype(F32)).astype(BF16)
        br = jnp.concatenate([router_exp_b[l], router_grp_b[l]])
        br = jnp.pad(br, (0, LANE - br.shape[0])).reshape(1, LANE)
        x1, hnp, route, counts8 = _outproj_router(yab, yc, w_out_all, l, xf, norm2_w[l].reshape(1, D),
                                                  wr_hi, wr_lo, br)

        xf = _moe(x1, hnp, route, counts8, exp_w_gate, exp_w_up, exp_w_down, l)
    return xf.reshape(B, S, D)
```
